```python
import math
import jax, jax.numpy as jnp
from jax import lax
import numpy as np

D_MODEL = 1024
BATCH = 8
SEQ = 2048
DEPTH = 2
DEC_BATCH = 8
DEC_SEQ = 32
PAST_LEN = 2048

CHUNK = 64
Q_BLOCK = 128
EPS = 1e-6
DA_HEADS = 8
DA_HEAD_DIM = 64
DA_QK_DIM = 2 * DA_HEAD_DIM
DA_V_DIM = 2 * DA_HEAD_DIM
DA_WIDTH = DA_HEADS * DA_V_DIM
ROPE_THETA = 500000.0
ROT_DIM = DA_HEAD_DIM // 4
RW_HEAD_DIM = 64
RW_HEADS = D_MODEL // RW_HEAD_DIM
RW_WIDTH = RW_HEADS * RW_HEAD_DIM
DECAY_LORA = 64
AAA_LORA = 64
GATE_LORA = 128
GN_EPS = 64e-5
D_FF = 2816
CONV_W = 3
P_DA = 3 * DA_WIDTH
RW_OFF_R = 0
RW_OFF_K = RW_WIDTH
RW_OFF_V = 2 * RW_WIDTH
RW_OFF_WD = 3 * RW_WIDTH
RW_OFF_AD = RW_OFF_WD + DECAY_LORA
RW_OFF_GD = RW_OFF_AD + AAA_LORA
P_RW = RW_OFF_GD + GATE_LORA
P_GATE = 2 * D_MODEL
P_TOTAL = P_DA + P_RW + P_GATE

kernel_name = 'diff_rwkv7_convffn_stream_step'


def rmsnorm(x, g):
    xf = x.astype(jnp.float32)
    y = xf * lax.rsqrt(jnp.mean(xf * xf, axis=-1, keepdims=True) + EPS)
    return (y * g.astype(jnp.float32)).astype(x.dtype)


def partial_rope(x, pos):
    inv = ROPE_THETA ** (-jnp.arange(0, ROT_DIM, 2, dtype=jnp.float32) / ROT_DIM)
    ang = pos.astype(jnp.float32)[:, None] * inv[None, :]
    cos = jnp.cos(ang)[None, :, None, None, :]
    sin = jnp.sin(ang)[None, :, None, None, :]
    xr = x[..., :ROT_DIM].astype(jnp.float32)
    x1, x2 = xr[..., :ROT_DIM // 2], xr[..., ROT_DIM // 2:]
    rot = jnp.concatenate([x1 * cos - x2 * sin, x2 * cos + x1 * sin], axis=-1).astype(x.dtype)
    return jnp.concatenate([rot, x[..., ROT_DIM:]], axis=-1)


def diff_attend(q, k, v, lam, mask):
    s = jnp.einsum('bqhcd,bkhcd->bchqk', q.astype(jnp.float32), k.astype(jnp.float32)) * (DA_HEAD_DIM ** -0.5)
    if mask is not None:
        s = jnp.where(mask, s, -1e30)
    p = jax.nn.softmax(s, axis=-1)
    pd = p[:, 0] - lam * p[:, 1]
    return jnp.einsum('bhqk,bkhd->bqhd', pd, v.astype(jnp.float32)).astype(v.dtype)


def diff_attn_prompt(q, k, v, lam):
    B, T = q.shape[0], q.shape[1]
    nb = T // Q_BLOCK
    qb = jnp.moveaxis(q.reshape(B, nb, Q_BLOCK, DA_HEADS, 2, DA_HEAD_DIM), 1, 0)
    kchunk = jnp.arange(T, dtype=jnp.int32) // CHUNK

    def blk(args):
        qi, i = args
        qchunk = (i * Q_BLOCK + jnp.arange(Q_BLOCK, dtype=jnp.int32)) // CHUNK
        mask = kchunk[None, :] <= qchunk[:, None]
        return diff_attend(qi, k, v, lam, mask)

    out = lax.map(blk, (qb, jnp.arange(nb, dtype=jnp.int32)))
    return jnp.moveaxis(out, 0, 1).reshape(B, T, DA_HEADS, DA_V_DIM)


def heads(t):
    return t.reshape(t.shape[0], t.shape[1], RW_HEADS, RW_HEAD_DIM).astype(jnp.float32)


def wkv7_scan(S0, r, w, k, v, kk, a):
    def step(S, inp):
        r_t, w_t, k_t, v_t, kk_t, a_t = inp
        sa = jnp.einsum('bhvk,bhk->bhv', S, -kk_t)
        S = (S * w_t[:, :, None, :] + sa[..., None] * (kk_t * a_t)[:, :, None, :]
             + v_t[..., None] * k_t[:, :, None, :])
        y = jnp.einsum('bhvk,bhk->bhv', S, r_t)
        return S, y

    xs = tuple(jnp.moveaxis(t, 1, 0) for t in (r, w, k, v, kk, a))
    S, ys = lax.scan(step, S0.astype(jnp.float32), xs)
    return S, jnp.moveaxis(ys, 0, 1)


def layer(x, pos, lidx, past_k, past_v, wkv0, shift0, conv0,
          norm_mix, w_in, da_lambda, da_subln, w_o_da,
          rw_mu, rw_w0, rw_w2, rw_a0, rw_a2, rw_g2, rw_k_k, rw_k_a, rw_r_k,
          rw_ln_w, rw_ln_b, w_o_rw, w_out, norm_ffn, w_up, ffn_conv, ffn_conv_b, w_down):
    B, T, _ = x.shape
    h = rmsnorm(x, norm_mix)
    proj = h @ w_in
    u_da = proj[..., :P_DA]
    u_rw = proj[..., P_DA:P_DA + P_RW]
    u_gate = proj[..., P_DA + P_RW:]

    q = partial_rope(u_da[..., :DA_WIDTH].reshape(B, T, DA_HEADS, 2, DA_HEAD_DIM), pos)
    k = partial_rope(u_da[..., DA_WIDTH:2 * DA_WIDTH].reshape(B, T, DA_HEADS, 2, DA_HEAD_DIM), pos)
    v = u_da[..., 2 * DA_WIDTH:].reshape(B, T, DA_HEADS, DA_V_DIM)
    lam_init = 0.8 - 0.6 * math.exp(-0.3 * lidx)
    lp = da_lambda.astype(jnp.float32)
    lam = jnp.exp(jnp.sum(lp[0] * lp[1])) - jnp.exp(jnp.sum(lp[2] * lp[3])) + lam_init
    if past_k is None:
        o = diff_attn_prompt(q, k, v, lam)
    else:
        k_all = jnp.concatenate([past_k.reshape(B, -1, DA_HEADS, 2, DA_HEAD_DIM).astype(k.dtype), k], axis=1)
        v_all = jnp.concatenate([past_v.astype(v.dtype), v], axis=1)
        o = diff_attend(q, k_all, v_all, lam, None)
    o = rmsnorm(o, da_subln) * (1.0 - lam_init)
    o_da = o.reshape(B, T, DA_WIDTH) @ w_o_da
    new_k = k.reshape(B, T, DA_HEADS, DA_QK_DIM)

    u_prev = jnp.concatenate([shift0.astype(u_rw.dtype), u_rw[:, :-1]], axis=1)
    us = (u_rw + (u_prev - u_rw) * rw_mu).astype(jnp.float32)
    new_shift = u_rw[:, -1:]
    r = heads(us[..., RW_OFF_R:RW_OFF_K])
    kr = us[..., RW_OFF_K:RW_OFF_V]
    vr = heads(us[..., RW_OFF_V:RW_OFF_WD])
    wd = us[..., RW_OFF_WD:RW_OFF_AD]
    ad = us[..., RW_OFF_AD:RW_OFF_GD]
    gd = us[..., RW_OFF_GD:]
    w_log = -jax.nn.softplus(-(rw_w0 + jnp.tanh(wd) @ rw_w2)) - 0.5
    decay = heads(jnp.exp(-jnp.exp(w_log)))
    a = jax.nn.sigmoid(rw_a0 + ad @ rw_a2)
    g = jax.nn.sigmoid(gd) @ rw_g2
    kk = heads(kr * rw_k_k)
    kk = kk * lax.rsqrt(jnp.maximum(jnp.sum(kk * kk, axis=-1, keepdims=True), 1e-24))
    kmod = heads(kr * (1.0 + (a - 1.0) * rw_k_a))
    S, y = wkv7_scan(wkv0, r, decay, kmod, vr, kk, heads(a))
    mu = jnp.mean(y, axis=-1, keepdims=True)
    var = jnp.mean(jnp.square(y - mu), axis=-1, keepdims=True)
    yn = ((y - mu) * lax.rsqrt(var + GN_EPS)).reshape(B, T, RW_WIDTH) * rw_ln_w + rw_ln_b
    bonus = (jnp.sum(r * kmod * rw_r_k, axis=-1, keepdims=True) * vr).reshape(B, T, RW_WIDTH)
    o_rw = ((yn + bonus) * g).astype(x.dtype) @ w_o_rw

    g_da = jax.nn.sigmoid(u_gate[..., :D_MODEL])
    g_rw = jax.nn.sigmoid(u_gate[..., D_MODEL:])
    x = x + (g_da * o_da + g_rw * o_rw) @ w_out

    h2 = rmsnorm(x, norm_ffn)
    up = h2 @ w_up
    hp = jnp.concatenate([conv0.astype(up.dtype), up], axis=1)
    c = ffn_conv_b + hp[:, 0:T] * ffn_conv[0]
    for j in range(1, CONV_W):
        c = c + hp[:, j:j + T] * ffn_conv[j]
    x = x + (jax.nn.silu(c[..., :D_FF]) * c[..., D_FF:]) @ w_down
    new_conv = hp[:, -(CONV_W - 1):]
    return x, new_k, v, S.astype(wkv0.dtype), new_shift, new_conv


def setup_inputs(seed: int = 0) -> dict:
    key = jax.random.key(seed)
    ks = jax.random.split(key, 32)
    f32 = jnp.float32

    def nrm(i, shape, scale):
        return jax.random.normal(ks[i], shape, f32) * scale

    return {
        'x_prompt': nrm(0, (BATCH, SEQ, D_MODEL), 1.0),
        'x_sample': nrm(1, (DEC_BATCH, DEC_SEQ, D_MODEL), 1.0),
        'cache_k': nrm(2, (DEPTH, DEC_BATCH, PAST_LEN, DA_HEADS, DA_QK_DIM), 1.0),
        'cache_v': nrm(3, (DEPTH, DEC_BATCH, PAST_LEN, DA_HEADS, DA_V_DIM), 1.0),
        'state_wkv': nrm(4, (DEPTH, DEC_BATCH, RW_HEADS, RW_HEAD_DIM, RW_HEAD_DIM), 0.3),
        'state_shift': nrm(5, (DEPTH, DEC_BATCH, 1, P_RW), 1.0),
        'state_ffn_conv': nrm(6, (DEPTH, DEC_BATCH, CONV_W - 1, 2 * D_FF), 1.0),
        'norm_mix': 1.0 + nrm(7, (DEPTH, D_MODEL), 0.02),
        'w_in': nrm(8, (DEPTH, D_MODEL, P_TOTAL), D_MODEL ** -0.5),
        'da_lambda': nrm(9, (DEPTH, 4, DA_HEAD_DIM), 0.1),
        'da_subln': 1.0 + nrm(10, (DEPTH, DA_V_DIM), 0.02),
        'w_o_da': nrm(11, (DEPTH, DA_WIDTH, D_MODEL), DA_WIDTH ** -0.5),
        'rw_mu': 0.5 + nrm(12, (DEPTH, P_RW), 0.1),
        'rw_w0': -1.0 + nrm(13, (DEPTH, RW_WIDTH), 0.5),
        'rw_w2': nrm(14, (DEPTH, DECAY_LORA, RW_WIDTH), 0.1),
        'rw_a0': nrm(15, (DEPTH, RW_WIDTH), 0.1),
        'rw_a2': nrm(16, (DEPTH, AAA_LORA, RW_WIDTH), 0.1),
        'rw_g2': nrm(17, (DEPTH, GATE_LORA, RW_WIDTH), GATE_LORA ** -0.5),
        'rw_k_k': 0.85 + nrm(18, (DEPTH, RW_WIDTH), 0.05),
        'rw_k_a': 1.0 + nrm(19, (DEPTH, RW_WIDTH), 0.05),
        'rw_r_k': nrm(20, (DEPTH, RW_HEADS, RW_HEAD_DIM), 0.1),
        'rw_ln_w': 1.0 + nrm(21, (DEPTH, RW_WIDTH), 0.02),
        'rw_ln_b': nrm(22, (DEPTH, RW_WIDTH), 0.01),
        'w_o_rw': nrm(23, (DEPTH, RW_WIDTH, D_MODEL), RW_WIDTH ** -0.5),
        'w_out': nrm(24, (DEPTH, D_MODEL, D_MODEL), D_MODEL ** -0.5),
        'norm_ffn': 1.0 + nrm(25, (DEPTH, D_MODEL), 0.02),
        'w_up': nrm(26, (DEPTH, D_MODEL, 2 * D_FF), D_MODEL ** -0.5),
        'ffn_conv': nrm(27, (DEPTH, CONV_W, 2 * D_FF), 0.5),
        'ffn_conv_b': nrm(28, (DEPTH, 2 * D_FF), 0.01),
        'w_down': nrm(29, (DEPTH, D_FF, D_MODEL), D_FF ** -0.5),
        'norm_final': 1.0 + nrm(30, (D_MODEL,), 0.02),
    }


def reference(x_prompt, x_sample, cache_k, cache_v, state_wkv, state_shift, state_ffn_conv,
              norm_mix, w_in, da_lambda, da_subln, w_o_da,
              rw_mu, rw_w0, rw_w2, rw_a0, rw_a2, rw_g2, rw_k_k, rw_k_a, rw_r_k,
              rw_ln_w, rw_ln_b, w_o_rw, w_out, norm_ffn, w_up, ffn_conv, ffn_conv_b, w_down,
              norm_final):
    Bp, Tp, _ = x_prompt.shape
    Bs, Ts, _ = x_sample.shape
    past = cache_k.shape[2]
    pos_p = jnp.arange(Tp, dtype=jnp.int32)
    pos_s = past + jnp.arange(Ts, dtype=jnp.int32)
    wkv_zero = jnp.zeros((Bp, RW_HEADS, RW_HEAD_DIM, RW_HEAD_DIM), x_prompt.dtype)
    shift_zero = jnp.zeros((Bp, 1, P_RW), x_prompt.dtype)
    conv_zero = jnp.zeros((Bp, CONV_W - 1, 2 * D_FF), x_prompt.dtype)

    xp, xs = x_prompt, x_sample
    outs_p, outs_s = [], []
    for l in range(DEPTH):
        wl = (norm_mix[l], w_in[l], da_lambda[l], da_subln[l], w_o_da[l],
              rw_mu[l], rw_w0[l], rw_w2[l], rw_a0[l], rw_a2[l], rw_g2[l], rw_k_k[l], rw_k_a[l], rw_r_k[l],
              rw_ln_w[l], rw_ln_b[l], w_o_rw[l], w_out[l], norm_ffn[l], w_up[l], ffn_conv[l], ffn_conv_b[l], w_down[l])
        xp, kp, vp, sp, shp, cp = layer(xp, pos_p, l, None, None, wkv_zero, shift_zero, conv_zero, *wl)
        xs, kq, vq, sq, shq, cq = layer(xs, pos_s, l, cache_k[l], cache_v[l], state_wkv[l],
                                        state_shift[l], state_ffn_conv[l], *wl)
        outs_p.append((kp, vp, sp, shp, cp))
        outs_s.append((kq, vq, sq, shq, cq))

    k_prompt, v_prompt, wkv_prompt, shift_prompt, conv_prompt = [jnp.stack(t) for t in zip(*outs_p)]
    k_sample, v_sample, wkv_sample, shift_sample, conv_sample = [jnp.stack(t) for t in zip(*outs_s)]
    y_prompt = rmsnorm(xp, norm_final)
    y_sample = rmsnorm(xs, norm_final)
    return (y_prompt, y_sample, k_prompt, v_prompt, wkv_prompt, shift_prompt, conv_prompt,
            k_sample, v_sample, wkv_sample, shift_sample, conv_sample)
```

```python
import functools
import math

import jax
import jax.numpy as jnp
from jax import lax
from jax.experimental import pallas as pl
from jax.experimental.pallas import tpu as pltpu

F32 = jnp.float32
BF16 = jnp.bfloat16

D_MODEL = 1024
CHUNK = 64
EPS = 1e-6
DA_HEADS = 8
DA_HEAD_DIM = 64
DA_V_DIM = 128
DA_WIDTH = DA_HEADS * DA_V_DIM
ROPE_THETA = 500000.0
ROT_DIM = DA_HEAD_DIM // 4
RW_HEAD_DIM = 64
RW_HEADS = 16
RW_WIDTH = 1024
DECAY_LORA = 64
AAA_LORA = 64
GATE_LORA = 128
GN_EPS = 64e-5
D_FF = 2816
CONV_W = 3
P_DA = 3 * DA_WIDTH
P_RW = 3 * RW_WIDTH + DECAY_LORA + AAA_LORA + GATE_LORA
P_GATE = 2 * D_MODEL
P_TOTAL = P_DA + P_RW + P_GATE

LANES = 128
SUBLANES = 8
MXU_N = 256
VMEM_LIMIT = 56 * 1024 * 1024

WKV_CHUNK = 64
NEG_BIG = -1e30

_ARB = "arbitrary"


def _cparams(n_axes):
    return pltpu.CompilerParams(dimension_semantics=(_ARB,) * n_axes,
                                vmem_limit_bytes=VMEM_LIMIT)


def _resident(shape):
    nd = len(shape)
    return pl.BlockSpec(shape, lambda *_: (0,) * nd, pipeline_mode=pl.Buffered(1))


def _dot(a, b):
    return jnp.dot(a, b, preferred_element_type=F32)


def _dot_nt(a, b):
    return lax.dot_general(a, b, (((1,), (1,)), ((), ())), preferred_element_type=F32)


def _rms(x, g):
    return x * lax.rsqrt(jnp.mean(x * x, axis=-1, keepdims=True) + EPS) * g


def _sigmoid(x):
    return 1.0 / (1.0 + jnp.exp(-x))


_IN_CW = 256


def _in_proj_kernel(x_ref, g_ref, w_ref, rc_ref, ra_ref, rb_ref,
                    q_ref, k_ref, v_ref, u_ref, gt_ref, *, tm):
    x = x_ref[0]
    hb = _rms(x, g_ref[...]).astype(BF16)
    rc = rc_ref[...]
    ra = ra_ref[...]
    rb = rb_ref[...]

    def rope(t):
        return t * rc + pltpu.roll(t, ROT_DIM // 2, 1) * ra + pltpu.roll(t, LANES - ROT_DIM // 2, 1) * rb

    for c0 in range(0, P_TOTAL, _IN_CW):
        acc = _dot(hb, w_ref[:, c0:c0 + _IN_CW])
        if c0 < DA_WIDTH:
            for s in range(0, _IN_CW, LANES):
                q_ref[0, :, c0 + s:c0 + s + LANES] = (
                    rope(acc[:, s:s + LANES]) * (DA_HEAD_DIM ** -0.5)).astype(BF16)
        elif c0 < 2 * DA_WIDTH:
            o = c0 - DA_WIDTH
            for s in range(0, _IN_CW, LANES):
                k_ref[0, :, o + s:o + s + LANES] = rope(acc[:, s:s + LANES])
        elif c0 < P_DA:
            o = c0 - 2 * DA_WIDTH
            v_ref[0, :, o:o + _IN_CW] = acc
        elif c0 < P_DA + P_RW:
            o = c0 - P_DA
            u_ref[0, :, o:o + _IN_CW] = acc
        else:
            o = c0 - P_DA - P_RW
            gt_ref[0, :, o:o + _IN_CW] = acc


def _in_proj(x, g, w_bf, rc, ra, rb, *, tm):
    bx, t, _ = x.shape
    nt = t // tm
    row = lambda w: pl.BlockSpec((1, tm, w), lambda b, i: (b, i, 0))
    tab = pl.BlockSpec((tm, LANES), lambda b, i: (i, 0))
    return pl.pallas_call(
        functools.partial(_in_proj_kernel, tm=tm),
        grid=(bx, nt),
        in_specs=[row(D_MODEL), _resident((1, D_MODEL)), _resident((D_MODEL, P_TOTAL)), tab, tab, tab],
        out_specs=[row(DA_WIDTH), row(DA_WIDTH), row(DA_WIDTH), row(P_RW), row(P_GATE)],
        out_shape=[jax.ShapeDtypeStruct((bx, t, DA_WIDTH), BF16),
                   jax.ShapeDtypeStruct((bx, t, DA_WIDTH), F32),
                   jax.ShapeDtypeStruct((bx, t, DA_WIDTH), F32),
                   jax.ShapeDtypeStruct((bx, t, P_RW), F32),
                   jax.ShapeDtypeStruct((bx, t, P_GATE), F32)],
        compiler_params=_cparams(2),
        name="in_proj",
    )(x, g, w_bf, rc, ra, rb)


_TK = 256
_CHUNK_SHIFT = CHUNK.bit_length() - 1


def _attn_kernel(*refs, tq, n_new, n_past, n_valid_new, causal, lam_init):
    if n_past:
        lam_ref, sub_ref, q_ref, k_ref, v_ref, kp_ref, vp_ref, o_ref, kb_s, vt_s = refs
    else:
        lam_ref, sub_ref, q_ref, k_ref, v_ref, o_ref, kb_s, vt_s = refs
        kp_ref = vp_ref = None
    i = pl.program_id(2)

    @pl.when(i == 0)
    def _():
        for src_k, src_v, n, base in ((kp_ref, vp_ref, n_past, 0), (k_ref, v_ref, n_new, n_past)):
            for r0 in range(0, n, _TK):
                rn = min(_TK, n - r0)
                kb_s[base + r0:base + r0 + rn, :] = src_k[0, r0:r0 + rn, :].astype(BF16)
                vt_s[(base + r0) // _TK, :, 0:rn] = src_v[0, r0:r0 + rn, :].T.astype(BF16)

    lp = lam_ref[...]
    lam = (jnp.exp(jnp.sum(lp[0:1] * lp[1:2], axis=1, keepdims=True))
           - jnp.exp(jnp.sum(lp[2:3] * lp[3:4], axis=1, keepdims=True)) + lam_init)

    qt = q_ref[0].astype(F32).T.astype(BF16)
    z = jnp.zeros((DA_HEAD_DIM, tq), BF16)
    qbd = jnp.concatenate([jnp.concatenate([qt[:DA_HEAD_DIM], z], axis=1),
                           jnp.concatenate([z, qt[DA_HEAD_DIM:]], axis=1)], axis=0)

    def block(carry, kblk, vtblk, mask):
        s = _dot(kblk, qbd)
        out = []
        for c in range(2):
            m, l, acc = carry[3 * c:3 * c + 3]
            sc = s[:, c * tq:(c + 1) * tq]
            if mask is not None:
                sc = jnp.where(mask, sc, NEG_BIG)
            m_new = jnp.maximum(m, jnp.max(sc, axis=0, keepdims=True))
            alpha = jnp.exp(m - m_new)
            p = jnp.exp(sc - m_new)
            l = alpha * l + jnp.sum(p, axis=0, keepdims=True)
            acc = alpha * acc + _dot(vtblk, p.astype(BF16))
            out += [m_new, l, acc]
        return tuple(out)

    init = []
    for _ in range(2):
        init += [jnp.full((1, tq), NEG_BIG, F32), jnp.zeros((1, tq), F32), jnp.zeros((DA_V_DIM, tq), F32)]
    carry = tuple(init)

    if causal:
        def body(kb, c):
            r0 = pl.multiple_of(kb * _TK, _TK)
            return block(c, kb_s[pl.ds(r0, _TK), :], vt_s[kb], None)
        carry = lax.fori_loop(0, i, body, carry)
        r0 = pl.multiple_of(i * _TK, _TK)
        kr = jnp.right_shift(lax.broadcasted_iota(jnp.int32, (_TK, tq), 0), _CHUNK_SHIFT)
        qc = jnp.right_shift(lax.broadcasted_iota(jnp.int32, (_TK, tq), 1), _CHUNK_SHIFT)
        carry = block(carry, kb_s[pl.ds(r0, _TK), :], vt_s[i], kr <= qc)
    else:
        for r0 in range(0, n_past, _TK):
            carry = block(carry, kb_s[r0:r0 + _TK, :], vt_s[r0 // _TK], None)
        valid = lax.broadcasted_iota(jnp.int32, (n_new, tq), 0) < n_valid_new
        carry = block(carry, kb_s[n_past:n_past + n_new, :], vt_s[n_past // _TK, :, 0:n_new], valid)

    m1, l1, a1, m2, l2, a2 = carry
    ot = a1 * (1.0 / l1) - lam * (a2 * (1.0 / l2))
    ss = jnp.sum(ot * ot, axis=0, keepdims=True) * (1.0 / DA_V_DIM)
    ot = ot * lax.rsqrt(ss + EPS) * sub_ref[...] * (1.0 - lam_init)
    o_ref[0] = ot.T.astype(BF16)


def _attn(da_lambda, subln_col, q, k, v, past_k, past_v, *, tq, causal, lam_init, n_valid_new):
    bx, t, _ = q.shape
    n_new = k.shape[1]
    n_past = 0 if past_k is None else past_k.shape[1]
    nq = t // tq
    head = lambda rows, at: pl.BlockSpec((1, rows, LANES), at)
    in_specs = [_resident((4, DA_HEAD_DIM)), _resident((DA_V_DIM, 1)),
                head(tq, lambda b, h, i: (b, i, h)),
                head(n_new, lambda b, h, i: (b, 0, h)),
                head(n_new, lambda b, h, i: (b, 0, h))]
    args = [da_lambda, subln_col, q, k, v]
    if n_past:
        in_specs += [head(n_past, lambda b, h, i: (b, 0, h))] * 2
        args += [past_k, past_v]
    return pl.pallas_call(
        functools.partial(_attn_kernel, tq=tq, n_new=n_new, n_past=n_past,
                          n_valid_new=n_valid_new, causal=causal, lam_init=lam_init),
        grid=(bx, DA_HEADS, nq),
        in_specs=in_specs,
        out_specs=head(tq, lambda b, h, i: (b, i, h)),
        out_shape=jax.ShapeDtypeStruct((bx, t, DA_WIDTH), BF16),
        scratch_shapes=[pltpu.VMEM((n_past + n_new, LANES), BF16),
                        pltpu.VMEM((pl.cdiv(n_past + n_new, _TK), DA_V_DIM, _TK), BF16)],
        compiler_params=_cparams(3),
        name="diff_attn",
    )(*args)


def _split3(x):
    hi = x.astype(BF16)
    r1 = x - hi.astype(F32)
    mid = r1.astype(BF16)
    lo = (r1 - mid.astype(F32)).astype(BF16)
    return hi, mid, lo


def _rw_prep_kernel(u_ref, halo_ref, sh_ref, mu_ref, w0_ref, w2_ref, a0_ref, a2_ref, g2_ref,
                    kk_ref, ka_ref, seg_ref, tri_ref,
                    rt_ref, at_ref, bt_ref, kt_ref, v_ref, g_ref, wc_ref, *, tm, chunk):
    i = pl.program_id(1)
    u = u_ref[0]
    prev = jnp.where(i == 0, sh_ref[0], halo_ref[0, SUBLANES - 1:SUBLANES, :])
    rows = lax.broadcasted_iota(jnp.int32, (tm, 1), 0)
    up = jnp.where(rows == 0, prev, pltpu.roll(u, 1, 0))
    us = u + (up - u) * mu_ref[...]

    r = us[:, 0:RW_WIDTH]
    kr = us[:, RW_WIDTH:2 * RW_WIDTH]
    vr = us[:, 2 * RW_WIDTH:3 * RW_WIDTH]
    wa = us[:, 3 * RW_WIDTH:3 * RW_WIDTH + DECAY_LORA + AAA_LORA]
    gd = us[:, 3 * RW_WIDTH + DECAY_LORA + AAA_LORA:]

    z = w0_ref[...] + _dot(jnp.tanh(wa).astype(BF16), w2_ref[...])
    nz = -z
    w_log = -(jnp.maximum(nz, 0.0) + jnp.log1p(jnp.exp(-jnp.abs(nz)))) - 0.5
    lw = -jnp.exp(w_log)
    a = _sigmoid(a0_ref[...] + _dot(wa.astype(BF16), a2_ref[...]))
    g_ref[0] = _dot(_sigmoid(gd).astype(BF16), g2_ref[...])
    v_ref[0] = vr

    kk = kr * kk_ref[...]
    k2 = kk * kk
    k2h = k2.astype(BF16)
    k2l = (k2 - k2h.astype(F32)).astype(BF16)
    seg = seg_ref[...]
    ssum = jnp.concatenate(
        [_dot(k2h[:, c:c + LANES], seg) + _dot(k2l[:, c:c + LANES], seg) for c in range(0, RW_WIDTH, LANES)],
        axis=1)
    kk = kk * lax.rsqrt(jnp.maximum(ssum, 1e-24))
    kmod = kr * (1.0 + (a - 1.0) * ka_ref[...])

    tri = tri_ref[...]
    for c in range(tm // chunk):
        sl = slice(c * chunk, (c + 1) * chunk)
        lwc = lw[sl]
        hi, mid, lo = _split3(lwc)
        cum = _dot(tri, hi) + _dot(tri, mid) + _dot(tri, lo)
        e_pos = jnp.exp(cum)
        e_neg = jnp.exp(-cum)
        rt_ref[0, sl, :] = r[sl] * e_pos
        at_ref[0, sl, :] = -kk[sl] * jnp.exp(cum - lwc)
        bt_ref[0, sl, :] = kk[sl] * a[sl] * e_neg
        kt_ref[0, sl, :] = kmod[sl] * e_neg
        wc_ref[0, c] = e_pos[chunk - 1:chunk, :]


def _rw_prep(u, shift0, mu, w0, w2p, a0, a2p, g2, k_k, k_a, seg, tri, *, tm, chunk):
    bx, t, _ = u.shape
    nt = t // tm
    row = lambda w: pl.BlockSpec((1, tm, w), lambda b, i: (b, i, 0))
    halo = pl.BlockSpec((1, SUBLANES, P_RW),
                        lambda b, i: (b, jnp.maximum(i * (tm // SUBLANES) - 1, 0), 0))
    outs = [jax.ShapeDtypeStruct((bx, t, RW_WIDTH), F32)] * 6
    outs.append(jax.ShapeDtypeStruct((bx, t // chunk, 1, RW_WIDTH), F32))
    return pl.pallas_call(
        functools.partial(_rw_prep_kernel, tm=tm, chunk=chunk),
        grid=(bx, nt),
        in_specs=[row(P_RW), halo, pl.BlockSpec((1, 1, P_RW), lambda b, i: (b, 0, 0)),
                  _resident((1, P_RW)), _resident((1, RW_WIDTH)),
                  _resident((DECAY_LORA + AAA_LORA, RW_WIDTH)), _resident((1, RW_WIDTH)),
                  _resident((DECAY_LORA + AAA_LORA, RW_WIDTH)), _resident((GATE_LORA, RW_WIDTH)),
                  _resident((1, RW_WIDTH)), _resident((1, RW_WIDTH)),
                  _resident((LANES, LANES)), _resident((chunk, chunk))],
        out_specs=[row(RW_WIDTH)] * 6 + [pl.BlockSpec((1, tm // chunk, 1, RW_WIDTH), lambda b, i: (b, i, 0, 0))],
        out_shape=outs,
        compiler_params=_cparams(2),
        name="rw_prep",
    )(u, u, shift0, mu, w0, w2p, a0, a2p, g2, k_k, k_a, seg, tri)


def _wkv_kernel(rt_ref, at_ref, bt_ref, kt_ref, v_ref, g_ref, wc_ref, s0_ref, rk_ref, lnw_ref, lnb_ref,
                o_ref, sout_ref, s_scr, rp_s, m_s, y0_s, gc_s, bon_s, *, n_chunks):
    C = WKV_CHUNK
    R = 2 * C
    t = pl.program_id(2)

    @pl.when(t == 0)
    def _():
        s_scr[...] = s0_ref[0, 0]

    ri = lax.broadcasted_iota(jnp.int32, (R, R), 0)
    ci = lax.broadcasted_iota(jnp.int32, (R, R), 1)
    sh = lambda x, sz: jnp.right_shift(x, sz.bit_length() - 1)
    same_head = sh(ri, C) == sh(ci, C)
    strict = same_head & (ci < ri)
    incl = same_head & (ci <= ri)
    eye = ri == ci

    def bd(sz):
        return sh(ri, sz) == sh(ci, sz)

    lane_lo = lax.broadcasted_iota(jnp.int32, (C, LANES), 1) < RW_HEAD_DIM

    def stack(x):
        return jnp.concatenate([jnp.where(lane_lo, x, 0.0), jnp.where(lane_lo, 0.0, x)], axis=0)

    rk = rk_ref[...]
    lnw = lnw_ref[...]
    lnb = lnb_ref[...]
    b16 = lambda x: x.astype(BF16)
    ident = jnp.where(eye, 1.0, 0.0)

    for c in range(n_chunks):
        sl = slice(c * C, (c + 1) * C)
        rs = stack(rt_ref[0, sl, :])
        as_ = stack(at_ref[0, sl, :])
        bs = stack(bt_ref[0, sl, :])
        ks = stack(kt_ref[0, sl, :])
        vs = stack(v_ref[0, sl, :])
        wc = wc_ref[0, c]

        ab = _dot_nt(b16(jnp.concatenate([as_, rs], axis=0)), b16(jnp.concatenate([bs, ks], axis=0)))
        n = jnp.where(strict, ab[:R, :R], 0.0)
        aak = jnp.where(strict, ab[:R, R:], 0.0)
        arb = jnp.where(incl, ab[R:, :R], 0.0)
        ark = jnp.where(incl, ab[R:, R:], 0.0)

        n8 = jnp.where(bd(8), n, 0.0)
        n8b = b16(n8)
        n2 = _dot(n8b, n8b)
        n2b = b16(n2)
        n4 = _dot(n2b, n2b)
        tinv = _dot(b16(_dot(b16(ident + n8), b16(ident + n2))), b16(ident + n4))
        for sz in (8, 16, 32):
            noff = jnp.where(bd(2 * sz) & jnp.logical_not(bd(sz)), n, 0.0)
            tb = b16(tinv)
            tinv = tinv + _dot(b16(_dot(tb, b16(noff))), tb)
        tb = b16(tinv)

        vb = b16(vs)
        u0 = _dot(tb, b16(_dot(b16(aak), vb)))
        ap = _dot(tb, b16(as_))
        arbb = b16(arb)
        rp_s[c] = b16(rs + _dot(arbb, b16(ap)))
        y0_s[c] = _dot(arbb, b16(u0)) + _dot(b16(ark), vb)
        bdk = b16(bs * wc)
        m_s[c] = b16(jnp.where(eye, wc, 0.0) + _dot(b16(ap.T), bdk))
        gc_s[c] = _dot(b16(u0.T), bdk) + _dot(b16(vs.T), b16(ks * wc))
        bon_s[c] = jnp.sum(rs * ks * rk, axis=1, keepdims=True) * vs

    s = s_scr[...]
    for c in range(n_chunks):
        sl = slice(c * C, (c + 1) * C)
        sb = b16(s)
        y = _dot_nt(rp_s[c], sb) + y0_s[c]
        s = _dot(sb, m_s[c]) + gc_s[c]
        mu = jnp.sum(y, axis=1, keepdims=True) * (1.0 / RW_HEAD_DIM)
        d = jnp.where(same_head, y - mu, 0.0)
        var = jnp.sum(d * d, axis=1, keepdims=True) * (1.0 / RW_HEAD_DIM)
        ost = d * lax.rsqrt(var + GN_EPS) * lnw + jnp.where(same_head, lnb, 0.0) + bon_s[c]
        o_ref[0, sl, :] = ((ost[:C] + ost[C:]) * g_ref[0, sl, :]).astype(BF16)
    s_scr[...] = s
    sout_ref[0, 0] = s


def _wkv(rt, at, bt, kt, v, g, wc, s0bd, rk, lnw, lnb, *, tb):
    bx, t, _ = rt.shape
    npair = RW_HEADS // 2
    n_chunks = tb // WKV_CHUNK
    seq = pl.BlockSpec((1, tb, LANES), lambda b, p, i: (b, i, p))
    vec = pl.BlockSpec((1, LANES), lambda b, p, i: (0, p))
    st = pl.BlockSpec((1, 1, LANES, LANES), lambda b, p, i: (b, p, 0, 0))
    return pl.pallas_call(
        functools.partial(_wkv_kernel, n_chunks=n_chunks),
        grid=(bx, npair, t // tb),
        in_specs=[seq] * 6 + [pl.BlockSpec((1, n_chunks, 1, LANES), lambda b, p, i: (b, i, 0, p)),
                              st, vec, vec, vec],
        out_specs=[seq, st],
        out_shape=[jax.ShapeDtypeStruct((bx, t, RW_WIDTH), BF16),
                   jax.ShapeDtypeStruct((bx, npair, LANES, LANES), F32)],
        scratch_shapes=[pltpu.VMEM((LANES, LANES), F32),
                        pltpu.VMEM((n_chunks, LANES, LANES), BF16), pltpu.VMEM((n_chunks, LANES, LANES), BF16),
                        pltpu.VMEM((n_chunks, LANES, LANES), F32), pltpu.VMEM((n_chunks, LANES, LANES), F32),
                        pltpu.VMEM((n_chunks, LANES, LANES), F32)],
        compiler_params=_cparams(3),
        name="wkv",
    )(rt, at, bt, kt, v, g, wc, s0bd, rk, lnw, lnb)


def _merge_kernel(x_ref, oda_ref, orw_ref, gt_ref, wda_ref, wrw_ref, wout_ref, o_ref):
    a = _dot(oda_ref[0], wda_ref[...])
    b = _dot(orw_ref[0], wrw_ref[...])
    gt = gt_ref[0]
    m = _sigmoid(gt[:, :D_MODEL]) * a + _sigmoid(gt[:, D_MODEL:]) * b
    o_ref[0] = x_ref[0] + _dot(m.astype(BF16), wout_ref[...])


def _merge(x, oda, orw, gt, wda, wrw, wout, *, tm):
    bx, t, _ = x.shape
    row = lambda w: pl.BlockSpec((1, tm, w), lambda b, i: (b, i, 0))
    sq = _resident((D_MODEL, D_MODEL))
    return pl.pallas_call(
        _merge_kernel,
        grid=(bx, t // tm),
        in_specs=[row(D_MODEL), row(DA_WIDTH), row(RW_WIDTH), row(P_GATE), sq, sq, sq],
        out_specs=row(D_MODEL),
        out_shape=jax.ShapeDtypeStruct((bx, t, D_MODEL), F32),
        compiler_params=_cparams(2),
        name="merge",
    )(x, oda, orw, gt, wda, wrw, wout)


_FF_CW = 256
_HALO = SUBLANES


def _ffn_kernel(x_ref, halo_ref, c0_ref, g_ref, wup_ref, f_ref, fb_ref, wdn_ref, gf_ref,
                o_ref, conv_ref, ext_a, ext_b, act_s, *, tm, final):
    i = pl.program_id(1)
    xe = jnp.concatenate([halo_ref[0], x_ref[0]], axis=0)
    hb = _rms(xe, g_ref[...]).astype(BF16)
    for c0 in range(0, D_FF, _FF_CW):
        for ext, base in ((ext_a, c0), (ext_b, D_FF + c0)):
            ext[...] = _dot(hb, wup_ref[:, base:base + _FF_CW])

            @pl.when(i == 0)
            def _(ext=ext, base=base):
                ext[_HALO - 2:_HALO, :] = c0_ref[0, :, base:base + _FF_CW]

            conv_ref[0, :, base:base + _FF_CW] = ext[_HALO + tm - 2:_HALO + tm, :]
        cs = []
        for ext, base in ((ext_a, c0), (ext_b, D_FF + c0)):
            f = f_ref[:, base:base + _FF_CW]
            cs.append(fb_ref[:, base:base + _FF_CW]
                      + ext[_HALO - 2:_HALO - 2 + tm, :] * f[0:1]
                      + ext[_HALO - 1:_HALO - 1 + tm, :] * f[1:2]
                      + ext[_HALO:_HALO + tm, :] * f[2:3])
        ca, cb = cs
        act_s[:, c0:c0 + _FF_CW] = (ca * _sigmoid(ca) * cb).astype(BF16)
    out = x_ref[0] + _dot(act_s[...], wdn_ref[...])
    if final:
        out = _rms(out, gf_ref[...])
    o_ref[0] = out


def _ffn(x, conv0, g, wup, f, fb, wdn, gfinal, *, tm, final):
    bx, t, _ = x.shape
    row = pl.BlockSpec((1, tm, D_MODEL), lambda b, i: (b, i, 0))
    halo = pl.BlockSpec((1, _HALO, D_MODEL),
                        lambda b, i: (b, jnp.maximum(i * (tm // _HALO) - 1, 0), 0))
    cst = pl.BlockSpec((1, CONV_W - 1, 2 * D_FF), lambda b, i: (b, 0, 0))
    return pl.pallas_call(
        functools.partial(_ffn_kernel, tm=tm, final=final),
        grid=(bx, t // tm),
        in_specs=[row, halo, cst, _resident((1, D_MODEL)), _resident((D_MODEL, 2 * D_FF)),
                  _resident((CONV_W, 2 * D_FF)), _resident((1, 2 * D_FF)), _resident((D_FF, D_MODEL)),
                  _resident((1, D_MODEL))],
        out_specs=[row, cst],
        out_shape=[jax.ShapeDtypeStruct((bx, t, D_MODEL), F32),
                   jax.ShapeDtypeStruct((bx, CONV_W - 1, 2 * D_FF), F32)],
        scratch_shapes=[pltpu.VMEM((_HALO + tm, _FF_CW), F32), pltpu.VMEM((_HALO + tm, _FF_CW), F32),
                        pltpu.VMEM((tm, D_FF), BF16)],
        compiler_params=_cparams(2),
        name="ffn",
    )(x, x, conv0, g, wup, f, fb, wdn, gfinal)


def _rope_tables(pos):
    half = ROT_DIM // 2
    inv = ROPE_THETA ** (-jnp.arange(0, ROT_DIM, 2, dtype=F32) / ROT_DIM)
    ang = pos.astype(F32)[:, None] * inv[None, :]
    cos, sin = jnp.cos(ang), jnp.sin(ang)
    t = pos.shape[0]
    pad = jnp.zeros((t, DA_HEAD_DIM - ROT_DIM), F32)
    z = jnp.zeros((t, half), F32)
    one_map = lambda a, b, fill: jnp.concatenate([a, b, pad + fill], axis=1)
    rc = one_map(cos, cos, 1.0)
    ra = one_map(z, sin, 0.0)
    rb = one_map(-sin, z, 0.0)
    dup = lambda m: jnp.concatenate([m, m], axis=1)
    return dup(rc), dup(ra), dup(rb)


def _state_to_blockdiag(s):
    b = s.shape[0]
    s = s.reshape(b, RW_HEADS // 2, 2, RW_HEAD_DIM, RW_HEAD_DIM)
    z = jnp.zeros_like(s[:, :, 0])
    top = jnp.concatenate([s[:, :, 0], z], axis=-1)
    bot = jnp.concatenate([z, s[:, :, 1]], axis=-1)
    return jnp.concatenate([top, bot], axis=-2)


def _blockdiag_to_state(sbd):
    b = sbd.shape[0]
    h0 = sbd[:, :, :RW_HEAD_DIM, :RW_HEAD_DIM]
    h1 = sbd[:, :, RW_HEAD_DIM:, RW_HEAD_DIM:]
    return jnp.stack([h0, h1], axis=2).reshape(b, RW_HEADS, RW_HEAD_DIM, RW_HEAD_DIM)


def _layer(x, lidx, pos, past_k, past_v, wkv0, shift0, conv0, w, norm_final, *, final):
    bx, t, _ = x.shape
    prompt = past_k is None
    tm = 256 if prompt else t
    lam_init = 0.8 - 0.6 * math.exp(-0.3 * lidx)

    rc, ra, rb = _rope_tables(pos)
    q, k, v, u_rw, gate = _in_proj(x, w["norm_mix"], w["w_in"], rc, ra, rb, tm=tm)

    if prompt:
        o_da = _attn(w["da_lambda"], w["da_subln"], q, k, v, None, None,
                     tq=_TK, causal=True, lam_init=lam_init, n_valid_new=t)
    else:
        padq = LANES - t
        qp = jnp.pad(q, ((0, 0), (0, padq), (0, 0)))
        kp = jnp.pad(k, ((0, 0), (0, padq), (0, 0)))
        vp = jnp.pad(v, ((0, 0), (0, padq), (0, 0)))
        o_da = _attn(w["da_lambda"], w["da_subln"], qp, kp, vp,
                     past_k.reshape(bx, -1, DA_WIDTH), past_v.reshape(bx, -1, DA_WIDTH),
                     tq=LANES, causal=False, lam_init=lam_init, n_valid_new=t)[:, :t]

    chunk = WKV_CHUNK if prompt else t
    rt, at, bt, kt, vr, g, wc = _rw_prep(
        u_rw, shift0, w["rw_mu"], w["rw_w0"], w["rw_w2p"], w["rw_a0"], w["rw_a2p"], w["rw_g2"],
        w["rw_k_k"], w["rw_k_a"], w["seg"], w["tri_p"] if prompt else w["tri_s"], tm=tm, chunk=chunk)
    if not prompt:
        padt = ((0, 0), (0, WKV_CHUNK - t), (0, 0))
        rt, at, bt, kt, vr, g = [jnp.pad(a, padt) for a in (rt, at, bt, kt, vr, g)]
    o_rw, sbd = _wkv(rt, at, bt, kt, vr, g, wc, _state_to_blockdiag(wkv0),
                     w["rw_r_k"], w["rw_ln_w"], w["rw_ln_b"], tb=512 if prompt else WKV_CHUNK)
    o_rw = o_rw[:, :t]

    x = _merge(x, o_da, o_rw, gate, w["w_o_da"], w["w_o_rw"], w["w_out"], tm=tm)
    x, new_conv = _ffn(x, conv0, w["norm_ffn"], w["w_up"], w["ffn_conv"], w["ffn_conv_b"], w["w_down"],
                       norm_final, tm=tm, final=final)
    new_k = k.reshape(bx, t, DA_HEADS, DA_V_DIM)
    new_v = v.reshape(bx, t, DA_HEADS, DA_V_DIM)
    return x, new_k, new_v, _blockdiag_to_state(sbd), u_rw[:, -1:], new_conv


def kernel(x_prompt, x_sample, cache_k, cache_v, state_wkv, state_shift, state_ffn_conv, norm_mix, w_in, da_lambda, da_subln, w_o_da, rw_mu, rw_w0, rw_w2, rw_a0, rw_a2, rw_g2, rw_k_k, rw_k_a, rw_r_k, rw_ln_w, rw_ln_b, w_o_rw, w_out, norm_ffn, w_up, ffn_conv, ffn_conv_b, w_down, norm_final):
    bp, tp, _ = x_prompt.shape
    bs, ts, _ = x_sample.shape
    depth = w_in.shape[0]
    past = cache_k.shape[2]
    pos_p = jnp.arange(tp, dtype=jnp.int32)
    pos_s = past + jnp.arange(ts, dtype=jnp.int32)

    lane = jnp.arange(LANES)
    seg = (lane[:, None] // RW_HEAD_DIM == lane[None, :] // RW_HEAD_DIM).astype(BF16)
    tri = lambda n: (jnp.arange(n)[:, None] >= jnp.arange(n)[None, :]).astype(BF16)
    zl = jnp.zeros((DECAY_LORA, RW_WIDTH), F32)
    row = lambda a: a.reshape(1, -1)

    xp, xs = x_prompt, x_sample
    outs_p, outs_s = [], []
    nf = row(norm_final)
    for l in range(depth):
        w = dict(
            norm_mix=row(norm_mix[l]), w_in=w_in[l].astype(BF16), da_lambda=da_lambda[l],
            da_subln=da_subln[l].reshape(-1, 1), w_o_da=w_o_da[l].astype(BF16),
            rw_mu=row(rw_mu[l]), rw_w0=row(rw_w0[l]),
            rw_w2p=jnp.concatenate([rw_w2[l], zl], axis=0).astype(BF16),
            rw_a0=row(rw_a0[l]), rw_a2p=jnp.concatenate([zl, rw_a2[l]], axis=0).astype(BF16),
            rw_g2=rw_g2[l].astype(BF16), rw_k_k=row(rw_k_k[l]), rw_k_a=row(rw_k_a[l]),
            rw_r_k=row(rw_r_k[l]), rw_ln_w=row(rw_ln_w[l]), rw_ln_b=row(rw_ln_b[l]),
            w_o_rw=w_o_rw[l].astype(BF16), w_out=w_out[l].astype(BF16), norm_ffn=row(norm_ffn[l]),
            w_up=w_up[l].astype(BF16), ffn_conv=ffn_conv[l], ffn_conv_b=row(ffn_conv_b[l]),
            w_down=w_down[l].astype(BF16), seg=seg, tri_p=tri(WKV_CHUNK), tri_s=tri(ts))
        final = l == depth - 1
        zero = lambda *s: jnp.zeros(s, x_prompt.dtype)
        xp, kp, vp, sp, shp, cp = _layer(
            xp, l, pos_p, None, None, zero(bp, RW_HEADS, RW_HEAD_DIM, RW_HEAD_DIM), zero(bp, 1, P_RW),
            zero(bp, CONV_W - 1, 2 * D_FF), w, nf, final=final)
        xs, kq, vq, sq, shq, cq = _layer(
            xs, l, pos_s, cache_k[l], cache_v[l], state_wkv[l], state_shift[l], state_ffn_conv[l],
            w, nf, final=final)
        outs_p.append((kp, vp, sp, shp, cp))
        outs_s.append((kq, vq, sq, shq, cq))

    k_prompt, v_prompt, wkv_prompt, shift_prompt, conv_prompt = [jnp.stack(t) for t in zip(*outs_p)]
    k_sample, v_sample, wkv_sample, shift_sample, conv_sample = [jnp.stack(t) for t in zip(*outs_s)]
    return (xp, xs, k_prompt, v_prompt, wkv_prompt, shift_prompt, conv_prompt,
            k_sample, v_sample, wkv_sample, shift_sample, conv_sample)
```

```python
import functools
import math

import jax
import jax.numpy as jnp
from jax import lax
from jax.experimental import pallas as pl
from jax.experimental.pallas import tpu as pltpu

F32 = jnp.float32
BF16 = jnp.bfloat16

D_MODEL = 1024
CHUNK = 64
EPS = 1e-6
DA_HEADS = 8
DA_HEAD_DIM = 64
DA_V_DIM = 128
DA_WIDTH = DA_HEADS * DA_V_DIM
ROPE_THETA = 500000.0
ROT_DIM = DA_HEAD_DIM // 4
RW_HEAD_DIM = 64
RW_HEADS = 16
RW_WIDTH = 1024
DECAY_LORA = 64
AAA_LORA = 64
GATE_LORA = 128
GN_EPS = 64e-5
D_FF = 2816
CONV_W = 3
P_DA = 3 * DA_WIDTH
P_RW = 3 * RW_WIDTH + DECAY_LORA + AAA_LORA + GATE_LORA
P_GATE = 2 * D_MODEL
P_TOTAL = P_DA + P_RW + P_GATE

LANES = 128
SUBLANES = 8
MXU_N = 256
VMEM_LIMIT = 56 * 1024 * 1024

WKV_CHUNK = 64
NEG_BIG = -1e30

_ARB = "arbitrary"


def _cparams(n_axes):
    return pltpu.CompilerParams(dimension_semantics=(_ARB,) * n_axes,
                                vmem_limit_bytes=VMEM_LIMIT)


def _resident(shape):
    nd = len(shape)
    return pl.BlockSpec(shape, lambda *_: (0,) * nd, pipeline_mode=pl.Buffered(1))


def _dot(a, b):
    return jnp.dot(a, b, preferred_element_type=F32)


def _dot_nt(a, b):
    return lax.dot_general(a, b, (((1,), (1,)), ((), ())), preferred_element_type=F32)


def _rms(x, g):
    return x * lax.rsqrt(jnp.mean(x * x, axis=-1, keepdims=True) + EPS) * g


def _sigmoid(x):
    return 1.0 / (1.0 + jnp.exp(-x))


_IN_CW = 256
_LOG2E = math.log2(math.e)


def _in_proj_kernel(x_ref, g_ref, w_ref, rc_ref, ra_ref, rb_ref,
                    q_ref, k_ref, v_ref, u_ref, gt_ref, *, tm):
    x = x_ref[0]
    hb = _rms(x, g_ref[...]).astype(BF16)
    rc = rc_ref[...]
    ra = ra_ref[...]
    rb = rb_ref[...]

    def rope(t):
        return t * rc + pltpu.roll(t, ROT_DIM // 2, 1) * ra + pltpu.roll(t, LANES - ROT_DIM // 2, 1) * rb

    for c0 in range(0, P_TOTAL, _IN_CW):
        acc = _dot(hb, w_ref[:, c0:c0 + _IN_CW])
        if c0 < DA_WIDTH:
            for s in range(0, _IN_CW, LANES):
                q_ref[0, :, c0 + s:c0 + s + LANES] = (
                    rope(acc[:, s:s + LANES]) * (DA_HEAD_DIM ** -0.5 * _LOG2E)).astype(BF16)
        elif c0 < 2 * DA_WIDTH:
            o = c0 - DA_WIDTH
            for s in range(0, _IN_CW, LANES):
                k_ref[0, :, o + s:o + s + LANES] = rope(acc[:, s:s + LANES])
        elif c0 < P_DA:
            o = c0 - 2 * DA_WIDTH
            v_ref[0, :, o:o + _IN_CW] = acc
        elif c0 < P_DA + P_RW:
            o = c0 - P_DA
            u_ref[0, :, o:o + _IN_CW] = acc
        else:
            o = c0 - P_DA - P_RW
            gt_ref[0, :, o:o + _IN_CW] = acc


def _in_proj(x, g, w_bf, rc, ra, rb, *, tm):
    bx, t, _ = x.shape
    nt = t // tm
    row = lambda w: pl.BlockSpec((1, tm, w), lambda b, i: (b, i, 0))
    tab = pl.BlockSpec((tm, LANES), lambda b, i: (i, 0))
    return pl.pallas_call(
        functools.partial(_in_proj_kernel, tm=tm),
        grid=(bx, nt),
        in_specs=[row(D_MODEL), _resident((1, D_MODEL)), _resident((D_MODEL, P_TOTAL)), tab, tab, tab],
        out_specs=[row(DA_WIDTH), row(DA_WIDTH), row(DA_WIDTH), row(P_RW), row(P_GATE)],
        out_shape=[jax.ShapeDtypeStruct((bx, t, DA_WIDTH), BF16),
                   jax.ShapeDtypeStruct((bx, t, DA_WIDTH), F32),
                   jax.ShapeDtypeStruct((bx, t, DA_WIDTH), F32),
                   jax.ShapeDtypeStruct((bx, t, P_RW), F32),
                   jax.ShapeDtypeStruct((bx, t, P_GATE), F32)],
        compiler_params=_cparams(2),
        name="in_proj",
    )(x, g, w_bf, rc, ra, rb)


_TK = 256
_CHUNK_SHIFT = CHUNK.bit_length() - 1


def _attn_kernel(*refs, tq, n_new, n_past, n_valid_new, causal, lam_init):
    if n_past:
        lam_ref, sub_ref, q_ref, k_ref, v_ref, kp_ref, vp_ref, o_ref, kb_s, vt_s = refs
    else:
        lam_ref, sub_ref, q_ref, k_ref, v_ref, o_ref, kb_s, vt_s, s_s, p_s, acc_s = refs
        kp_ref = vp_ref = None
    i = pl.program_id(2)

    @pl.when(i == 0)
    def _():
        for src_k, src_v, n, base in ((kp_ref, vp_ref, n_past, 0), (k_ref, v_ref, n_new, n_past)):
            for r0 in range(0, n, _TK):
                rn = min(_TK, n - r0)
                kb_s[base + r0:base + r0 + rn, :] = src_k[0, r0:r0 + rn, :].astype(BF16)
                vt_s[(base + r0) // _TK, :, 0:rn] = src_v[0, r0:r0 + rn, :].T.astype(BF16)

    lp = lam_ref[...]
    lam = (jnp.exp(jnp.sum(lp[0:1] * lp[1:2], axis=1, keepdims=True))
           - jnp.exp(jnp.sum(lp[2:3] * lp[3:4], axis=1, keepdims=True)) + lam_init)

    qt = q_ref[0].astype(F32).T.astype(BF16)
    z = jnp.zeros((DA_HEAD_DIM, tq), BF16)
    qbd = jnp.concatenate([jnp.concatenate([qt[:DA_HEAD_DIM], z], axis=1),
                           jnp.concatenate([z, qt[DA_HEAD_DIM:]], axis=1)], axis=0)

    def softmax_step(s, stats, mask):
        new_stats, alphas, ps = [], [], []
        for c in range(2):
            m, l = stats[2 * c:2 * c + 2]
            sc = s[:, c * tq:(c + 1) * tq]
            if mask is not None:
                sc = jnp.where(mask, sc, NEG_BIG)
            m_new = jnp.maximum(m, jnp.max(sc, axis=0, keepdims=True))
            alpha = jnp.exp2(m - m_new)
            p = jnp.exp2(sc - m_new)
            new_stats += [m_new, alpha * l + jnp.sum(p, axis=0, keepdims=True)]
            alphas.append(alpha)
            ps.append(p.astype(BF16))
        return tuple(new_stats), tuple(alphas), tuple(ps)

    def accumulate(accs, alphas, ps, vtblk):
        return tuple(alphas[c] * accs[c] + _dot(vtblk, ps[c]) for c in range(2))

    stats = (jnp.full((1, tq), NEG_BIG, F32), jnp.zeros((1, tq), F32)) * 2
    accs = (jnp.zeros((DA_V_DIM, tq), F32),) * 2

    if causal:
        def scores(kb):
            r0 = pl.multiple_of(kb * _TK, _TK)
            return _dot(kb_s[pl.ds(r0, _TK), :], qbd)

        def lagged_accumulate(j, prev, alphas):
            for c in range(2):
                acc_s[c] = alphas[c] * acc_s[c] + _dot(vt_s[jnp.maximum(j - 1, 0)], p_s[prev, c])

        def step(j, cur, carry):
            stats, alphas = carry
            s_s[1 - cur] = scores(j + 1)
            lagged_accumulate(j, 1 - cur, alphas)
            stats, alphas, ps = softmax_step(s_s[cur], stats, None)
            for c in range(2):
                p_s[cur, c] = ps[c]
            return stats, alphas

        s_s[0] = scores(0)
        p_s[1] = jnp.zeros((2, _TK, tq), BF16)
        acc_s[...] = jnp.zeros((2, DA_V_DIM, tq), F32)
        ones = (jnp.ones((1, tq), F32),) * 2
        carry = lax.fori_loop(0, i // 2, lambda jj, c: step(2 * jj + 1, 1, step(2 * jj, 0, c)), (stats, ones))
        odd = (i & 1) == 1
        stats, alphas = lax.cond(odd, lambda c: step(i - 1, 0, c), lambda c: c, carry)
        diag = i & 1
        lagged_accumulate(i, 1 - diag, alphas)
        kr = jnp.right_shift(lax.broadcasted_iota(jnp.int32, (_TK, tq), 0), _CHUNK_SHIFT)
        qc = jnp.right_shift(lax.broadcasted_iota(jnp.int32, (_TK, tq), 1), _CHUNK_SHIFT)
        stats, alphas, ps = softmax_step(s_s[diag], stats, kr <= qc)
        accs = accumulate((acc_s[0], acc_s[1]), alphas, ps, vt_s[i])
    else:
        blocks = [(kb_s[r0:r0 + _TK, :], vt_s[r0 // _TK], None) for r0 in range(0, n_past, _TK)]
        valid = lax.broadcasted_iota(jnp.int32, (n_new, tq), 0) < n_valid_new
        blocks.append((kb_s[n_past:n_past + n_new, :], vt_s[n_past // _TK, :, 0:n_new], valid))
        for kblk, vtblk, mask in blocks:
            stats, alphas, ps = softmax_step(_dot(kblk, qbd), stats, mask)
            accs = accumulate(accs, alphas, ps, vtblk)

    (m1, l1, m2, l2), (a1, a2) = stats, accs
    ot = a1 * (1.0 / l1) - lam * (a2 * (1.0 / l2))
    ss = jnp.sum(ot * ot, axis=0, keepdims=True) * (1.0 / DA_V_DIM)
    ot = ot * lax.rsqrt(ss + EPS) * sub_ref[...] * (1.0 - lam_init)
    o_ref[0] = ot.T.astype(BF16)


def _attn(da_lambda, subln_col, q, k, v, past_k, past_v, *, tq, causal, lam_init, n_valid_new):
    bx, t, _ = q.shape
    n_new = k.shape[1]
    n_past = 0 if past_k is None else past_k.shape[1]
    nq = t // tq
    head = lambda rows, at: pl.BlockSpec((1, rows, LANES), at)
    in_specs = [_resident((4, DA_HEAD_DIM)), _resident((DA_V_DIM, 1)),
                head(tq, lambda b, h, i: (b, i, h)),
                head(n_new, lambda b, h, i: (b, 0, h)),
                head(n_new, lambda b, h, i: (b, 0, h))]
    args = [da_lambda, subln_col, q, k, v]
    if n_past:
        in_specs += [head(n_past, lambda b, h, i: (b, 0, h))] * 2
        args += [past_k, past_v]
    scratch = [pltpu.VMEM((n_past + n_new, LANES), BF16),
               pltpu.VMEM((pl.cdiv(n_past + n_new, _TK), DA_V_DIM, _TK), BF16)]
    if causal:
        scratch += [pltpu.VMEM((2, _TK, 2 * tq), F32), pltpu.VMEM((2, 2, _TK, tq), BF16),
                    pltpu.VMEM((2, DA_V_DIM, tq), F32)]
    return pl.pallas_call(
        functools.partial(_attn_kernel, tq=tq, n_new=n_new, n_past=n_past,
                          n_valid_new=n_valid_new, causal=causal, lam_init=lam_init),
        grid=(bx, DA_HEADS, nq),
        in_specs=in_specs,
        out_specs=head(tq, lambda b, h, i: (b, i, h)),
        out_shape=jax.ShapeDtypeStruct((bx, t, DA_WIDTH), BF16),
        scratch_shapes=scratch,
        compiler_params=_cparams(3),
        name="diff_attn",
    )(*args)


def _split3(x):
    hi = x.astype(BF16)
    r1 = x - hi.astype(F32)
    mid = r1.astype(BF16)
    lo = (r1 - mid.astype(F32)).astype(BF16)
    return hi, mid, lo


def _rw_prep_kernel(u_ref, halo_ref, sh_ref, mu_ref, w0_ref, w2_ref, a0_ref, a2_ref, g2_ref,
                    kk_ref, ka_ref, seg_ref, tri_ref,
                    rt_ref, at_ref, bt_ref, kt_ref, v_ref, g_ref, wc_ref, *, tm, chunk):
    i = pl.program_id(1)
    u = u_ref[0]
    prev = jnp.where(i == 0, sh_ref[0], halo_ref[0, SUBLANES - 1:SUBLANES, :])
    rows = lax.broadcasted_iota(jnp.int32, (tm, 1), 0)
    up = jnp.where(rows == 0, prev, pltpu.roll(u, 1, 0))
    us = u + (up - u) * mu_ref[...]

    r = us[:, 0:RW_WIDTH]
    kr = us[:, RW_WIDTH:2 * RW_WIDTH]
    vr = us[:, 2 * RW_WIDTH:3 * RW_WIDTH]
    wa = us[:, 3 * RW_WIDTH:3 * RW_WIDTH + DECAY_LORA + AAA_LORA]
    gd = us[:, 3 * RW_WIDTH + DECAY_LORA + AAA_LORA:]

    z = w0_ref[...] + _dot(jnp.tanh(wa).astype(BF16), w2_ref[...])
    nz = -z
    w_log = -(jnp.maximum(nz, 0.0) + jnp.log1p(jnp.exp(-jnp.abs(nz)))) - 0.5
    lw = -jnp.exp(w_log)
    a = _sigmoid(a0_ref[...] + _dot(wa.astype(BF16), a2_ref[...]))
    g_ref[0] = _dot(_sigmoid(gd).astype(BF16), g2_ref[...])
    v_ref[0] = vr

    kk = kr * kk_ref[...]
    k2 = kk * kk
    k2h = k2.astype(BF16)
    k2l = (k2 - k2h.astype(F32)).astype(BF16)
    seg = seg_ref[...]
    ssum = jnp.concatenate(
        [_dot(k2h[:, c:c + LANES], seg) + _dot(k2l[:, c:c + LANES], seg) for c in range(0, RW_WIDTH, LANES)],
        axis=1)
    kk = kk * lax.rsqrt(jnp.maximum(ssum, 1e-24))
    kmod = kr * (1.0 + (a - 1.0) * ka_ref[...])

    tri = tri_ref[...]
    for c in range(tm // chunk):
        sl = slice(c * chunk, (c + 1) * chunk)
        lwc = lw[sl]
        hi, mid, lo = _split3(lwc)
        cum = _dot(tri, hi) + _dot(tri, mid) + _dot(tri, lo)
        e_pos = jnp.exp(cum)
        e_neg = jnp.exp(-cum)
        rt_ref[0, sl, :] = r[sl] * e_pos
        at_ref[0, sl, :] = -kk[sl] * jnp.exp(cum - lwc)
        bt_ref[0, sl, :] = kk[sl] * a[sl] * e_neg
        kt_ref[0, sl, :] = kmod[sl] * e_neg
        wc_ref[0, c] = e_pos[chunk - 1:chunk, :]


def _rw_prep(u, shift0, mu, w0, w2p, a0, a2p, g2, k_k, k_a, seg, tri, *, tm, chunk):
    bx, t, _ = u.shape
    nt = t // tm
    row = lambda w: pl.BlockSpec((1, tm, w), lambda b, i: (b, i, 0))
    halo = pl.BlockSpec((1, SUBLANES, P_RW),
                        lambda b, i: (b, jnp.maximum(i * (tm // SUBLANES) - 1, 0), 0))
    outs = [jax.ShapeDtypeStruct((bx, t, RW_WIDTH), F32)] * 6
    outs.append(jax.ShapeDtypeStruct((bx, t // chunk, 1, RW_WIDTH), F32))
    return pl.pallas_call(
        functools.partial(_rw_prep_kernel, tm=tm, chunk=chunk),
        grid=(bx, nt),
        in_specs=[row(P_RW), halo, pl.BlockSpec((1, 1, P_RW), lambda b, i: (b, 0, 0)),
                  _resident((1, P_RW)), _resident((1, RW_WIDTH)),
                  _resident((DECAY_LORA + AAA_LORA, RW_WIDTH)), _resident((1, RW_WIDTH)),
                  _resident((DECAY_LORA + AAA_LORA, RW_WIDTH)), _resident((GATE_LORA, RW_WIDTH)),
                  _resident((1, RW_WIDTH)), _resident((1, RW_WIDTH)),
                  _resident((LANES, LANES)), _resident((chunk, chunk))],
        out_specs=[row(RW_WIDTH)] * 6 + [pl.BlockSpec((1, tm // chunk, 1, RW_WIDTH), lambda b, i: (b, i, 0, 0))],
        out_shape=outs,
        compiler_params=_cparams(2),
        name="rw_prep",
    )(u, u, shift0, mu, w0, w2p, a0, a2p, g2, k_k, k_a, seg, tri)


def _wkv_kernel(rt_ref, at_ref, bt_ref, kt_ref, v_ref, g_ref, wc_ref, s0_ref, rk_ref, lnw_ref, lnb_ref,
                o_ref, sout_ref, s_scr, rp_s, m_s, y0_s, gc_s, bon_s, *, n_chunks):
    C = WKV_CHUNK
    R = 2 * C
    t = pl.program_id(2)

    @pl.when(t == 0)
    def _():
        s_scr[...] = s0_ref[0, 0]

    ri = lax.broadcasted_iota(jnp.int32, (R, R), 0)
    ci = lax.broadcasted_iota(jnp.int32, (R, R), 1)
    sh = lambda x, sz: jnp.right_shift(x, sz.bit_length() - 1)
    same_head = sh(ri, C) == sh(ci, C)
    strict = same_head & (ci < ri)
    incl = same_head & (ci <= ri)
    eye = ri == ci

    def bd(sz):
        return sh(ri, sz) == sh(ci, sz)

    lane_lo = lax.broadcasted_iota(jnp.int32, (C, LANES), 1) < RW_HEAD_DIM

    def stack(x):
        return jnp.concatenate([jnp.where(lane_lo, x, 0.0), jnp.where(lane_lo, 0.0, x)], axis=0)

    rk = rk_ref[...]
    lnw = lnw_ref[...]
    lnb = lnb_ref[...]
    b16 = lambda x: x.astype(BF16)
    ident = jnp.where(eye, 1.0, 0.0)

    chunks = range(n_chunks)
    rs, asb, bsb, ksb, vsb, bdk, kdk, wcs = [], [], [], [], [], [], [], []
    for c in chunks:
        sl = slice(c * C, (c + 1) * C)
        r_, a_, b_, k_, v_ = (stack(ref[0, sl, :]) for ref in (rt_ref, at_ref, bt_ref, kt_ref, v_ref))
        wc = wc_ref[0, c]
        bon_s[c] = jnp.sum(r_ * k_ * rk, axis=1, keepdims=True) * v_
        rs.append(r_); asb.append(b16(a_)); bsb.append(b16(b_)); ksb.append(b16(k_)); vsb.append(b16(v_))
        bdk.append(b16(b_ * wc)); kdk.append(b16(k_ * wc)); wcs.append(wc)

    n, n8, aak, arb, ark = [], [], [], [], []
    for c in chunks:
        ab = _dot_nt(jnp.concatenate([asb[c], b16(rs[c])], axis=0), jnp.concatenate([bsb[c], ksb[c]], axis=0))
        nc = jnp.where(strict, ab[:R, :R], 0.0)
        n.append(b16(nc)); n8.append(jnp.where(bd(8), nc, 0.0))
        aak.append(b16(jnp.where(strict, ab[:R, R:], 0.0)))
        arb.append(b16(jnp.where(incl, ab[R:, :R], 0.0)))
        ark.append(b16(jnp.where(incl, ab[R:, R:], 0.0)))

    n8b = [b16(x) for x in n8]
    n2 = [_dot(n8b[c], n8b[c]) for c in chunks]
    n2b = [b16(x) for x in n2]
    n4 = [_dot(n2b[c], n2b[c]) for c in chunks]
    p1 = [_dot(b16(ident + n8[c]), b16(ident + n2[c])) for c in chunks]
    tinv = [_dot(b16(p1[c]), b16(ident + n4[c])) for c in chunks]
    zero16 = jnp.zeros((R, R), BF16)
    for sz in (8, 16, 32):
        off = bd(2 * sz) & jnp.logical_not(bd(sz))
        tb = [b16(x) for x in tinv]
        x = [_dot(tb[c], jnp.where(off, n[c], zero16)) for c in chunks]
        tinv = [tinv[c] + _dot(b16(x[c]), tb[c]) for c in chunks]
    tb = [b16(x) for x in tinv]

    ap = [_dot(tb[c], asb[c]) for c in chunks]
    u0p = [_dot(aak[c], vsb[c]) for c in chunks]
    u0 = [_dot(tb[c], b16(u0p[c])) for c in chunks]
    for c in chunks:
        rp_s[c] = b16(rs[c] + _dot(arb[c], b16(ap[c])))
        m_s[c] = b16(jnp.where(eye, wcs[c], 0.0) + _dot(b16(ap[c].T), bdk[c]))
    for c in chunks:
        y0_s[c] = _dot(arb[c], b16(u0[c])) + _dot(ark[c], vsb[c])
        gc_s[c] = _dot(b16(u0[c].T), bdk[c]) + _dot(b16(vsb[c].astype(F32).T), kdk[c])

    s = s_scr[...]
    for c in range(n_chunks):
        sl = slice(c * C, (c + 1) * C)
        sb = b16(s)
        y = _dot_nt(rp_s[c], sb) + y0_s[c]
        s = _dot(sb, m_s[c]) + gc_s[c]
        mu = jnp.sum(y, axis=1, keepdims=True) * (1.0 / RW_HEAD_DIM)
        d = jnp.where(same_head, y - mu, 0.0)
        var = jnp.sum(d * d, axis=1, keepdims=True) * (1.0 / RW_HEAD_DIM)
        ost = d * lax.rsqrt(var + GN_EPS) * lnw + jnp.where(same_head, lnb, 0.0) + bon_s[c]
        o_ref[0, sl, :] = ((ost[:C] + ost[C:]) * g_ref[0, sl, :]).astype(BF16)
    s_scr[...] = s
    sout_ref[0, 0] = s


def _wkv(rt, at, bt, kt, v, g, wc, s0bd, rk, lnw, lnb, *, tb):
    bx, t, _ = rt.shape
    npair = RW_HEADS // 2
    n_chunks = tb // WKV_CHUNK
    seq = pl.BlockSpec((1, tb, LANES), lambda b, p, i: (b, i, p))
    vec = pl.BlockSpec((1, LANES), lambda b, p, i: (0, p))
    st = pl.BlockSpec((1, 1, LANES, LANES), lambda b, p, i: (b, p, 0, 0))
    return pl.pallas_call(
        functools.partial(_wkv_kernel, n_chunks=n_chunks),
        grid=(bx, npair, t // tb),
        in_specs=[seq] * 6 + [pl.BlockSpec((1, n_chunks, 1, LANES), lambda b, p, i: (b, i, 0, p)),
                              st, vec, vec, vec],
        out_specs=[seq, st],
        out_shape=[jax.ShapeDtypeStruct((bx, t, RW_WIDTH), BF16),
                   jax.ShapeDtypeStruct((bx, npair, LANES, LANES), F32)],
        scratch_shapes=[pltpu.VMEM((LANES, LANES), F32),
                        pltpu.VMEM((n_chunks, LANES, LANES), BF16), pltpu.VMEM((n_chunks, LANES, LANES), BF16),
                        pltpu.VMEM((n_chunks, LANES, LANES), F32), pltpu.VMEM((n_chunks, LANES, LANES), F32),
                        pltpu.VMEM((n_chunks, LANES, LANES), F32)],
        compiler_params=_cparams(3),
        name="wkv",
    )(rt, at, bt, kt, v, g, wc, s0bd, rk, lnw, lnb)


def _merge_kernel(x_ref, oda_ref, orw_ref, gt_ref, wda_ref, wrw_ref, wout_ref, o_ref):
    a = _dot(oda_ref[0], wda_ref[...])
    b = _dot(orw_ref[0], wrw_ref[...])
    gt = gt_ref[0]
    m = _sigmoid(gt[:, :D_MODEL]) * a + _sigmoid(gt[:, D_MODEL:]) * b
    o_ref[0] = x_ref[0] + _dot(m.astype(BF16), wout_ref[...])


def _merge(x, oda, orw, gt, wda, wrw, wout, *, tm):
    bx, t, _ = x.shape
    row = lambda w: pl.BlockSpec((1, tm, w), lambda b, i: (b, i, 0))
    sq = _resident((D_MODEL, D_MODEL))
    return pl.pallas_call(
        _merge_kernel,
        grid=(bx, t // tm),
        in_specs=[row(D_MODEL), row(DA_WIDTH), row(RW_WIDTH), row(P_GATE), sq, sq, sq],
        out_specs=row(D_MODEL),
        out_shape=jax.ShapeDtypeStruct((bx, t, D_MODEL), F32),
        compiler_params=_cparams(2),
        name="merge",
    )(x, oda, orw, gt, wda, wrw, wout)


_FF_CW = 256
_HALO = SUBLANES


def _ffn_kernel(x_ref, halo_ref, c0_ref, g_ref, wup_ref, f_ref, fb_ref, wdn_ref, gf_ref,
                o_ref, conv_ref, ext_s, act_s, *, tm, final):
    first = pl.program_id(1) == 0
    xe = jnp.concatenate([halo_ref[0], x_ref[0]], axis=0)
    hb = _rms(xe, g_ref[...]).astype(BF16)
    starts = list(range(0, D_FF, _FF_CW))

    def up_proj(c0):
        for base in (c0, D_FF + c0):
            cols = slice(base, base + _FF_CW)
            up = _dot(hb, wup_ref[:, cols])
            ext_s[:, cols] = up
            ext_s[_HALO - 2:_HALO, cols] = jnp.where(first, c0_ref[0, :, cols], up[_HALO - 2:_HALO])
            conv_ref[0, :, cols] = up[_HALO + tm - 2:_HALO + tm]

    def conv_act(c0):
        cs = []
        for base in (c0, D_FF + c0):
            cols = slice(base, base + _FF_CW)
            f = f_ref[:, cols]
            cs.append(fb_ref[:, cols]
                      + ext_s[_HALO - 2:_HALO - 2 + tm, cols] * f[0:1]
                      + ext_s[_HALO - 1:_HALO - 1 + tm, cols] * f[1:2]
                      + ext_s[_HALO:_HALO + tm, cols] * f[2:3])
        ca, cb = cs
        act_s[:, c0:c0 + _FF_CW] = (ca * _sigmoid(ca) * cb).astype(BF16)

    up_proj(starts[0])
    for j, c0 in enumerate(starts):
        if j + 1 < len(starts):
            up_proj(starts[j + 1])
        conv_act(c0)
    out = x_ref[0] + _dot(act_s[...], wdn_ref[...])
    if final:
        out = _rms(out, gf_ref[...])
    o_ref[0] = out


def _ffn(x, conv0, g, wup, f, fb, wdn, gfinal, *, tm, final):
    bx, t, _ = x.shape
    row = pl.BlockSpec((1, tm, D_MODEL), lambda b, i: (b, i, 0))
    halo = pl.BlockSpec((1, _HALO, D_MODEL),
                        lambda b, i: (b, jnp.maximum(i * (tm // _HALO) - 1, 0), 0))
    cst = pl.BlockSpec((1, CONV_W - 1, 2 * D_FF), lambda b, i: (b, 0, 0))
    return pl.pallas_call(
        functools.partial(_ffn_kernel, tm=tm, final=final),
        grid=(bx, t // tm),
        in_specs=[row, halo, cst, _resident((1, D_MODEL)), _resident((D_MODEL, 2 * D_FF)),
                  _resident((CONV_W, 2 * D_FF)), _resident((1, 2 * D_FF)), _resident((D_FF, D_MODEL)),
                  _resident((1, D_MODEL))],
        out_specs=[row, cst],
        out_shape=[jax.ShapeDtypeStruct((bx, t, D_MODEL), F32),
                   jax.ShapeDtypeStruct((bx, CONV_W - 1, 2 * D_FF), F32)],
        scratch_shapes=[pltpu.VMEM((_HALO + tm, 2 * D_FF), F32), pltpu.VMEM((tm, D_FF), BF16)],
        compiler_params=_cparams(2),
        name="ffn",
    )(x, x, conv0, g, wup, f, fb, wdn, gfinal)


def _rope_tables(pos):
    half = ROT_DIM // 2
    inv = ROPE_THETA ** (-jnp.arange(0, ROT_DIM, 2, dtype=F32) / ROT_DIM)
    ang = pos.astype(F32)[:, None] * inv[None, :]
    cos, sin = jnp.cos(ang), jnp.sin(ang)
    t = pos.shape[0]
    pad = jnp.zeros((t, DA_HEAD_DIM - ROT_DIM), F32)
    z = jnp.zeros((t, half), F32)
    one_map = lambda a, b, fill: jnp.concatenate([a, b, pad + fill], axis=1)
    rc = one_map(cos, cos, 1.0)
    ra = one_map(z, sin, 0.0)
    rb = one_map(-sin, z, 0.0)
    dup = lambda m: jnp.concatenate([m, m], axis=1)
    return dup(rc), dup(ra), dup(rb)


def _state_to_blockdiag(s):
    b = s.shape[0]
    s = s.reshape(b, RW_HEADS // 2, 2, RW_HEAD_DIM, RW_HEAD_DIM)
    z = jnp.zeros_like(s[:, :, 0])
    top = jnp.concatenate([s[:, :, 0], z], axis=-1)
    bot = jnp.concatenate([z, s[:, :, 1]], axis=-1)
    return jnp.concatenate([top, bot], axis=-2)


def _blockdiag_to_state(sbd):
    b = sbd.shape[0]
    h0 = sbd[:, :, :RW_HEAD_DIM, :RW_HEAD_DIM]
    h1 = sbd[:, :, RW_HEAD_DIM:, RW_HEAD_DIM:]
    return jnp.stack([h0, h1], axis=2).reshape(b, RW_HEADS, RW_HEAD_DIM, RW_HEAD_DIM)


def _layer(x, lidx, pos, past_k, past_v, wkv0, shift0, conv0, w, norm_final, *, final):
    bx, t, _ = x.shape
    prompt = past_k is None
    tm = 256 if prompt else t
    lam_init = 0.8 - 0.6 * math.exp(-0.3 * lidx)

    rc, ra, rb = _rope_tables(pos)
    q, k, v, u_rw, gate = _in_proj(x, w["norm_mix"], w["w_in"], rc, ra, rb, tm=tm)

    if prompt:
        o_da = _attn(w["da_lambda"], w["da_subln"], q, k, v, None, None,
                     tq=_TK, causal=True, lam_init=lam_init, n_valid_new=t)
    else:
        padq = LANES - t
        qp = jnp.pad(q, ((0, 0), (0, padq), (0, 0)))
        kp = jnp.pad(k, ((0, 0), (0, padq), (0, 0)))
        vp = jnp.pad(v, ((0, 0), (0, padq), (0, 0)))
        o_da = _attn(w["da_lambda"], w["da_subln"], qp, kp, vp,
                     past_k.reshape(bx, -1, DA_WIDTH), past_v.reshape(bx, -1, DA_WIDTH),
                     tq=LANES, causal=False, lam_init=lam_init, n_valid_new=t)[:, :t]

    chunk = WKV_CHUNK if prompt else t
    rt, at, bt, kt, vr, g, wc = _rw_prep(
        u_rw, shift0, w["rw_mu"], w["rw_w0"], w["rw_w2p"], w["rw_a0"], w["rw_a2p"], w["rw_g2"],
        w["rw_k_k"], w["rw_k_a"], w["seg"], w["tri_p"] if prompt else w["tri_s"], tm=tm, chunk=chunk)
    if not prompt:
        padt = ((0, 0), (0, WKV_CHUNK - t), (0, 0))
        rt, at, bt, kt, vr, g = [jnp.pad(a, padt) for a in (rt, at, bt, kt, vr, g)]
    o_rw, sbd = _wkv(rt, at, bt, kt, vr, g, wc, _state_to_blockdiag(wkv0),
                     w["rw_r_k"], w["rw_ln_w"], w["rw_ln_b"], tb=512 if prompt else WKV_CHUNK)
    o_rw = o_rw[:, :t]

    x = _merge(x, o_da, o_rw, gate, w["w_o_da"], w["w_o_rw"], w["w_out"], tm=tm)
    x, new_conv = _ffn(x, conv0, w["norm_ffn"], w["w_up"], w["ffn_conv"], w["ffn_conv_b"], w["w_down"],
                       norm_final, tm=tm, final=final)
    new_k = k.reshape(bx, t, DA_HEADS, DA_V_DIM)
    new_v = v.reshape(bx, t, DA_HEADS, DA_V_DIM)
    return x, new_k, new_v, _blockdiag_to_state(sbd), u_rw[:, -1:], new_conv


def kernel(x_prompt, x_sample, cache_k, cache_v, state_wkv, state_shift, state_ffn_conv, norm_mix, w_in, da_lambda, da_subln, w_o_da, rw_mu, rw_w0, rw_w2, rw_a0, rw_a2, rw_g2, rw_k_k, rw_k_a, rw_r_k, rw_ln_w, rw_ln_b, w_o_rw, w_out, norm_ffn, w_up, ffn_conv, ffn_conv_b, w_down, norm_final):
    bp, tp, _ = x_prompt.shape
    bs, ts, _ = x_sample.shape
    depth = w_in.shape[0]
    past = cache_k.shape[2]
    pos_p = jnp.arange(tp, dtype=jnp.int32)
    pos_s = past + jnp.arange(ts, dtype=jnp.int32)

    lane = jnp.arange(LANES)
    seg = (lane[:, None] // RW_HEAD_DIM == lane[None, :] // RW_HEAD_DIM).astype(BF16)
    tri = lambda n: (jnp.arange(n)[:, None] >= jnp.arange(n)[None, :]).astype(BF16)
    zl = jnp.zeros((DECAY_LORA, RW_WIDTH), F32)
    row = lambda a: a.reshape(1, -1)

    xp, xs = x_prompt, x_sample
    outs_p, outs_s = [], []
    nf = row(norm_final)
    for l in range(depth):
        w = dict(
            norm_mix=row(norm_mix[l]), w_in=w_in[l].astype(BF16), da_lambda=da_lambda[l],
            da_subln=da_subln[l].reshape(-1, 1), w_o_da=w_o_da[l].astype(BF16),
            rw_mu=row(rw_mu[l]), rw_w0=row(rw_w0[l]),
            rw_w2p=jnp.concatenate([rw_w2[l], zl], axis=0).astype(BF16),
            rw_a0=row(rw_a0[l]), rw_a2p=jnp.concatenate([zl, rw_a2[l]], axis=0).astype(BF16),
            rw_g2=rw_g2[l].astype(BF16), rw_k_k=row(rw_k_k[l]), rw_k_a=row(rw_k_a[l]),
            rw_r_k=row(rw_r_k[l]), rw_ln_w=row(rw_ln_w[l]), rw_ln_b=row(rw_ln_b[l]),
            w_o_rw=w_o_rw[l].astype(BF16), w_out=w_out[l].astype(BF16), norm_ffn=row(norm_ffn[l]),
            w_up=w_up[l].astype(BF16), ffn_conv=ffn_conv[l], ffn_conv_b=row(ffn_conv_b[l]),
            w_down=w_down[l].astype(BF16), seg=seg, tri_p=tri(WKV_CHUNK), tri_s=tri(ts))
        final = l == depth - 1
        zero = lambda *s: jnp.zeros(s, x_prompt.dtype)
        xp, kp, vp, sp, shp, cp = _layer(
            xp, l, pos_p, None, None, zero(bp, RW_HEADS, RW_HEAD_DIM, RW_HEAD_DIM), zero(bp, 1, P_RW),
            zero(bp, CONV_W - 1, 2 * D_FF), w, nf, final=final)
        xs, kq, vq, sq, shq, cq = _layer(
            xs, l, pos_s, cache_k[l], cache_v[l], state_wkv[l], state_shift[l], state_ffn_conv[l],
            w, nf, final=final)
        outs_p.append((kp, vp, sp, shp, cp))
        outs_s.append((kq, vq, sq, shq, cq))

    k_prompt, v_prompt, wkv_prompt, shift_prompt, conv_prompt = [jnp.stack(t) for t in zip(*outs_p)]
    k_sample, v_sample, wkv_sample, shift_sample, conv_sample = [jnp.stack(t) for t in zip(*outs_s)]
    return (xp, xs, k_prompt, v_prompt, wkv_prompt, shift_prompt, conv_prompt,
            k_sample, v_sample, wkv_sample, shift_sample, conv_sample)
```

```python
import functools
import math

import jax
import jax.numpy as jnp
from jax import lax
from jax.experimental import pallas as pl
from jax.experimental.pallas import tpu as pltpu

F32 = jnp.float32
BF16 = jnp.bfloat16

D_MODEL = 1024
CHUNK = 64
EPS = 1e-6
DA_HEADS = 8
DA_HEAD_DIM = 64
DA_V_DIM = 128
DA_WIDTH = DA_HEADS * DA_V_DIM
ROPE_THETA = 500000.0
ROT_DIM = DA_HEAD_DIM // 4
RW_HEAD_DIM = 64
RW_HEADS = 16
RW_WIDTH = 1024
DECAY_LORA = 64
AAA_LORA = 64
GATE_LORA = 128
GN_EPS = 64e-5
D_FF = 2816
CONV_W = 3
P_DA = 3 * DA_WIDTH
P_RW = 3 * RW_WIDTH + DECAY_LORA + AAA_LORA + GATE_LORA
P_GATE = 2 * D_MODEL
P_TOTAL = P_DA + P_RW + P_GATE

LANES = 128
SUBLANES = 8
VMEM_LIMIT = 56 * 1024 * 1024

WKV_CHUNK = 64
N_PAIRS = RW_HEADS // 2
NEG_BIG = -1e30
ROW_TILE = 256
WKV_ROWS = 512
WKV_PAIRS = 2

_ARB = "arbitrary"


def _cparams(n_axes):
    return pltpu.CompilerParams(dimension_semantics=(_ARB,) * n_axes,
                                vmem_limit_bytes=VMEM_LIMIT)


def _resident(shape):
    nd = len(shape)
    return pl.BlockSpec(shape, lambda *_: (0,) * nd, pipeline_mode=pl.Buffered(1))


def _dot(a, b):
    return jnp.dot(a, b, preferred_element_type=F32)


def _dot_nt(a, b):
    return lax.dot_general(a, b, (((1,), (1,)), ((), ())), preferred_element_type=F32)


def _rms(x, g):
    return x * lax.rsqrt(jnp.mean(x * x, axis=-1, keepdims=True) + EPS) * g


def _sigmoid(x):
    return 1.0 / (1.0 + jnp.exp(-x))


_IN_CW = 256
_LOG2E = math.log2(math.e)


def _in_proj_kernel(*refs, aliased):
    x_ref, g_ref, w_ref, rc_ref, ra_ref, rb_ref = refs[:6]
    q_ref, k_ref, v_ref, u_ref, gt_ref = refs[6 + (2 if aliased else 0):]
    hb = _rms(x_ref[0], g_ref[...]).astype(BF16)
    rc = rc_ref[...]
    ra = ra_ref[...]
    rb = rb_ref[...]

    def rope(t):
        return t * rc + pltpu.roll(t, ROT_DIM // 2, 1) * ra + pltpu.roll(t, LANES - ROT_DIM // 2, 1) * rb

    for c0 in range(0, P_TOTAL, _IN_CW):
        acc = _dot(hb, w_ref[:, c0:c0 + _IN_CW])
        if c0 < DA_WIDTH:
            for s in range(0, _IN_CW, LANES):
                q_ref[0, :, c0 + s:c0 + s + LANES] = (
                    rope(acc[:, s:s + LANES]) * (DA_HEAD_DIM ** -0.5 * _LOG2E)).astype(BF16)
        elif c0 < 2 * DA_WIDTH:
            o = c0 - DA_WIDTH
            for s in range(0, _IN_CW, LANES):
                k_ref[0, 0, :, o + s:o + s + LANES] = rope(acc[:, s:s + LANES])
        elif c0 < P_DA:
            o = c0 - 2 * DA_WIDTH
            v_ref[0, 0, :, o:o + _IN_CW] = acc
        elif c0 < P_DA + P_RW:
            o = c0 - P_DA
            u_ref[0, :, o:o + _IN_CW] = acc
        else:
            o = c0 - P_DA - P_RW
            gt_ref[0, :, o:o + _IN_CW] = acc.astype(BF16)


def _in_proj(x, g, w_bf, rc, ra, rb, kbuf, vbuf, *, layer, depth, tm):
    bx, t, _ = x.shape
    row = lambda w: pl.BlockSpec((1, tm, w), lambda b, i: (b, i, 0))
    tab = pl.BlockSpec((tm, LANES), lambda b, i: (i, 0))
    slot = pl.BlockSpec((1, 1, tm, DA_WIDTH), lambda b, i: (layer, b, i, 0))
    aliased = kbuf is not None
    in_specs = [row(D_MODEL), _resident((1, D_MODEL)), _resident((D_MODEL, P_TOTAL)), tab, tab, tab]
    args = [x, g, w_bf, rc, ra, rb]
    if aliased:
        in_specs += [pl.BlockSpec(memory_space=pl.ANY)] * 2
        args += [kbuf, vbuf]
    stacked = jax.ShapeDtypeStruct((depth, bx, t, DA_WIDTH), F32)
    return pl.pallas_call(
        functools.partial(_in_proj_kernel, aliased=aliased),
        grid=(bx, t // tm),
        in_specs=in_specs,
        out_specs=[row(DA_WIDTH), slot, slot, row(P_RW), row(P_GATE)],
        out_shape=[jax.ShapeDtypeStruct((bx, t, DA_WIDTH), BF16), stacked, stacked,
                   jax.ShapeDtypeStruct((bx, t, P_RW), F32),
                   jax.ShapeDtypeStruct((bx, t, P_GATE), BF16)],
        input_output_aliases={6: 1, 7: 2} if aliased else {},
        compiler_params=_cparams(2),
        name="in_proj",
    )(*args)


_TK = 256
_CHUNK_SHIFT = CHUNK.bit_length() - 1


def _lambda(lam_ref, lam_init):
    lp = lam_ref[...]
    return (jnp.exp(jnp.sum(lp[0:1] * lp[1:2], axis=1, keepdims=True))
            - jnp.exp(jnp.sum(lp[2:3] * lp[3:4], axis=1, keepdims=True)) + lam_init)


def _q_blockdiag(q_tile):
    tq = q_tile.shape[0]
    qt = q_tile.astype(F32).T.astype(BF16)
    z = jnp.zeros((DA_HEAD_DIM, tq), BF16)
    return jnp.concatenate([jnp.concatenate([qt[:DA_HEAD_DIM], z], axis=1),
                            jnp.concatenate([z, qt[DA_HEAD_DIM:]], axis=1)], axis=0)


def _softmax_step(s, stats, mask, tq):
    new_stats, alphas, ps = [], [], []
    for c in range(2):
        m, l = stats[2 * c:2 * c + 2]
        sc = s[:, c * tq:(c + 1) * tq]
        if mask is not None:
            sc = jnp.where(mask, sc, NEG_BIG)
        m_new = jnp.maximum(m, jnp.max(sc, axis=0, keepdims=True))
        alpha = jnp.exp2(m - m_new)
        p = jnp.exp2(sc - m_new)
        new_stats += [m_new, alpha * l + jnp.sum(p, axis=0, keepdims=True)]
        alphas.append(alpha)
        ps.append(p.astype(BF16))
    return tuple(new_stats), tuple(alphas), tuple(ps)


def _attn_finish(a1, a2, l1, l2, lam, sub_col, lam_init):
    ot = a1 * (1.0 / l1) - lam * (a2 * (1.0 / l2))
    ss = jnp.sum(ot * ot, axis=0, keepdims=True) * (1.0 / DA_V_DIM)
    ot = ot * lax.rsqrt(ss + EPS) * sub_col * (1.0 - lam_init)
    return ot.T.astype(BF16)


def _attn_prompt_kernel(lam_ref, sub_ref, q_ref, k_ref, v_ref, o_ref, kb_s, vt_s, s_s, p_s, acc_s,
                        *, n_tiles, lam_init):
    tq = _TK
    for jb in range(n_tiles):
        rows = slice(jb * _TK, (jb + 1) * _TK)
        kb_s[rows, :] = k_ref[0, 0, rows, :].astype(BF16)
        vt_s[jb] = v_ref[0, 0, rows, :].T.astype(BF16)
    lam = _lambda(lam_ref, lam_init)
    sub_col = sub_ref[...]
    kr = jnp.right_shift(lax.broadcasted_iota(jnp.int32, (_TK, tq), 0), _CHUNK_SHIFT)
    qc = jnp.right_shift(lax.broadcasted_iota(jnp.int32, (_TK, tq), 1), _CHUNK_SHIFT)
    diag_mask = kr <= qc
    init_stats = (jnp.full((1, tq), NEG_BIG, F32), jnp.zeros((1, tq), F32)) * 2

    steps = [(i, j) for i in range(n_tiles) for j in range(i + 1)]
    qbd = {}
    alphas_of = {}
    final_stats = {}

    def scores(n):
        i, j = steps[n]
        if j == 0:
            qbd[i] = _q_blockdiag(q_ref[0, i * tq:(i + 1) * tq, :])
        s_s[n % 2] = _dot(kb_s[j * _TK:(j + 1) * _TK, :], qbd[i])

    def values(n):
        i, j = steps[n]
        for c in range(2):
            pv = _dot(vt_s[j], p_s[n % 2, c])
            acc_s[c] = pv if j == 0 else alphas_of[n][c] * acc_s[c] + pv
        if j == i:
            _, l1, _, l2 = final_stats[i]
            o_ref[0, i * tq:(i + 1) * tq, :] = _attn_finish(acc_s[0], acc_s[1], l1, l2, lam, sub_col, lam_init)

    scores(0)
    stats = init_stats
    for n, (i, j) in enumerate(steps):
        if n + 1 < len(steps):
            scores(n + 1)
        if n >= 1:
            values(n - 1)
        if j == 0:
            stats = init_stats
        stats, alphas_of[n], ps = _softmax_step(s_s[n % 2], stats, diag_mask if j == i else None, tq)
        for c in range(2):
            p_s[n % 2, c] = ps[c]
        if j == i:
            final_stats[i] = stats
    values(len(steps) - 1)


def _attn_prompt(da_lambda, subln_col, q, kbuf, vbuf, *, layer, lam_init):
    bx, t, _ = q.shape
    n_tiles = t // _TK
    seq = pl.BlockSpec((1, t, LANES), lambda b, h: (b, 0, h))
    kv = pl.BlockSpec((1, 1, t, LANES), lambda b, h: (layer, b, 0, h))
    return pl.pallas_call(
        functools.partial(_attn_prompt_kernel, n_tiles=n_tiles, lam_init=lam_init),
        grid=(bx, DA_HEADS),
        in_specs=[_resident((4, DA_HEAD_DIM)), _resident((DA_V_DIM, 1)), seq, kv, kv],
        out_specs=seq,
        out_shape=jax.ShapeDtypeStruct((bx, t, DA_WIDTH), BF16),
        scratch_shapes=[pltpu.VMEM((t, LANES), BF16), pltpu.VMEM((n_tiles, DA_V_DIM, _TK), BF16),
                        pltpu.VMEM((2, _TK, 2 * _TK), F32), pltpu.VMEM((2, 2, _TK, _TK), BF16),
                        pltpu.VMEM((2, DA_V_DIM, _TK), F32)],
        compiler_params=_cparams(2),
        name="diff_attn",
    )(da_lambda, subln_col, q, kbuf, vbuf)


def _attn_sample_kernel(lam_ref, sub_ref, q_ref, k_ref, v_ref, kp_ref, vp_ref, o_ref, kb_s, vt_s,
                        *, tq, n_new, n_past, n_valid_new, lam_init):
    for src_k, src_v, n, base in ((kp_ref, vp_ref, n_past, 0), (k_ref, v_ref, n_new, n_past)):
        for r0 in range(0, n, _TK):
            rn = min(_TK, n - r0)
            kb_s[base + r0:base + r0 + rn, :] = src_k[0, 0, r0:r0 + rn, :].astype(BF16)
            vt_s[(base + r0) // _TK, :, 0:rn] = src_v[0, 0, r0:r0 + rn, :].T.astype(BF16)
    lam = _lambda(lam_ref, lam_init)
    qbd = _q_blockdiag(q_ref[0])
    blocks = [(kb_s[r0:r0 + _TK, :], vt_s[r0 // _TK], None) for r0 in range(0, n_past, _TK)]
    valid = lax.broadcasted_iota(jnp.int32, (n_new, tq), 0) < n_valid_new
    blocks.append((kb_s[n_past:n_past + n_new, :], vt_s[n_past // _TK, :, 0:n_new], valid))
    scores = [_dot(kblk, qbd) for kblk, _, _ in blocks]
    stats = (jnp.full((1, tq), NEG_BIG, F32), jnp.zeros((1, tq), F32)) * 2
    accs = (jnp.zeros((DA_V_DIM, tq), F32),) * 2
    for s, (_, vtblk, mask) in zip(scores, blocks):
        stats, alphas, ps = _softmax_step(s, stats, mask, tq)
        accs = tuple(alphas[c] * accs[c] + _dot(vtblk, ps[c]) for c in range(2))
    _, l1, _, l2 = stats
    o_ref[0] = _attn_finish(accs[0], accs[1], l1, l2, lam, sub_ref[...], lam_init)


def _attn_sample(da_lambda, subln_col, q, k, v, past_k, past_v, *, layer, lam_init, n_valid_new):
    bx, tq, _ = q.shape
    n_new = k.shape[2]
    n_past = past_k.shape[2]
    head = lambda rows, l: pl.BlockSpec((1, 1, rows, LANES), lambda b, h: (l, b, 0, h))
    return pl.pallas_call(
        functools.partial(_attn_sample_kernel, tq=tq, n_new=n_new, n_past=n_past,
                          n_valid_new=n_valid_new, lam_init=lam_init),
        grid=(bx, DA_HEADS),
        in_specs=[_resident((4, DA_HEAD_DIM)), _resident((DA_V_DIM, 1)),
                  pl.BlockSpec((1, tq, LANES), lambda b, h: (b, 0, h)),
                  head(n_new, 0), head(n_new, 0), head(n_past, layer), head(n_past, layer)],
        out_specs=pl.BlockSpec((1, tq, LANES), lambda b, h: (b, 0, h)),
        out_shape=jax.ShapeDtypeStruct((bx, tq, DA_WIDTH), BF16),
        scratch_shapes=[pltpu.VMEM((n_past + n_new, LANES), BF16),
                        pltpu.VMEM((pl.cdiv(n_past + n_new, _TK), DA_V_DIM, _TK), BF16)],
        compiler_params=_cparams(2),
        name="diff_attn_step",
    )(da_lambda, subln_col, q, k, v, past_k, past_v)


def _split3(x):
    hi = x.astype(BF16)
    r1 = x - hi.astype(F32)
    mid = r1.astype(BF16)
    lo = (r1 - mid.astype(F32)).astype(BF16)
    return hi, mid, lo


def _rw_prep_kernel(u_ref, halo_ref, sh_ref, mu_ref, w0_ref, w2_ref, a0_ref, a2_ref, g2_ref,
                    kk_ref, ka_ref, seg_ref, tri_ref,
                    rt_ref, at_ref, bt_ref, kt_ref, v_ref, g_ref, wc_ref, *, tm, chunk):
    i = pl.program_id(1)
    u = u_ref[0]
    prev = jnp.where(i == 0, sh_ref[0], halo_ref[0, SUBLANES - 1:SUBLANES, :])
    rows = lax.broadcasted_iota(jnp.int32, (tm, 1), 0)
    up = jnp.where(rows == 0, prev, pltpu.roll(u, 1, 0))
    us = u + (up - u) * mu_ref[...]

    r = us[:, 0:RW_WIDTH]
    kr = us[:, RW_WIDTH:2 * RW_WIDTH]
    vr = us[:, 2 * RW_WIDTH:3 * RW_WIDTH]
    wa = us[:, 3 * RW_WIDTH:3 * RW_WIDTH + DECAY_LORA + AAA_LORA]
    gd = us[:, 3 * RW_WIDTH + DECAY_LORA + AAA_LORA:]

    z = w0_ref[...] + _dot(jnp.tanh(wa).astype(BF16), w2_ref[...])
    nz = -z
    w_log = -(jnp.maximum(nz, 0.0) + jnp.log1p(jnp.exp(-jnp.abs(nz)))) - 0.5
    lw = -jnp.exp(w_log)
    a = _sigmoid(a0_ref[...] + _dot(wa.astype(BF16), a2_ref[...]))
    g_ref[0] = _dot(_sigmoid(gd).astype(BF16), g2_ref[...]).astype(BF16)
    v_ref[0] = vr.astype(BF16)

    kk = kr * kk_ref[...]
    k2 = kk * kk
    k2h = k2.astype(BF16)
    k2l = (k2 - k2h.astype(F32)).astype(BF16)
    seg = seg_ref[...]
    ssum = jnp.concatenate(
        [_dot(k2h[:, c:c + LANES], seg) + _dot(k2l[:, c:c + LANES], seg) for c in range(0, RW_WIDTH, LANES)],
        axis=1)
    kk = kk * lax.rsqrt(jnp.maximum(ssum, 1e-24))
    kmod = kr * (1.0 + (a - 1.0) * ka_ref[...])

    tri = tri_ref[...]
    for c in range(tm // chunk):
        sl = slice(c * chunk, (c + 1) * chunk)
        lwc = lw[sl]
        hi, mid, lo = _split3(lwc)
        cum = _dot(tri, hi) + _dot(tri, mid) + _dot(tri, lo)
        e_pos = jnp.exp(cum)
        e_neg = jnp.exp(-cum)
        rt_ref[0, sl, :] = (r[sl] * e_pos).astype(BF16)
        at_ref[0, sl, :] = (-kk[sl] * jnp.exp(cum - lwc)).astype(BF16)
        bt_ref[0, sl, :] = (kk[sl] * a[sl] * e_neg).astype(BF16)
        kt_ref[0, sl, :] = (kmod[sl] * e_neg).astype(BF16)
        wc_ref[0, c] = e_pos[chunk - 1:chunk, :]


def _rw_prep(u, shift0, mu, w0, w2p, a0, a2p, g2, k_k, k_a, seg, tri, *, tm, chunk):
    bx, t, _ = u.shape
    nt = t // tm
    row = lambda w: pl.BlockSpec((1, tm, w), lambda b, i: (b, i, 0))
    halo = pl.BlockSpec((1, SUBLANES, P_RW),
                        lambda b, i: (b, jnp.maximum(i * (tm // SUBLANES) - 1, 0), 0))
    outs = [jax.ShapeDtypeStruct((bx, t, RW_WIDTH), BF16)] * 6
    outs.append(jax.ShapeDtypeStruct((bx, t // chunk, 1, RW_WIDTH), F32))
    return pl.pallas_call(
        functools.partial(_rw_prep_kernel, tm=tm, chunk=chunk),
        grid=(bx, nt),
        in_specs=[row(P_RW), halo, pl.BlockSpec((1, 1, P_RW), lambda b, i: (b, 0, 0)),
                  _resident((1, P_RW)), _resident((1, RW_WIDTH)),
                  _resident((DECAY_LORA + AAA_LORA, RW_WIDTH)), _resident((1, RW_WIDTH)),
                  _resident((DECAY_LORA + AAA_LORA, RW_WIDTH)), _resident((GATE_LORA, RW_WIDTH)),
                  _resident((1, RW_WIDTH)), _resident((1, RW_WIDTH)),
                  _resident((LANES, LANES)), _resident((chunk, chunk))],
        out_specs=[row(RW_WIDTH)] * 6 + [pl.BlockSpec((1, tm // chunk, 1, RW_WIDTH), lambda b, i: (b, i, 0, 0))],
        out_shape=outs,
        compiler_params=_cparams(2),
        name="rw_prep",
    )(u, u, shift0, mu, w0, w2p, a0, a2p, g2, k_k, k_a, seg, tri)


def _wkv_kernel(rt_ref, at_ref, bt_ref, kt_ref, v_ref, g_ref, wc_ref, s0_ref, rk_ref, lnw_ref, lnb_ref,
                o_ref, sout_ref, s_scr, rp_s, m_s, y0_s, gc_s, bon_s, *, n_pairs, n_chunks):
    C = WKV_CHUNK
    R = 2 * C
    t = pl.program_id(2)

    @pl.when(t == 0)
    def _():
        s_scr[...] = s0_ref[0]

    ri = lax.broadcasted_iota(jnp.int32, (R, R), 0)
    ci = lax.broadcasted_iota(jnp.int32, (R, R), 1)
    sh = lambda x, sz: jnp.right_shift(x, sz.bit_length() - 1)
    same_head = sh(ri, C) == sh(ci, C)
    strict = same_head & (ci < ri)
    incl = same_head & (ci <= ri)
    eye = ri == ci

    def bd(sz):
        return sh(ri, sz) == sh(ci, sz)

    lane_lo = lax.broadcasted_iota(jnp.int32, (C, LANES), 1) < RW_HEAD_DIM

    def stack(x):
        z = jnp.zeros_like(x)
        return jnp.concatenate([jnp.where(lane_lo, x, z), jnp.where(lane_lo, z, x)], axis=0)

    b16 = lambda x: x.astype(BF16)
    f32 = lambda x: x.astype(F32)
    ident = jnp.where(eye, 1.0, 0.0)

    units = [(p, c) for p in range(n_pairs) for c in range(n_chunks)]
    idx = range(len(units))
    rs, asb, bsb, ksb, vsb, bdk, kdk, wcs = [], [], [], [], [], [], [], []
    for u, (p, c) in enumerate(units):
        rows, lanes = slice(c * C, (c + 1) * C), slice(p * LANES, (p + 1) * LANES)
        r_, a_, b_, k_, v_ = (stack(ref[0, rows, lanes]) for ref in (rt_ref, at_ref, bt_ref, kt_ref, v_ref))
        wc = wc_ref[0, c, :, lanes]
        rk = rk_ref[:, lanes]
        bon_s[u] = jnp.sum(f32(r_) * f32(k_) * rk, axis=1, keepdims=True) * f32(v_)
        rs.append(r_); asb.append(a_); bsb.append(b_); ksb.append(k_); vsb.append(v_)
        bdk.append(b16(f32(b_) * wc)); kdk.append(b16(f32(k_) * wc)); wcs.append(wc)

    n, n8, aak, arb, ark = [], [], [], [], []
    for u in idx:
        ab = _dot_nt(jnp.concatenate([asb[u], rs[u]], axis=0), jnp.concatenate([bsb[u], ksb[u]], axis=0))
        nc = jnp.where(strict, ab[:R, :R], 0.0)
        n.append(b16(nc)); n8.append(jnp.where(bd(8), nc, 0.0))
        aak.append(b16(jnp.where(strict, ab[:R, R:], 0.0)))
        arb.append(b16(jnp.where(incl, ab[R:, :R], 0.0)))
        ark.append(b16(jnp.where(incl, ab[R:, R:], 0.0)))

    n8b = [b16(x) for x in n8]
    n2 = [_dot(n8b[u], n8b[u]) for u in idx]
    n2b = [b16(x) for x in n2]
    n4 = [_dot(n2b[u], n2b[u]) for u in idx]
    p1 = [_dot(b16(ident + n8[u]), b16(ident + n2[u])) for u in idx]
    tinv = [_dot(b16(p1[u]), b16(ident + n4[u])) for u in idx]
    zero16 = jnp.zeros((R, R), BF16)
    for sz in (8, 16, 32):
        off = bd(2 * sz) & jnp.logical_not(bd(sz))
        tb = [b16(x) for x in tinv]
        x = [_dot(tb[u], jnp.where(off, n[u], zero16)) for u in idx]
        tinv = [tinv[u] + _dot(b16(x[u]), tb[u]) for u in idx]
    tb = [b16(x) for x in tinv]

    u0p = [_dot(aak[u], vsb[u]) for u in idx]
    tx = [_dot(tb[u], jnp.concatenate([asb[u], b16(u0p[u])], axis=1)) for u in idx]
    txb = [b16(x) for x in tx]
    az = [_dot(arb[u], txb[u]) for u in idx]
    for u in idx:
        rp_s[u] = b16(f32(rs[u]) + az[u][:, :LANES])
        y0_s[u] = az[u][:, LANES:] + _dot(ark[u], vsb[u])
    for u in idx:
        txt = b16(tx[u].T)
        mg = _dot(txt, bdk[u])
        m_s[u] = b16(jnp.where(eye, wcs[u], 0.0) + mg[:LANES])
        gc_s[u] = mg[LANES:] + _dot(b16(f32(vsb[u]).T), kdk[u])

    s = [s_scr[p] for p in range(n_pairs)]
    for c in range(n_chunks):
        for p in range(n_pairs):
            u = p * n_chunks + c
            rows, lanes = slice(c * C, (c + 1) * C), slice(p * LANES, (p + 1) * LANES)
            sb = b16(s[p])
            y = _dot_nt(rp_s[u], sb) + y0_s[u]
            s[p] = _dot(sb, m_s[u]) + gc_s[u]
            mu = jnp.sum(y, axis=1, keepdims=True) * (1.0 / RW_HEAD_DIM)
            d = jnp.where(same_head, y - mu, 0.0)
            var = jnp.sum(d * d, axis=1, keepdims=True) * (1.0 / RW_HEAD_DIM)
            ost = (d * lax.rsqrt(var + GN_EPS) * lnw_ref[:, lanes]
                   + jnp.where(same_head, lnb_ref[:, lanes], 0.0) + bon_s[u])
            o_ref[0, rows, lanes] = ((ost[:C] + ost[C:]) * f32(g_ref[0, rows, lanes])).astype(BF16)
    for p in range(n_pairs):
        s_scr[p] = s[p]
        sout_ref[0, p] = s[p]


def _wkv(rt, at, bt, kt, v, g, wc, s0bd, rk, lnw, lnb, *, tb, n_pairs):
    bx, t, _ = rt.shape
    n_chunks = tb // WKV_CHUNK
    n_units = n_pairs * n_chunks
    w = n_pairs * LANES
    seq = pl.BlockSpec((1, tb, w), lambda b, p, i: (b, i, p))
    vec = pl.BlockSpec((1, w), lambda b, p, i: (0, p))
    st = pl.BlockSpec((1, n_pairs, LANES, LANES), lambda b, p, i: (b, p, 0, 0))
    mat = lambda dt: pltpu.VMEM((n_units, LANES, LANES), dt)
    return pl.pallas_call(
        functools.partial(_wkv_kernel, n_pairs=n_pairs, n_chunks=n_chunks),
        grid=(bx, N_PAIRS // n_pairs, t // tb),
        in_specs=[seq] * 6 + [pl.BlockSpec((1, n_chunks, 1, w), lambda b, p, i: (b, i, 0, p)),
                              st, vec, vec, vec],
        out_specs=[seq, st],
        out_shape=[jax.ShapeDtypeStruct((bx, t, RW_WIDTH), BF16),
                   jax.ShapeDtypeStruct((bx, N_PAIRS, LANES, LANES), F32)],
        scratch_shapes=[pltpu.VMEM((n_pairs, LANES, LANES), F32),
                        mat(BF16), mat(BF16), mat(F32), mat(F32), mat(F32)],
        compiler_params=_cparams(3),
        name="wkv",
    )(rt, at, bt, kt, v, g, wc, s0bd, rk, lnw, lnb)


def _merge_kernel(x_ref, oda_ref, orw_ref, gt_ref, wda_ref, wrw_ref, wout_ref, o_ref):
    a = _dot(oda_ref[0], wda_ref[...])
    b = _dot(orw_ref[0], wrw_ref[...])
    gt = gt_ref[0].astype(F32)
    m = _sigmoid(gt[:, :D_MODEL]) * a + _sigmoid(gt[:, D_MODEL:]) * b
    o_ref[0] = x_ref[0] + _dot(m.astype(BF16), wout_ref[...])


def _merge(x, oda, orw, gt, wda, wrw, wout, *, tm):
    bx, t, _ = x.shape
    row = lambda w: pl.BlockSpec((1, tm, w), lambda b, i: (b, i, 0))
    sq = _resident((D_MODEL, D_MODEL))
    return pl.pallas_call(
        _merge_kernel,
        grid=(bx, t // tm),
        in_specs=[row(D_MODEL), row(DA_WIDTH), row(RW_WIDTH), row(P_GATE), sq, sq, sq],
        out_specs=row(D_MODEL),
        out_shape=jax.ShapeDtypeStruct((bx, t, D_MODEL), F32),
        compiler_params=_cparams(2),
        name="merge",
    )(x, oda, orw, gt, wda, wrw, wout)


_FF_CW = 256
_HALO = SUBLANES


def _ffn_kernel(x_ref, halo_ref, c0_ref, g_ref, wup_ref, f_ref, fb_ref, wdn_ref, gf_ref,
                o_ref, conv_ref, ext_s, act_s, *, tm, final):
    first = pl.program_id(1) == 0
    xe = jnp.concatenate([halo_ref[0], x_ref[0]], axis=0)
    hb = _rms(xe, g_ref[...]).astype(BF16)
    starts = list(range(0, D_FF, _FF_CW))

    def up_proj(c0):
        for base in (c0, D_FF + c0):
            cols = slice(base, base + _FF_CW)
            up = _dot(hb, wup_ref[:, cols])
            ext_s[:, cols] = up
            ext_s[_HALO - 2:_HALO, cols] = jnp.where(first, c0_ref[0, :, cols], up[_HALO - 2:_HALO])
            conv_ref[0, :, cols] = up[_HALO + tm - 2:_HALO + tm]

    def conv_act(c0):
        cs = []
        for base in (c0, D_FF + c0):
            cols = slice(base, base + _FF_CW)
            f = f_ref[:, cols]
            cs.append(fb_ref[:, cols]
                      + ext_s[_HALO - 2:_HALO - 2 + tm, cols] * f[0:1]
                      + ext_s[_HALO - 1:_HALO - 1 + tm, cols] * f[1:2]
                      + ext_s[_HALO:_HALO + tm, cols] * f[2:3])
        ca, cb = cs
        act_s[:, c0:c0 + _FF_CW] = (ca * _sigmoid(ca) * cb).astype(BF16)

    up_proj(starts[0])
    for j, c0 in enumerate(starts):
        if j + 1 < len(starts):
            up_proj(starts[j + 1])
        conv_act(c0)
    out = x_ref[0] + _dot(act_s[...], wdn_ref[...])
    if final:
        out = _rms(out, gf_ref[...])
    o_ref[0] = out


def _ffn(x, conv0, g, wup, f, fb, wdn, gfinal, *, tm, final):
    bx, t, _ = x.shape
    row = pl.BlockSpec((1, tm, D_MODEL), lambda b, i: (b, i, 0))
    halo = pl.BlockSpec((1, _HALO, D_MODEL),
                        lambda b, i: (b, jnp.maximum(i * (tm // _HALO) - 1, 0), 0))
    cst = pl.BlockSpec((1, CONV_W - 1, 2 * D_FF), lambda b, i: (b, 0, 0))
    return pl.pallas_call(
        functools.partial(_ffn_kernel, tm=tm, final=final),
        grid=(bx, t // tm),
        in_specs=[row, halo, cst, _resident((1, D_MODEL)), _resident((D_MODEL, 2 * D_FF)),
                  _resident((CONV_W, 2 * D_FF)), _resident((1, 2 * D_FF)), _resident((D_FF, D_MODEL)),
                  _resident((1, D_MODEL))],
        out_specs=[row, cst],
        out_shape=[jax.ShapeDtypeStruct((bx, t, D_MODEL), F32),
                   jax.ShapeDtypeStruct((bx, CONV_W - 1, 2 * D_FF), F32)],
        scratch_shapes=[pltpu.VMEM((_HALO + tm, 2 * D_FF), F32), pltpu.VMEM((tm, D_FF), BF16)],
        compiler_params=_cparams(2),
        name="ffn",
    )(x, x, conv0, g, wup, f, fb, wdn, gfinal)


def _rope_tables(pos):
    half = ROT_DIM // 2
    inv = ROPE_THETA ** (-jnp.arange(0, ROT_DIM, 2, dtype=F32) / ROT_DIM)
    ang = pos.astype(F32)[:, None] * inv[None, :]
    cos, sin = jnp.cos(ang), jnp.sin(ang)
    t = pos.shape[0]
    pad = jnp.zeros((t, DA_HEAD_DIM - ROT_DIM), F32)
    z = jnp.zeros((t, half), F32)
    one_map = lambda a, b, fill: jnp.concatenate([a, b, pad + fill], axis=1)
    rc = one_map(cos, cos, 1.0)
    ra = one_map(z, sin, 0.0)
    rb = one_map(-sin, z, 0.0)
    dup = lambda m: jnp.concatenate([m, m], axis=1)
    return dup(rc), dup(ra), dup(rb)


def _state_to_blockdiag(s):
    b = s.shape[0]
    s = s.reshape(b, N_PAIRS, 2, RW_HEAD_DIM, RW_HEAD_DIM)
    z = jnp.zeros_like(s[:, :, 0])
    top = jnp.concatenate([s[:, :, 0], z], axis=-1)
    bot = jnp.concatenate([z, s[:, :, 1]], axis=-1)
    return jnp.concatenate([top, bot], axis=-2)


def _blockdiag_to_state(sbd):
    b = sbd.shape[0]
    h0 = sbd[:, :, :RW_HEAD_DIM, :RW_HEAD_DIM]
    h1 = sbd[:, :, RW_HEAD_DIM:, RW_HEAD_DIM:]
    return jnp.stack([h0, h1], axis=2).reshape(b, RW_HEADS, RW_HEAD_DIM, RW_HEAD_DIM)


def _layer(x, lidx, depth, tabs, kvbuf, cache, wkv0, shift0, conv0, w, norm_final, *, final):
    bx, t, _ = x.shape
    prompt = cache is None
    lam_init = 0.8 - 0.6 * math.exp(-0.3 * lidx)
    kb_in, vb_in = kvbuf if kvbuf is not None else (None, None)

    if prompt:
        tm = ROW_TILE
        q, kbuf, vbuf, u_rw, gate = _in_proj(x, w["norm_mix"], w["w_in"], *tabs, kb_in, vb_in,
                                             layer=lidx, depth=depth, tm=tm)
        o_da = _attn_prompt(w["da_lambda"], w["da_subln"], q, kbuf, vbuf, layer=lidx, lam_init=lam_init)
    else:
        tm = t
        flat = lambda a: a.reshape(1, bx * t, a.shape[-1])
        q, kbuf, vbuf, u_rw, gate = _in_proj(flat(x), w["norm_mix"], w["w_in"], *tabs, kb_in, vb_in,
                                             layer=lidx, depth=depth, tm=bx * t)
        u_rw = u_rw.reshape(bx, t, P_RW)
        padq = ((0, 0), (0, LANES - t), (0, 0))
        unflat = lambda a: a.reshape(bx, t, DA_WIDTH)
        knew = jnp.pad(unflat(kbuf[lidx]), padq)[None]
        vnew = jnp.pad(unflat(vbuf[lidx]), padq)[None]
        o_da = _attn_sample(w["da_lambda"], w["da_subln"], jnp.pad(unflat(q), padq), knew, vnew, *cache,
                            layer=lidx, lam_init=lam_init, n_valid_new=t)[:, :t]

    chunk = WKV_CHUNK if prompt else t
    rt, at, bt, kt, vr, g, wc = _rw_prep(
        u_rw, shift0, w["rw_mu"], w["rw_w0"], w["rw_w2p"], w["rw_a0"], w["rw_a2p"], w["rw_g2"],
        w["rw_k_k"], w["rw_k_a"], w["seg"], w["tri_p"] if prompt else w["tri_s"], tm=tm, chunk=chunk)
    if not prompt:
        padt = ((0, 0), (0, WKV_CHUNK - t), (0, 0))
        rt, at, bt, kt, vr, g = [jnp.pad(a, padt) for a in (rt, at, bt, kt, vr, g)]
    o_rw, sbd = _wkv(rt, at, bt, kt, vr, g, wc, _state_to_blockdiag(wkv0),
                     w["rw_r_k"], w["rw_ln_w"], w["rw_ln_b"],
                     tb=WKV_ROWS if prompt else WKV_CHUNK, n_pairs=WKV_PAIRS if prompt else N_PAIRS)
    o_rw = o_rw[:, :t]

    if prompt:
        x = _merge(x, o_da, o_rw, gate, w["w_o_da"], w["w_o_rw"], w["w_out"], tm=tm)
    else:
        x = _merge(flat(x), flat(o_da), flat(o_rw), gate, w["w_o_da"], w["w_o_rw"], w["w_out"],
                   tm=bx * t).reshape(bx, t, D_MODEL)
    x, new_conv = _ffn(x, conv0, w["norm_ffn"], w["w_up"], w["ffn_conv"], w["ffn_conv_b"], w["w_down"],
                       norm_final, tm=tm, final=final)
    return x, (kbuf, vbuf), _blockdiag_to_state(sbd), u_rw[:, -1:], new_conv


def kernel(x_prompt, x_sample, cache_k, cache_v, state_wkv, state_shift, state_ffn_conv, norm_mix, w_in, da_lambda, da_subln, w_o_da, rw_mu, rw_w0, rw_w2, rw_a0, rw_a2, rw_g2, rw_k_k, rw_k_a, rw_r_k, rw_ln_w, rw_ln_b, w_o_rw, w_out, norm_ffn, w_up, ffn_conv, ffn_conv_b, w_down, norm_final):
    bp, tp, _ = x_prompt.shape
    bs, ts, _ = x_sample.shape
    depth = w_in.shape[0]
    past = cache_k.shape[2]
    tabs_p = _rope_tables(jnp.arange(tp, dtype=jnp.int32))
    tabs_s = tuple(jnp.tile(a, (bs, 1)) for a in _rope_tables(past + jnp.arange(ts, dtype=jnp.int32)))
    cache = (cache_k.reshape(depth, bs, past, DA_WIDTH), cache_v.reshape(depth, bs, past, DA_WIDTH))

    lane = jnp.arange(LANES)
    seg = (lane[:, None] // RW_HEAD_DIM == lane[None, :] // RW_HEAD_DIM).astype(BF16)
    tri = lambda n: (jnp.arange(n)[:, None] >= jnp.arange(n)[None, :]).astype(BF16)
    tri_p, tri_s = tri(WKV_CHUNK), tri(ts)
    zl = jnp.zeros((DECAY_LORA, RW_WIDTH), F32)
    row = lambda a: a.reshape(1, -1)
    zero = lambda *s: jnp.zeros(s, x_prompt.dtype)
    wkv_z, shift_z, conv_z = zero(bp, RW_HEADS, RW_HEAD_DIM, RW_HEAD_DIM), zero(bp, 1, P_RW), zero(bp, CONV_W - 1, 2 * D_FF)

    xp, xs = x_prompt, x_sample
    kv_p = kv_s = None
    outs_p, outs_s = [], []
    nf = row(norm_final)
    for l in range(depth):
        w = dict(
            norm_mix=row(norm_mix[l]), w_in=w_in[l].astype(BF16), da_lambda=da_lambda[l],
            da_subln=da_subln[l].reshape(-1, 1), w_o_da=w_o_da[l].astype(BF16),
            rw_mu=row(rw_mu[l]), rw_w0=row(rw_w0[l]),
            rw_w2p=jnp.concatenate([rw_w2[l], zl], axis=0).astype(BF16),
            rw_a0=row(rw_a0[l]), rw_a2p=jnp.concatenate([zl, rw_a2[l]], axis=0).astype(BF16),
            rw_g2=rw_g2[l].astype(BF16), rw_k_k=row(rw_k_k[l]), rw_k_a=row(rw_k_a[l]),
            rw_r_k=row(rw_r_k[l]), rw_ln_w=row(rw_ln_w[l]), rw_ln_b=row(rw_ln_b[l]),
            w_o_rw=w_o_rw[l].astype(BF16), w_out=w_out[l].astype(BF16), norm_ffn=row(norm_ffn[l]),
            w_up=w_up[l].astype(BF16), ffn_conv=ffn_conv[l], ffn_conv_b=row(ffn_conv_b[l]),
            w_down=w_down[l].astype(BF16), seg=seg, tri_p=tri_p, tri_s=tri_s)
        final = l == depth - 1
        xp, kv_p, sp, shp, cp = _layer(xp, l, depth, tabs_p, kv_p, None, wkv_z, shift_z, conv_z, w, nf, final=final)
        xs, kv_s, sq, shq, cq = _layer(xs, l, depth, tabs_s, kv_s, cache, state_wkv[l], state_shift[l],
                                       state_ffn_conv[l], w, nf, final=final)
        outs_p.append((sp, shp, cp))
        outs_s.append((sq, shq, cq))

    wkv_prompt, shift_prompt, conv_prompt = [jnp.stack(t) for t in zip(*outs_p)]
    wkv_sample, shift_sample, conv_sample = [jnp.stack(t) for t in zip(*outs_s)]
    heads = lambda a, b, t: a.reshape(depth, b, t, DA_HEADS, DA_V_DIM)
    return (xp, xs, heads(kv_p[0], bp, tp), heads(kv_p[1], bp, tp), wkv_prompt, shift_prompt, conv_prompt,
            heads(kv_s[0], bs, ts), heads(kv_s[1], bs, ts), wkv_sample, shift_sample, conv_sample)
```

```python
import functools
import math

import jax
import jax.numpy as jnp
from jax import lax
from jax.experimental import pallas as pl
from jax.experimental.pallas import tpu as pltpu

F32 = jnp.float32
BF16 = jnp.bfloat16

D_MODEL = 1024
CHUNK = 64
EPS = 1e-6
DA_HEADS = 8
DA_HEAD_DIM = 64
DA_V_DIM = 128
DA_WIDTH = DA_HEADS * DA_V_DIM
ROPE_THETA = 500000.0
ROT_DIM = DA_HEAD_DIM // 4
RW_HEAD_DIM = 64
RW_HEADS = 16
RW_WIDTH = 1024
DECAY_LORA = 64
AAA_LORA = 64
GATE_LORA = 128
GN_EPS = 64e-5
D_FF = 2816
CONV_W = 3
P_DA = 3 * DA_WIDTH
P_RW = 3 * RW_WIDTH + DECAY_LORA + AAA_LORA + GATE_LORA
P_GATE = 2 * D_MODEL
P_TOTAL = P_DA + P_RW + P_GATE

LANES = 128
SUBLANES = 8
VMEM_LIMIT = 56 * 1024 * 1024

WKV_CHUNK = 64
N_PAIRS = RW_HEADS // 2
NEG_BIG = -1e30
ROW_TILE = 256
WKV_ROWS = 512
WKV_PAIRS = 2

_ARB = "arbitrary"


def _cparams(n_axes):
    return pltpu.CompilerParams(dimension_semantics=(_ARB,) * n_axes,
                                vmem_limit_bytes=VMEM_LIMIT)


def _resident(shape):
    nd = len(shape)
    return pl.BlockSpec(shape, lambda *_: (0,) * nd, pipeline_mode=pl.Buffered(1))


def _dot(a, b):
    return jnp.dot(a, b, preferred_element_type=F32)


def _dot_nt(a, b):
    return lax.dot_general(a, b, (((1,), (1,)), ((), ())), preferred_element_type=F32)


def _rms(x, g):
    return x * lax.rsqrt(jnp.mean(x * x, axis=-1, keepdims=True) + EPS) * g


def _sigmoid(x):
    return 1.0 / (1.0 + jnp.exp(-x))


def _split3(x):
    hi = x.astype(BF16)
    r1 = x - hi.astype(F32)
    mid = r1.astype(BF16)
    lo = (r1 - mid.astype(F32)).astype(BF16)
    return hi, mid, lo


def _prep_rows(us, w0_ref, w2_ref, a0_ref, a2_ref, g2_ref, kk_ref, ka_ref, seg_ref):
    r = us[:, 0:RW_WIDTH]
    kr = us[:, RW_WIDTH:2 * RW_WIDTH]
    vr = us[:, 2 * RW_WIDTH:3 * RW_WIDTH]
    wa = us[:, 3 * RW_WIDTH:3 * RW_WIDTH + DECAY_LORA + AAA_LORA]
    gd = us[:, 3 * RW_WIDTH + DECAY_LORA + AAA_LORA:]

    z = w0_ref[...] + _dot(jnp.tanh(wa).astype(BF16), w2_ref[...])
    nz = -z
    w_log = -(jnp.maximum(nz, 0.0) + jnp.log1p(jnp.exp(-jnp.abs(nz)))) - 0.5
    lw = -jnp.exp(w_log)
    a = _sigmoid(a0_ref[...] + _dot(wa.astype(BF16), a2_ref[...]))
    g = _dot(_sigmoid(gd).astype(BF16), g2_ref[...])

    kk = kr * kk_ref[...]
    k2 = (kk * kk).astype(BF16)
    seg = seg_ref[...]
    w = seg.shape[0]
    ssum = jnp.concatenate([_dot(k2[:, c:c + w], seg) for c in range(0, RW_WIDTH, w)],
                           axis=1)
    kk = kk * lax.rsqrt(jnp.maximum(ssum, 1e-24))
    kmod = kr * (1.0 + (a - 1.0) * ka_ref[...])
    return r, vr, g, lw, a, kk, kmod


def _prep_chunk(r, lw, a, kk, kmod, tri):
    cum = _dot(tri, jnp.concatenate(_split3(lw), axis=0))
    e_pos = jnp.exp(cum)
    e_neg = jnp.exp(-cum)
    rt = r * e_pos
    at = -kk * jnp.exp(cum - lw)
    bt = kk * a * e_neg
    kt = kmod * e_neg
    return rt, at, bt, kt, e_pos[-1:, :]


def _rw_prep_kernel(u_ref, sh_ref, mu_ref, w0_ref, w2_ref, a0_ref, a2_ref, g2_ref,
                    kk_ref, ka_ref, seg_ref, tri_ref,
                    rt_ref, at_ref, bt_ref, kt_ref, v_ref, g_ref, wc_ref, *, tm):
    u = u_ref[0]
    rows = lax.broadcasted_iota(jnp.int32, (tm, 1), 0)
    up = jnp.where(rows == 0, sh_ref[0], pltpu.roll(u, 1, 0))
    us = u + (up - u) * mu_ref[...]
    r, vr, g, lw, a, kk, kmod = _prep_rows(us, w0_ref, w2_ref, a0_ref, a2_ref, g2_ref, kk_ref, ka_ref, seg_ref)
    rt, at, bt, kt, wc = _prep_chunk(r, lw, a, kk, kmod, tri_ref[...])
    for ref, val in ((rt_ref, rt), (at_ref, at), (bt_ref, bt), (kt_ref, kt), (v_ref, vr), (g_ref, g)):
        ref[0] = val.astype(BF16)
    wc_ref[0, 0] = wc


def _prep_param_specs(chunk):
    return [_resident((1, P_RW)), _resident((1, RW_WIDTH)),
            _resident((DECAY_LORA + AAA_LORA, RW_WIDTH)), _resident((1, RW_WIDTH)),
            _resident((DECAY_LORA + AAA_LORA, RW_WIDTH)), _resident((GATE_LORA, RW_WIDTH)),
            _resident((1, RW_WIDTH)), _resident((1, RW_WIDTH)),
            _resident((2 * LANES, 2 * LANES)), _resident((chunk, 3 * chunk))]


def _rw_prep(u, shift0, prep, *, tm):
    bx, t, _ = u.shape
    assert t == tm
    row = lambda w: pl.BlockSpec((1, tm, w), lambda b: (b, 0, 0))
    outs = [jax.ShapeDtypeStruct((bx, t, RW_WIDTH), BF16)] * 6
    outs.append(jax.ShapeDtypeStruct((bx, 1, 1, RW_WIDTH), F32))
    return pl.pallas_call(
        functools.partial(_rw_prep_kernel, tm=tm),
        grid=(bx,),
        in_specs=[row(P_RW), pl.BlockSpec((1, 1, P_RW), lambda b: (b, 0, 0))] + _prep_param_specs(tm),
        out_specs=[row(RW_WIDTH)] * 6 + [pl.BlockSpec((1, 1, 1, RW_WIDTH), lambda b: (b, 0, 0, 0))],
        out_shape=outs,
        compiler_params=_cparams(1),
        name="rw_prep",
    )(u, shift0, *prep)


_IN_CW = 256
_LOG2E = math.log2(math.e)
_N_PREP = 10


def _in_proj_kernel(*refs, tm, fused, aliased):
    x_ref, g_ref, w_ref, rc_ref, ra_ref, rb_ref = refs[:6]
    n_in = 6
    if fused:
        sh_ref = refs[6]
        mu_ref, w0_ref, w2_ref, a0_ref, a2_ref, g2_ref, kk_ref, ka_ref, seg_ref, tri_ref = refs[7:7 + _N_PREP]
        n_in = 7 + _N_PREP
    n_in += 2 if aliased else 0
    q_ref, kn_ref, vn_ref, ko_ref, vo_ref, gt_ref = refs[n_in:n_in + 6]
    if fused:
        rt_ref, at_ref, bt_ref, kt_ref, vr_ref, gr_ref, wc_ref, last_ref = refs[n_in + 6:n_in + 14]
        k_scr, v_scr, u_s = refs[n_in + 14:]
    else:
        u_ref = refs[n_in + 6]
        k_scr, v_scr = refs[n_in + 7:]
    hb = _rms(x_ref[0], g_ref[...]).astype(BF16)
    rc = rc_ref[...]
    ra = ra_ref[...]
    rb = rb_ref[...]

    def rope(t):
        return t * rc + pltpu.roll(t, ROT_DIM // 2, 1) * ra + pltpu.roll(t, LANES - ROT_DIM // 2, 1) * rb

    def chunk(c0):
        if P_DA <= c0 < P_DA + P_RW:
            o = c0 - P_DA
            if fused:
                u_s[SUBLANES:, o:o + _IN_CW] = _dot(hb, w_ref[:, c0:c0 + _IN_CW])
            else:
                u_ref[0, :, o:o + _IN_CW] = _dot(hb, w_ref[:, c0:c0 + _IN_CW])
            return
        acc = _dot(hb, w_ref[:, c0:c0 + _IN_CW])
        if c0 < DA_WIDTH:
            for s in range(0, _IN_CW, LANES):
                q_ref[0, :, c0 + s:c0 + s + LANES] = (
                    rope(acc[:, s:s + LANES]) * (DA_HEAD_DIM ** -0.5 * _LOG2E)).astype(BF16)
        elif c0 < 2 * DA_WIDTH:
            o = c0 - DA_WIDTH
            for s in range(0, _IN_CW, LANES):
                kr = rope(acc[:, s:s + LANES])
                k_scr[:, o + s:o + s + LANES] = kr
                kn_ref[0, :, o + s:o + s + LANES] = kr.astype(BF16)
        elif c0 < P_DA:
            o = c0 - 2 * DA_WIDTH
            v_scr[:, o:o + _IN_CW] = acc
            vn_ref[0, :, o:o + _IN_CW] = acc.astype(BF16)
        else:
            o = c0 - P_DA - P_RW
            gt_ref[0, :, o:o + _IN_CW] = acc.astype(BF16)

    def chunks(lo, hi):
        for c0 in range(lo, hi, _IN_CW):
            chunk(c0)

    if not fused:
        chunks(0, P_TOTAL)
    else:
        @pl.when(pl.program_id(1) == 0)
        def _():
            u_s[SUBLANES - 1:SUBLANES, :] = sh_ref[0]

        chunks(P_DA, P_DA + P_RW)
        chunks(0, DA_WIDTH)
        u = u_s[SUBLANES:SUBLANES + tm, :]
        up = u_s[SUBLANES - 1:SUBLANES - 1 + tm, :]
        us = u + (up - u) * mu_ref[...]
        last = u[tm - 1:tm, :]
        last_ref[0] = last
        u_s[SUBLANES - 1:SUBLANES, :] = last
        r, vr, g, lw, a, kk, kmod = _prep_rows(us, w0_ref, w2_ref, a0_ref, a2_ref, g2_ref, kk_ref, ka_ref, seg_ref)
        vr_ref[0] = vr.astype(BF16)
        gr_ref[0] = g.astype(BF16)
        chunks(DA_WIDTH, P_DA)
        chunks(P_DA + P_RW, P_DA + P_RW + P_GATE // 2)
        tri = tri_ref[...]
        for c in range(tm // WKV_CHUNK):
            sl = slice(c * WKV_CHUNK, (c + 1) * WKV_CHUNK)
            rt, at, bt, kt, wc = _prep_chunk(r[sl], lw[sl], a[sl], kk[sl], kmod[sl], tri)
            for ref, val in ((rt_ref, rt), (at_ref, at), (bt_ref, bt), (kt_ref, kt)):
                ref[0, sl, :] = val.astype(BF16)
            wc_ref[0, c] = wc
        chunks(P_DA + P_RW + P_GATE // 2, P_TOTAL)
    ko_ref[0, 0] = k_scr[...].reshape(tm, DA_HEADS, DA_V_DIM)
    vo_ref[0, 0] = v_scr[...].reshape(tm, DA_HEADS, DA_V_DIM)


def _in_proj(x, g, w_bf, tabs, kvout, *, layer, depth, tm, fused, shift0=None, prep=None):
    bx, t, _ = x.shape
    row = lambda w: pl.BlockSpec((1, tm, w), lambda b, i: (b, i, 0))
    tab = pl.BlockSpec((tm, LANES), lambda b, i: (i, 0))
    slot = pl.BlockSpec((1, 1, tm, DA_HEADS, DA_V_DIM), lambda b, i: (layer, b, i, 0, 0))
    aliased = kvout is not None
    in_specs = [row(D_MODEL), _resident((1, D_MODEL)), _resident((D_MODEL, P_TOTAL)), tab, tab, tab]
    args = [x, g, w_bf, *tabs]
    if fused:
        in_specs += [pl.BlockSpec((1, 1, P_RW), lambda b, i: (b, 0, 0))] + _prep_param_specs(WKV_CHUNK)
        args += [shift0, *prep]
    if aliased:
        aliases = {len(args): 3, len(args) + 1: 4}
        in_specs += [pl.BlockSpec(memory_space=pl.ANY)] * 2
        args += list(kvout)
    else:
        aliases = {}
    nat = jax.ShapeDtypeStruct((bx, t, DA_WIDTH), BF16)
    stacked = jax.ShapeDtypeStruct((depth, bx, t, DA_HEADS, DA_V_DIM), F32)
    out_specs = [row(DA_WIDTH)] * 3 + [slot, slot, row(P_GATE)]
    out_shape = [nat, nat, nat, stacked, stacked, jax.ShapeDtypeStruct((bx, t, P_GATE), BF16)]
    scratch = [pltpu.VMEM((tm, DA_WIDTH), F32)] * 2
    if fused:
        nc = tm // WKV_CHUNK
        out_specs += [row(RW_WIDTH)] * 6 + [pl.BlockSpec((1, nc, 1, RW_WIDTH), lambda b, i: (b, i, 0, 0)),
                                            pl.BlockSpec((1, 1, P_RW), lambda b, i: (b, 0, 0))]
        out_shape += [jax.ShapeDtypeStruct((bx, t, RW_WIDTH), BF16)] * 6
        out_shape += [jax.ShapeDtypeStruct((bx, t // WKV_CHUNK, 1, RW_WIDTH), F32),
                      jax.ShapeDtypeStruct((bx, 1, P_RW), F32)]
        scratch.append(pltpu.VMEM((SUBLANES + tm, P_RW), F32))
    else:
        out_specs.append(row(P_RW))
        out_shape.append(jax.ShapeDtypeStruct((bx, t, P_RW), F32))
    return pl.pallas_call(
        functools.partial(_in_proj_kernel, tm=tm, fused=fused, aliased=aliased),
        grid=(bx, t // tm),
        in_specs=in_specs,
        out_specs=out_specs,
        out_shape=out_shape,
        scratch_shapes=scratch,
        input_output_aliases=aliases,
        compiler_params=_cparams(2),
        name="in_proj",
    )(*args)


_TK = 256
_CHUNK_SHIFT = CHUNK.bit_length() - 1


def _lambda(lam_ref, lam_init):
    lp = lam_ref[...]
    return (jnp.exp(jnp.sum(lp[0:1] * lp[1:2], axis=1, keepdims=True))
            - jnp.exp(jnp.sum(lp[2:3] * lp[3:4], axis=1, keepdims=True)) + lam_init)


def _q_blockdiag(q_tile):
    tq = q_tile.shape[0]
    qt = q_tile.astype(F32).T.astype(BF16)
    z = jnp.zeros((DA_HEAD_DIM, tq), BF16)
    return jnp.concatenate([jnp.concatenate([qt[:DA_HEAD_DIM], z], axis=1),
                            jnp.concatenate([z, qt[DA_HEAD_DIM:]], axis=1)], axis=0)


def _softmax_step(s, stats, mask, tq):
    new_stats, alphas, ps = [], [], []
    for c in range(2):
        m, l = stats[2 * c:2 * c + 2]
        sc = s[:, c * tq:(c + 1) * tq]
        if mask is not None:
            sc = jnp.where(mask, sc, NEG_BIG)
        m_new = jnp.maximum(m, jnp.max(sc, axis=0, keepdims=True))
        alpha = jnp.exp2(m - m_new)
        p = jnp.exp2(sc - m_new)
        new_stats += [m_new, alpha * l + jnp.sum(p, axis=0, keepdims=True)]
        alphas.append(alpha)
        ps.append(p.astype(BF16))
    return tuple(new_stats), tuple(alphas), tuple(ps)


def _attn_finish(a1, a2, l1, l2, lam, sub_col, lam_init):
    ot = a1 * (1.0 / l1) - lam * (a2 * (1.0 / l2))
    ss = jnp.sum(ot * ot, axis=0, keepdims=True) * (1.0 / DA_V_DIM)
    ot = ot * lax.rsqrt(ss + EPS) * sub_col * (1.0 - lam_init)
    return ot.T.astype(BF16)


def _attn_prompt_kernel(lam_ref, sub_ref, q_ref, k_ref, v_ref, o_ref, vt_s, s_s, p_s, acc_s,
                        *, n_tiles, lam_init):
    tq = _TK
    for jb in range(n_tiles):
        vt_s[jb] = v_ref[0, jb * _TK:(jb + 1) * _TK, :].astype(F32).T.astype(BF16)
    lam = _lambda(lam_ref, lam_init)
    sub_col = sub_ref[...]
    kr = jnp.right_shift(lax.broadcasted_iota(jnp.int32, (_TK, tq), 0), _CHUNK_SHIFT)
    qc = jnp.right_shift(lax.broadcasted_iota(jnp.int32, (_TK, tq), 1), _CHUNK_SHIFT)
    diag_mask = kr <= qc
    init_stats = (jnp.full((1, tq), NEG_BIG, F32), jnp.zeros((1, tq), F32)) * 2

    steps = [(i, j) for i in range(n_tiles) for j in range(i + 1)]
    qbd = {}
    alphas_of = {}
    final_stats = {}

    def scores(n):
        i, j = steps[n]
        if j == 0:
            qbd[i] = _q_blockdiag(q_ref[0, i * tq:(i + 1) * tq, :])
        s_s[n % 2] = _dot(k_ref[0, j * _TK:(j + 1) * _TK, :], qbd[i])

    def values(n):
        i, j = steps[n]
        for c in range(2):
            pv = _dot(vt_s[j], p_s[n % 2, c])
            acc_s[c] = pv if j == 0 else alphas_of[n][c] * acc_s[c] + pv
        if j == i:
            _, l1, _, l2 = final_stats[i]
            o_ref[0, i * tq:(i + 1) * tq, :] = _attn_finish(acc_s[0], acc_s[1], l1, l2, lam, sub_col, lam_init)

    scores(0)
    stats = init_stats
    for n, (i, j) in enumerate(steps):
        if n + 1 < len(steps):
            scores(n + 1)
        if n >= 1:
            values(n - 1)
        if j == 0:
            stats = init_stats
        stats, alphas_of[n], ps = _softmax_step(s_s[n % 2], stats, diag_mask if j == i else None, tq)
        for c in range(2):
            p_s[n % 2, c] = ps[c]
        if j == i:
            final_stats[i] = stats
    values(len(steps) - 1)


def _attn_prompt(da_lambda, subln_col, q, k, v, *, lam_init):
    bx, t, _ = q.shape
    n_tiles = t // _TK
    seq = pl.BlockSpec((1, t, LANES), lambda b, h: (b, 0, h))
    return pl.pallas_call(
        functools.partial(_attn_prompt_kernel, n_tiles=n_tiles, lam_init=lam_init),
        grid=(bx, DA_HEADS),
        in_specs=[_resident((4, DA_HEAD_DIM)), _resident((DA_V_DIM, 1)), seq, seq, seq],
        out_specs=seq,
        out_shape=jax.ShapeDtypeStruct((bx, t, DA_WIDTH), BF16),
        scratch_shapes=[pltpu.VMEM((n_tiles, DA_V_DIM, _TK), BF16),
                        pltpu.VMEM((2, _TK, 2 * _TK), F32), pltpu.VMEM((2, 2, _TK, _TK), BF16),
                        pltpu.VMEM((2, DA_V_DIM, _TK), F32)],
        compiler_params=_cparams(2),
        name="diff_attn",
    )(da_lambda, subln_col, q, k, v)


def _attn_sample_kernel(lam_ref, sub_ref, q_ref, k_ref, v_ref, kp_ref, vp_ref, o_ref, kb_s, vt_s,
                        *, tq, n_new, n_past, n_valid_new, lam_init):
    for r0 in range(0, n_past, _TK):
        kb_s[r0:r0 + _TK, :] = kp_ref[0, 0, r0:r0 + _TK, :].astype(BF16)
        vt_s[r0 // _TK] = vp_ref[0, 0, r0:r0 + _TK, :].T.astype(BF16)
    kb_s[n_past:n_past + n_new, :] = k_ref[0]
    vt_s[n_past // _TK, :, 0:n_new] = v_ref[0].astype(F32).T.astype(BF16)
    lam = _lambda(lam_ref, lam_init)
    qbd = _q_blockdiag(q_ref[0])
    blocks = [(kb_s[r0:r0 + _TK, :], vt_s[r0 // _TK], None) for r0 in range(0, n_past, _TK)]
    valid = lax.broadcasted_iota(jnp.int32, (n_new, tq), 0) < n_valid_new
    blocks.append((kb_s[n_past:n_past + n_new, :], vt_s[n_past // _TK, :, 0:n_new], valid))
    scores = [_dot(kblk, qbd) for kblk, _, _ in blocks]
    stats = (jnp.full((1, tq), NEG_BIG, F32), jnp.zeros((1, tq), F32)) * 2
    accs = (jnp.zeros((DA_V_DIM, tq), F32),) * 2
    for s, (_, vtblk, mask) in zip(scores, blocks):
        stats, alphas, ps = _softmax_step(s, stats, mask, tq)
        accs = tuple(alphas[c] * accs[c] + _dot(vtblk, ps[c]) for c in range(2))
    _, l1, _, l2 = stats
    o_ref[0] = _attn_finish(accs[0], accs[1], l1, l2, lam, sub_ref[...], lam_init)


def _attn_sample(da_lambda, subln_col, q, k, v, past_k, past_v, *, layer, lam_init, n_valid_new):
    bx, tq, _ = q.shape
    n_new = k.shape[1]
    n_past = past_k.shape[2]
    new = pl.BlockSpec((1, n_new, LANES), lambda b, h: (b, 0, h))
    old = pl.BlockSpec((1, 1, n_past, LANES), lambda b, h: (layer, b, 0, h))
    return pl.pallas_call(
        functools.partial(_attn_sample_kernel, tq=tq, n_new=n_new, n_past=n_past,
                          n_valid_new=n_valid_new, lam_init=lam_init),
        grid=(bx, DA_HEADS),
        in_specs=[_resident((4, DA_HEAD_DIM)), _resident((DA_V_DIM, 1)),
                  pl.BlockSpec((1, tq, LANES), lambda b, h: (b, 0, h)), new, new, old, old],
        out_specs=pl.BlockSpec((1, tq, LANES), lambda b, h: (b, 0, h)),
        out_shape=jax.ShapeDtypeStruct((bx, tq, DA_WIDTH), BF16),
        scratch_shapes=[pltpu.VMEM((n_past + n_new, LANES), BF16),
                        pltpu.VMEM((pl.cdiv(n_past + n_new, _TK), DA_V_DIM, _TK), BF16)],
        compiler_params=_cparams(2),
        name="diff_attn_step",
    )(da_lambda, subln_col, q, k, v, past_k, past_v)


def _wkv_kernel(rt_ref, at_ref, bt_ref, kt_ref, v_ref, g_ref, wc_ref, s0_ref, rk_ref, lnw_ref, lnb_ref,
                o_ref, sout_ref, s_scr, rp_s, m_s, y0_s, gc_s, bon_s, *, n_pairs, n_chunks):
    C = WKV_CHUNK
    R = 2 * C
    t = pl.program_id(2)

    @pl.when(t == 0)
    def _():
        s_scr[...] = s0_ref[0]

    ri = lax.broadcasted_iota(jnp.int32, (R, R), 0)
    ci = lax.broadcasted_iota(jnp.int32, (R, R), 1)
    sh = lambda x, sz: jnp.right_shift(x, sz.bit_length() - 1)
    same_head = sh(ri, C) == sh(ci, C)
    strict = same_head & (ci < ri)
    incl = same_head & (ci <= ri)
    eye = ri == ci

    def bd(sz):
        return sh(ri, sz) == sh(ci, sz)

    lane_lo = lax.broadcasted_iota(jnp.int32, (C, LANES), 1) < RW_HEAD_DIM

    def stack(x):
        z = jnp.zeros_like(x)
        return jnp.concatenate([jnp.where(lane_lo, x, z), jnp.where(lane_lo, z, x)], axis=0)

    b16 = lambda x: x.astype(BF16)
    f32 = lambda x: x.astype(F32)
    ident = jnp.where(eye, 1.0, 0.0)

    units = [(p, c) for p in range(n_pairs) for c in range(n_chunks)]
    idx = range(len(units))
    rs, asb, bsb, ksb, vsb, bdk, kdk, wcs = [], [], [], [], [], [], [], []
    for u, (p, c) in enumerate(units):
        rows, lanes = slice(c * C, (c + 1) * C), slice(p * LANES, (p + 1) * LANES)
        r_, a_, b_, k_, v_ = (stack(ref[0, rows, lanes]) for ref in (rt_ref, at_ref, bt_ref, kt_ref, v_ref))
        wc = wc_ref[0, c, :, lanes]
        rk = rk_ref[:, lanes]
        bon_s[u] = jnp.sum(f32(r_) * f32(k_) * rk, axis=1, keepdims=True) * f32(v_)
        rs.append(r_); asb.append(a_); bsb.append(b_); ksb.append(k_); vsb.append(v_)
        bdk.append(b16(f32(b_) * wc)); kdk.append(b16(f32(k_) * wc)); wcs.append(wc)

    n, n8, aak, arb, ark = [], [], [], [], []
    for u in idx:
        ab = _dot_nt(jnp.concatenate([asb[u], rs[u]], axis=0), jnp.concatenate([bsb[u], ksb[u]], axis=0))
        nc = jnp.where(strict, ab[:R, :R], 0.0)
        n.append(b16(nc)); n8.append(jnp.where(bd(8), nc, 0.0))
        aak.append(b16(jnp.where(strict, ab[:R, R:], 0.0)))
        arb.append(b16(jnp.where(incl, ab[R:, :R], 0.0)))
        ark.append(b16(jnp.where(incl, ab[R:, R:], 0.0)))

    n8b = [b16(x) for x in n8]
    n2 = [_dot(n8b[u], n8b[u]) for u in idx]
    n2b = [b16(x) for x in n2]
    n4 = [_dot(n2b[u], n2b[u]) for u in idx]
    p1 = [_dot(b16(ident + n8[u]), b16(ident + n2[u])) for u in idx]
    tinv = [_dot(b16(p1[u]), b16(ident + n4[u])) for u in idx]
    zero16 = jnp.zeros((R, R), BF16)
    for sz in (8, 16, 32):
        off = bd(2 * sz) & jnp.logical_not(bd(sz))
        tb = [b16(x) for x in tinv]
        x = [_dot(tb[u], jnp.where(off, n[u], zero16)) for u in idx]
        tinv = [tinv[u] + _dot(b16(x[u]), tb[u]) for u in idx]
    tb = [b16(x) for x in tinv]

    u0p = [_dot(aak[u], vsb[u]) for u in idx]
    tx = [_dot(tb[u], jnp.concatenate([asb[u], b16(u0p[u])], axis=1)) for u in idx]
    txb = [b16(x) for x in tx]
    az = [_dot(arb[u], txb[u]) for u in idx]
    for u in idx:
        rp_s[u] = b16(f32(rs[u]) + az[u][:, :LANES])
        y0_s[u] = az[u][:, LANES:] + _dot(ark[u], vsb[u])
    for u in idx:
        txt = b16(tx[u].T)
        mg = _dot(txt, bdk[u])
        m_s[u] = b16(jnp.where(eye, wcs[u], 0.0) + mg[:LANES])
        gc_s[u] = mg[LANES:] + _dot(b16(f32(vsb[u]).T), kdk[u])

    s = [s_scr[p] for p in range(n_pairs)]
    for c in range(n_chunks):
        for p in range(n_pairs):
            u = p * n_chunks + c
            rows, lanes = slice(c * C, (c + 1) * C), slice(p * LANES, (p + 1) * LANES)
            sb = b16(s[p])
            y = _dot_nt(rp_s[u], sb) + y0_s[u]
            s[p] = _dot(sb, m_s[u]) + gc_s[u]
            mu = jnp.sum(y, axis=1, keepdims=True) * (1.0 / RW_HEAD_DIM)
            d = jnp.where(same_head, y - mu, 0.0)
            var = jnp.sum(d * d, axis=1, keepdims=True) * (1.0 / RW_HEAD_DIM)
            ost = (d * lax.rsqrt(var + GN_EPS) * lnw_ref[:, lanes]
                   + jnp.where(same_head, lnb_ref[:, lanes], 0.0) + bon_s[u])
            o_ref[0, rows, lanes] = ((ost[:C] + ost[C:]) * f32(g_ref[0, rows, lanes])).astype(BF16)
    for p in range(n_pairs):
        s_scr[p] = s[p]
        sout_ref[0, p] = s[p]


def _wkv(rt, at, bt, kt, v, g, wc, s0bd, rk, lnw, lnb, *, tb, n_pairs):
    bx, t, _ = rt.shape
    n_chunks = tb // WKV_CHUNK
    n_units = n_pairs * n_chunks
    w = n_pairs * LANES
    seq = pl.BlockSpec((1, tb, w), lambda b, p, i: (b, i, p))
    vec = pl.BlockSpec((1, w), lambda b, p, i: (0, p))
    st = pl.BlockSpec((1, n_pairs, LANES, LANES), lambda b, p, i: (b, p, 0, 0))
    mat = lambda dt: pltpu.VMEM((n_units, LANES, LANES), dt)
    return pl.pallas_call(
        functools.partial(_wkv_kernel, n_pairs=n_pairs, n_chunks=n_chunks),
        grid=(bx, N_PAIRS // n_pairs, t // tb),
        in_specs=[seq] * 6 + [pl.BlockSpec((1, n_chunks, 1, w), lambda b, p, i: (b, i, 0, p)),
                              st, vec, vec, vec],
        out_specs=[seq, st],
        out_shape=[jax.ShapeDtypeStruct((bx, t, RW_WIDTH), BF16),
                   jax.ShapeDtypeStruct((bx, N_PAIRS, LANES, LANES), F32)],
        scratch_shapes=[pltpu.VMEM((n_pairs, LANES, LANES), F32),
                        mat(BF16), mat(BF16), mat(F32), mat(F32), mat(F32)],
        compiler_params=_cparams(3),
        name="wkv",
    )(rt, at, bt, kt, v, g, wc, s0bd, rk, lnw, lnb)


def _merge_kernel(x_ref, oda_ref, orw_ref, gt_ref, wda_ref, wrw_ref, wout_ref, o_ref):
    a = _dot(oda_ref[0], wda_ref[...])
    b = _dot(orw_ref[0], wrw_ref[...])
    gt = gt_ref[0].astype(F32)
    m = _sigmoid(gt[:, :D_MODEL]) * a + _sigmoid(gt[:, D_MODEL:]) * b
    o_ref[0] = x_ref[0] + _dot(m.astype(BF16), wout_ref[...])


def _merge(x, oda, orw, gt, wda, wrw, wout, *, tm):
    bx, t, _ = x.shape
    row = lambda w: pl.BlockSpec((1, tm, w), lambda b, i: (b, i, 0))
    sq = _resident((D_MODEL, D_MODEL))
    return pl.pallas_call(
        _merge_kernel,
        grid=(bx, t // tm),
        in_specs=[row(D_MODEL), row(DA_WIDTH), row(RW_WIDTH), row(P_GATE), sq, sq, sq],
        out_specs=row(D_MODEL),
        out_shape=jax.ShapeDtypeStruct((bx, t, D_MODEL), F32),
        compiler_params=_cparams(2),
        name="merge",
    )(x, oda, orw, gt, wda, wrw, wout)


_FF_CW = 256
_HALO = SUBLANES


def _ffn_kernel(x_ref, halo_ref, c0_ref, g_ref, wup_ref, f_ref, fb_ref, wdn_ref, gf_ref,
                o_ref, conv_ref, ext_s, act_s, *, tm, final):
    first = pl.program_id(1) == 0
    xe = jnp.concatenate([halo_ref[0], x_ref[0]], axis=0)
    hb = _rms(xe, g_ref[...]).astype(BF16)
    starts = list(range(0, D_FF, _FF_CW))

    def up_proj(c0):
        for base in (c0, D_FF + c0):
            cols = slice(base, base + _FF_CW)
            up = _dot(hb, wup_ref[:, cols])
            ext_s[:, cols] = up
            ext_s[_HALO - 2:_HALO, cols] = jnp.where(first, c0_ref[0, :, cols], up[_HALO - 2:_HALO])
            conv_ref[0, :, cols] = up[_HALO + tm - 2:_HALO + tm]

    def conv_act(c0):
        cs = []
        for base in (c0, D_FF + c0):
            cols = slice(base, base + _FF_CW)
            f = f_ref[:, cols]
            cs.append(fb_ref[:, cols]
                      + ext_s[_HALO - 2:_HALO - 2 + tm, cols] * f[0:1]
                      + ext_s[_HALO - 1:_HALO - 1 + tm, cols] * f[1:2]
                      + ext_s[_HALO:_HALO + tm, cols] * f[2:3])
        ca, cb = cs
        act_s[:, c0:c0 + _FF_CW] = (ca * _sigmoid(ca) * cb).astype(BF16)

    up_proj(starts[0])
    for j, c0 in enumerate(starts):
        if j + 1 < len(starts):
            up_proj(starts[j + 1])
        conv_act(c0)
    out = x_ref[0] + _dot(act_s[...], wdn_ref[...])
    if final:
        out = _rms(out, gf_ref[...])
    o_ref[0] = out


def _ffn(x, conv0, g, wup, f, fb, wdn, gfinal, *, tm, final):
    bx, t, _ = x.shape
    row = pl.BlockSpec((1, tm, D_MODEL), lambda b, i: (b, i, 0))
    halo = pl.BlockSpec((1, _HALO, D_MODEL),
                        lambda b, i: (b, jnp.maximum(i * (tm // _HALO) - 1, 0), 0))
    cst = pl.BlockSpec((1, CONV_W - 1, 2 * D_FF), lambda b, i: (b, 0, 0))
    return pl.pallas_call(
        functools.partial(_ffn_kernel, tm=tm, final=final),
        grid=(bx, t // tm),
        in_specs=[row, halo, cst, _resident((1, D_MODEL)), _resident((D_MODEL, 2 * D_FF)),
                  _resident((CONV_W, 2 * D_FF)), _resident((1, 2 * D_FF)), _resident((D_FF, D_MODEL)),
                  _resident((1, D_MODEL))],
        out_specs=[row, cst],
        out_shape=[jax.ShapeDtypeStruct((bx, t, D_MODEL), F32),
                   jax.ShapeDtypeStruct((bx, CONV_W - 1, 2 * D_FF), F32)],
        scratch_shapes=[pltpu.VMEM((_HALO + tm, 2 * D_FF), F32), pltpu.VMEM((tm, D_FF), BF16)],
        compiler_params=_cparams(2),
        name="ffn",
    )(x, x, conv0, g, wup, f, fb, wdn, gfinal)


def _rope_tables(pos):
    half = ROT_DIM // 2
    inv = ROPE_THETA ** (-jnp.arange(0, ROT_DIM, 2, dtype=F32) / ROT_DIM)
    ang = pos.astype(F32)[:, None] * inv[None, :]
    cos, sin = jnp.cos(ang), jnp.sin(ang)
    t = pos.shape[0]
    pad = jnp.zeros((t, DA_HEAD_DIM - ROT_DIM), F32)
    z = jnp.zeros((t, half), F32)
    one_map = lambda a, b, fill: jnp.concatenate([a, b, pad + fill], axis=1)
    rc = one_map(cos, cos, 1.0)
    ra = one_map(z, sin, 0.0)
    rb = one_map(-sin, z, 0.0)
    dup = lambda m: jnp.concatenate([m, m], axis=1)
    return dup(rc), dup(ra), dup(rb)


def _state_to_blockdiag(s):
    b = s.shape[0]
    s = s.reshape(b, N_PAIRS, 2, RW_HEAD_DIM, RW_HEAD_DIM)
    z = jnp.zeros_like(s[:, :, 0])
    top = jnp.concatenate([s[:, :, 0], z], axis=-1)
    bot = jnp.concatenate([z, s[:, :, 1]], axis=-1)
    return jnp.concatenate([top, bot], axis=-2)


def _blockdiag_to_state(sbd):
    b = sbd.shape[0]
    h0 = sbd[:, :, :RW_HEAD_DIM, :RW_HEAD_DIM]
    h1 = sbd[:, :, RW_HEAD_DIM:, RW_HEAD_DIM:]
    return jnp.stack([h0, h1], axis=2).reshape(b, RW_HEADS, RW_HEAD_DIM, RW_HEAD_DIM)


def _layer(x, lidx, depth, tabs, kvbuf, cache, wkv0, shift0, conv0, w, norm_final, *, final):
    bx, t, _ = x.shape
    prompt = cache is None
    lam_init = 0.8 - 0.6 * math.exp(-0.3 * lidx)
    prep = lambda tri: (w["rw_mu"], w["rw_w0"], w["rw_w2p"], w["rw_a0"], w["rw_a2p"], w["rw_g2"],
                        w["rw_k_k"], w["rw_k_a"], w["seg"], tri)

    if prompt:
        tm = ROW_TILE
        q, k, v, kout, vout, gate, rt, at, bt, kt, vr, g, wc, new_shift = _in_proj(
            x, w["norm_mix"], w["w_in"], tabs, kvbuf, layer=lidx, depth=depth, tm=tm, fused=True,
            shift0=shift0, prep=prep(w["tri_p"]))
        o_da = _attn_prompt(w["da_lambda"], w["da_subln"], q, k, v, lam_init=lam_init)
    else:
        tm = t
        flat = lambda a: a.reshape(1, bx * t, a.shape[-1])
        q, k, v, kout, vout, gate, u_rw = _in_proj(flat(x), w["norm_mix"], w["w_in"], tabs, kvbuf,
                                                   layer=lidx, depth=depth, tm=bx * t, fused=False)
        u_rw = u_rw.reshape(bx, t, P_RW)
        new_shift = u_rw[:, -1:]
        pad_rows = lambda a, n: jnp.pad(a.reshape(bx, t, a.shape[-1]), ((0, 0), (0, n - t), (0, 0)))
        o_da = _attn_sample(w["da_lambda"], w["da_subln"], pad_rows(q, LANES), pad_rows(k, LANES),
                            pad_rows(v, LANES), *cache, layer=lidx, lam_init=lam_init, n_valid_new=t)[:, :t]
        rt, at, bt, kt, vr, g, wc = _rw_prep(u_rw, shift0, prep(w["tri_s"]), tm=t)
        rt, at, bt, kt, vr, g = [pad_rows(a, WKV_CHUNK) for a in (rt, at, bt, kt, vr, g)]
    o_rw, sbd = _wkv(rt, at, bt, kt, vr, g, wc, _state_to_blockdiag(wkv0),
                     w["rw_r_k"], w["rw_ln_w"], w["rw_ln_b"],
                     tb=WKV_ROWS if prompt else WKV_CHUNK, n_pairs=WKV_PAIRS if prompt else N_PAIRS)
    o_rw = o_rw[:, :t]

    if prompt:
        x = _merge(x, o_da, o_rw, gate, w["w_o_da"], w["w_o_rw"], w["w_out"], tm=tm)
    else:
        x = _merge(flat(x), flat(o_da), flat(o_rw), gate, w["w_o_da"], w["w_o_rw"], w["w_out"],
                   tm=bx * t).reshape(bx, t, D_MODEL)
    x, new_conv = _ffn(x, conv0, w["norm_ffn"], w["w_up"], w["ffn_conv"], w["ffn_conv_b"], w["w_down"],
                       norm_final, tm=tm, final=final)
    return x, (kout, vout), _blockdiag_to_state(sbd), new_shift, new_conv


def kernel(x_prompt, x_sample, cache_k, cache_v, state_wkv, state_shift, state_ffn_conv, norm_mix, w_in, da_lambda, da_subln, w_o_da, rw_mu, rw_w0, rw_w2, rw_a0, rw_a2, rw_g2, rw_k_k, rw_k_a, rw_r_k, rw_ln_w, rw_ln_b, w_o_rw, w_out, norm_ffn, w_up, ffn_conv, ffn_conv_b, w_down, norm_final):
    bp, tp, _ = x_prompt.shape
    bs, ts, _ = x_sample.shape
    depth = w_in.shape[0]
    past = cache_k.shape[2]
    tabs_p = _rope_tables(jnp.arange(tp, dtype=jnp.int32))
    tabs_s = tuple(jnp.tile(a, (bs, 1)) for a in _rope_tables(past + jnp.arange(ts, dtype=jnp.int32)))
    cache = (cache_k.reshape(depth, bs, past, DA_WIDTH), cache_v.reshape(depth, bs, past, DA_WIDTH))

    lane = jnp.arange(2 * LANES)
    seg = (lane[:, None] // RW_HEAD_DIM == lane[None, :] // RW_HEAD_DIM).astype(BF16)
    tri = lambda n: jnp.tile((jnp.arange(n)[:, None] >= jnp.arange(n)[None, :]).astype(BF16), (1, 3))
    tri_p, tri_s = tri(WKV_CHUNK), tri(ts)
    zl = jnp.zeros((DECAY_LORA, RW_WIDTH), F32)
    row = lambda a: a.reshape(1, -1)
    zero = lambda *s: jnp.zeros(s, x_prompt.dtype)
    wkv_z, shift_z, conv_z = zero(bp, RW_HEADS, RW_HEAD_DIM, RW_HEAD_DIM), zero(bp, 1, P_RW), zero(bp, CONV_W - 1, 2 * D_FF)

    xp, xs = x_prompt, x_sample
    kv_p = kv_s = None
    outs_p, outs_s = [], []
    nf = row(norm_final)
    for l in range(depth):
        w = dict(
            norm_mix=row(norm_mix[l]), w_in=w_in[l].astype(BF16), da_lambda=da_lambda[l],
            da_subln=da_subln[l].reshape(-1, 1), w_o_da=w_o_da[l].astype(BF16),
            rw_mu=row(rw_mu[l]), rw_w0=row(rw_w0[l]),
            rw_w2p=jnp.concatenate([rw_w2[l], zl], axis=0).astype(BF16),
            rw_a0=row(rw_a0[l]), rw_a2p=jnp.concatenate([zl, rw_a2[l]], axis=0).astype(BF16),
            rw_g2=rw_g2[l].astype(BF16), rw_k_k=row(rw_k_k[l]), rw_k_a=row(rw_k_a[l]),
            rw_r_k=row(rw_r_k[l]), rw_ln_w=row(rw_ln_w[l]), rw_ln_b=row(rw_ln_b[l]),
            w_o_rw=w_o_rw[l].astype(BF16), w_out=w_out[l].astype(BF16), norm_ffn=row(norm_ffn[l]),
            w_up=w_up[l].astype(BF16), ffn_conv=ffn_conv[l], ffn_conv_b=row(ffn_conv_b[l]),
            w_down=w_down[l].astype(BF16), seg=seg, tri_p=tri_p, tri_s=tri_s)
        final = l == depth - 1
        xp, kv_p, sp, shp, cp = _layer(xp, l, depth, tabs_p, kv_p, None, wkv_z, shift_z, conv_z, w, nf, final=final)
        xs, kv_s, sq, shq, cq = _layer(xs, l, depth, tabs_s, kv_s, cache, state_wkv[l], state_shift[l],
                                       state_ffn_conv[l], w, nf, final=final)
        outs_p.append((sp, shp, cp))
        outs_s.append((sq, shq, cq))

    wkv_prompt, shift_prompt, conv_prompt = [jnp.stack(t) for t in zip(*outs_p)]
    wkv_sample, shift_sample, conv_sample = [jnp.stack(t) for t in zip(*outs_s)]
    heads = lambda a, b, t: a.reshape(depth, b, t, DA_HEADS, DA_V_DIM)
    return (xp, xs, heads(kv_p[0], bp, tp), heads(kv_p[1], bp, tp), wkv_prompt, shift_prompt, conv_prompt,
            heads(kv_s[0], bs, ts), heads(kv_s[1], bs, ts), wkv_sample, shift_sample, conv_sample)
```

```python
import functools
import math

import jax
import jax.numpy as jnp
from jax import lax
from jax.experimental import pallas as pl
from jax.experimental.pallas import tpu as pltpu

F32 = jnp.float32
BF16 = jnp.bfloat16

D_MODEL = 1024
CHUNK = 64
EPS = 1e-6
DA_HEADS = 8
DA_HEAD_DIM = 64
DA_V_DIM = 128
DA_WIDTH = DA_HEADS * DA_V_DIM
ROPE_THETA = 500000.0
ROT_DIM = DA_HEAD_DIM // 4
RW_HEAD_DIM = 64
RW_HEADS = 16
RW_WIDTH = 1024
DECAY_LORA = 64
AAA_LORA = 64
GATE_LORA = 128
GN_EPS = 64e-5
D_FF = 2816
CONV_W = 3
P_DA = 3 * DA_WIDTH
P_RW = 3 * RW_WIDTH + DECAY_LORA + AAA_LORA + GATE_LORA
P_GATE = 2 * D_MODEL
P_TOTAL = P_DA + P_RW + P_GATE

LANES = 128
SUBLANES = 8
VMEM_LIMIT = 56 * 1024 * 1024

WKV_CHUNK = 64
N_PAIRS = RW_HEADS // 2
NEG_BIG = -1e30
ROW_TILE = 256
WKV_ROWS = 512
WKV_PAIRS = 4
CACHE_PARTS = 2

_ARB = "arbitrary"


def _cparams(n_axes):
    return pltpu.CompilerParams(dimension_semantics=(_ARB,) * n_axes,
                                vmem_limit_bytes=VMEM_LIMIT)


def _resident(shape):
    nd = len(shape)
    return pl.BlockSpec(shape, lambda *_: (0,) * nd, pipeline_mode=pl.Buffered(1))


def _dot(a, b):
    return jnp.dot(a, b, preferred_element_type=F32)


def _dot_nt(a, b):
    return lax.dot_general(a, b, (((1,), (1,)), ((), ())), preferred_element_type=F32)


def _rms(x, g):
    return x * lax.rsqrt(jnp.mean(x * x, axis=-1, keepdims=True) + EPS) * g


def _sigmoid(x):
    return 1.0 / (1.0 + jnp.exp(-x))


def _split3(x):
    hi = x.astype(BF16)
    r1 = x - hi.astype(F32)
    mid = r1.astype(BF16)
    lo = (r1 - mid.astype(F32)).astype(BF16)
    return hi, mid, lo


def _prep_rows(us, w0_ref, w2_ref, a0_ref, a2_ref, g2_ref, kk_ref, ka_ref, seg_ref):
    r = us[:, 0:RW_WIDTH]
    kr = us[:, RW_WIDTH:2 * RW_WIDTH]
    vr = us[:, 2 * RW_WIDTH:3 * RW_WIDTH]
    wa = us[:, 3 * RW_WIDTH:3 * RW_WIDTH + DECAY_LORA + AAA_LORA]
    gd = us[:, 3 * RW_WIDTH + DECAY_LORA + AAA_LORA:]

    z = w0_ref[...] + _dot(jnp.tanh(wa).astype(BF16), w2_ref[...])
    nz = -z
    w_log = -(jnp.maximum(nz, 0.0) + jnp.log1p(jnp.exp(-jnp.abs(nz)))) - 0.5
    lw = -jnp.exp(w_log)
    a = _sigmoid(a0_ref[...] + _dot(wa.astype(BF16), a2_ref[...]))
    g = _dot(_sigmoid(gd).astype(BF16), g2_ref[...])

    kk = kr * kk_ref[...]
    k2 = (kk * kk).astype(BF16)
    seg = seg_ref[...]
    w = seg.shape[0]
    ssum = jnp.concatenate([_dot(k2[:, c:c + w], seg) for c in range(0, RW_WIDTH, w)],
                           axis=1)
    kk = kk * lax.rsqrt(jnp.maximum(ssum, 1e-24))
    kmod = kr * (1.0 + (a - 1.0) * ka_ref[...])
    return r, vr, g, lw, a, kk, kmod


def _prep_chunk(r, lw, a, kk, kmod, tri):
    cum = _dot(tri, jnp.concatenate(_split3(lw), axis=0))
    e_pos = jnp.exp(cum)
    e_neg = jnp.exp(-cum)
    rt = r * e_pos
    at = -kk * jnp.exp(cum - lw)
    bt = kk * a * e_neg
    kt = kmod * e_neg
    return rt, at, bt, kt, e_pos[-1:, :]


def _rw_prep_kernel(u_ref, sh_ref, mu_ref, w0_ref, w2_ref, a0_ref, a2_ref, g2_ref,
                    kk_ref, ka_ref, seg_ref, tri_ref,
                    rt_ref, at_ref, bt_ref, kt_ref, v_ref, g_ref, wc_ref, *, tm):
    u = u_ref[0]
    rows = lax.broadcasted_iota(jnp.int32, (tm, 1), 0)
    up = jnp.where(rows == 0, sh_ref[0], pltpu.roll(u, 1, 0))
    us = u + (up - u) * mu_ref[...]
    r, vr, g, lw, a, kk, kmod = _prep_rows(us, w0_ref, w2_ref, a0_ref, a2_ref, g2_ref, kk_ref, ka_ref, seg_ref)
    rt, at, bt, kt, wc = _prep_chunk(r, lw, a, kk, kmod, tri_ref[...])
    for ref, val in ((rt_ref, rt), (at_ref, at), (bt_ref, bt), (kt_ref, kt), (v_ref, vr), (g_ref, g)):
        ref[0] = val.astype(BF16)
    wc_ref[0, 0] = wc


def _prep_param_specs(chunk):
    return [_resident((1, P_RW)), _resident((1, RW_WIDTH)),
            _resident((DECAY_LORA + AAA_LORA, RW_WIDTH)), _resident((1, RW_WIDTH)),
            _resident((DECAY_LORA + AAA_LORA, RW_WIDTH)), _resident((GATE_LORA, RW_WIDTH)),
            _resident((1, RW_WIDTH)), _resident((1, RW_WIDTH)),
            _resident((2 * LANES, 2 * LANES)), _resident((chunk, 3 * chunk))]


def _rw_prep(u, shift0, prep, *, tm):
    bx, t, _ = u.shape
    assert t == tm
    row = lambda w: pl.BlockSpec((1, tm, w), lambda b: (b, 0, 0))
    outs = [jax.ShapeDtypeStruct((bx, t, RW_WIDTH), BF16)] * 6
    outs.append(jax.ShapeDtypeStruct((bx, 1, 1, RW_WIDTH), F32))
    return pl.pallas_call(
        functools.partial(_rw_prep_kernel, tm=tm),
        grid=(bx,),
        in_specs=[row(P_RW), pl.BlockSpec((1, 1, P_RW), lambda b: (b, 0, 0))] + _prep_param_specs(tm),
        out_specs=[row(RW_WIDTH)] * 6 + [pl.BlockSpec((1, 1, 1, RW_WIDTH), lambda b: (b, 0, 0, 0))],
        out_shape=outs,
        compiler_params=_cparams(1),
        name="rw_prep",
    )(u, shift0, *prep)


_IN_CW = 256
_LOG2E = math.log2(math.e)
_N_PREP = 10


def _in_proj_kernel(*refs, tm, fused, aliased):
    x_ref, g_ref, w_ref, rc_ref, ra_ref, rb_ref = refs[:6]
    n_in = 6
    if fused:
        sh_ref = refs[6]
        mu_ref, w0_ref, w2_ref, a0_ref, a2_ref, g2_ref, kk_ref, ka_ref, seg_ref, tri_ref = refs[7:7 + _N_PREP]
        n_in = 7 + _N_PREP
    n_in += 2 if aliased else 0
    q_ref, kn_ref, vn_ref, ko_ref, vo_ref, gt_ref = refs[n_in:n_in + 6]
    if fused:
        rt_ref, at_ref, bt_ref, kt_ref, vr_ref, gr_ref, wc_ref, last_ref = refs[n_in + 6:n_in + 14]
        k_scr, v_scr, u_s = refs[n_in + 14:]
    else:
        u_ref = refs[n_in + 6]
        k_scr, v_scr = refs[n_in + 7:]
    hb = _rms(x_ref[0], g_ref[...]).astype(BF16)
    rc = rc_ref[...]
    ra = ra_ref[...]
    rb = rb_ref[...]

    def rope(t):
        return t * rc + pltpu.roll(t, ROT_DIM // 2, 1) * ra + pltpu.roll(t, LANES - ROT_DIM // 2, 1) * rb

    def chunk(c0):
        if P_DA <= c0 < P_DA + P_RW:
            o = c0 - P_DA
            if fused:
                u_s[SUBLANES:, o:o + _IN_CW] = _dot(hb, w_ref[:, c0:c0 + _IN_CW])
            else:
                u_ref[0, :, o:o + _IN_CW] = _dot(hb, w_ref[:, c0:c0 + _IN_CW])
            return
        acc = _dot(hb, w_ref[:, c0:c0 + _IN_CW])
        if c0 < DA_WIDTH:
            for s in range(0, _IN_CW, LANES):
                q_ref[0, :, c0 + s:c0 + s + LANES] = (
                    rope(acc[:, s:s + LANES]) * (DA_HEAD_DIM ** -0.5 * _LOG2E)).astype(BF16)
        elif c0 < 2 * DA_WIDTH:
            o = c0 - DA_WIDTH
            for s in range(0, _IN_CW, LANES):
                kr = rope(acc[:, s:s + LANES])
                k_scr[:, o + s:o + s + LANES] = kr
                kn_ref[0, :, o + s:o + s + LANES] = kr.astype(BF16)
        elif c0 < P_DA:
            o = c0 - 2 * DA_WIDTH
            v_scr[:, o:o + _IN_CW] = acc
            vn_ref[0, :, o:o + _IN_CW] = acc.astype(BF16)
        else:
            o = c0 - P_DA - P_RW
            gt_ref[0, :, o:o + _IN_CW] = acc.astype(BF16)

    def chunks(lo, hi):
        for c0 in range(lo, hi, _IN_CW):
            chunk(c0)

    if not fused:
        chunks(0, P_TOTAL)
    else:
        @pl.when(pl.program_id(1) == 0)
        def _():
            u_s[SUBLANES - 1:SUBLANES, :] = sh_ref[0]

        chunks(0, P_DA + P_RW)
        u = u_s[SUBLANES:SUBLANES + tm, :]
        up = u_s[SUBLANES - 1:SUBLANES - 1 + tm, :]
        us = u + (up - u) * mu_ref[...]
        last = u[tm - 1:tm, :]
        last_ref[0] = last
        u_s[SUBLANES - 1:SUBLANES, :] = last
        r, vr, g, lw, a, kk, kmod = _prep_rows(us, w0_ref, w2_ref, a0_ref, a2_ref, g2_ref, kk_ref, ka_ref, seg_ref)
        vr_ref[0] = vr.astype(BF16)
        gr_ref[0] = g.astype(BF16)
        chunks(P_DA + P_RW, P_DA + P_RW + P_GATE // 2)
        tri = tri_ref[...]
        for c in range(tm // WKV_CHUNK):
            sl = slice(c * WKV_CHUNK, (c + 1) * WKV_CHUNK)
            rt, at, bt, kt, wc = _prep_chunk(r[sl], lw[sl], a[sl], kk[sl], kmod[sl], tri)
            for ref, val in ((rt_ref, rt), (at_ref, at), (bt_ref, bt), (kt_ref, kt)):
                ref[0, sl, :] = val.astype(BF16)
            wc_ref[0, c] = wc
        chunks(P_DA + P_RW + P_GATE // 2, P_TOTAL)
    ko_ref[0, 0] = k_scr[...].reshape(tm, DA_HEADS, DA_V_DIM)
    vo_ref[0, 0] = v_scr[...].reshape(tm, DA_HEADS, DA_V_DIM)


def _in_proj(x, g, w_bf, tabs, kvout, *, layer, depth, tm, fused, shift0=None, prep=None):
    bx, t, _ = x.shape
    row = lambda w: pl.BlockSpec((1, tm, w), lambda b, i: (b, i, 0))
    tab = pl.BlockSpec((tm, LANES), lambda b, i: (i, 0))
    slot = pl.BlockSpec((1, 1, tm, DA_HEADS, DA_V_DIM), lambda b, i: (layer, b, i, 0, 0))
    aliased = kvout is not None
    in_specs = [row(D_MODEL), _resident((1, D_MODEL)), _resident((D_MODEL, P_TOTAL)), tab, tab, tab]
    args = [x, g, w_bf, *tabs]
    if fused:
        in_specs += [pl.BlockSpec((1, 1, P_RW), lambda b, i: (b, 0, 0))] + _prep_param_specs(WKV_CHUNK)
        args += [shift0, *prep]
    if aliased:
        aliases = {len(args): 3, len(args) + 1: 4}
        in_specs += [pl.BlockSpec(memory_space=pl.ANY)] * 2
        args += list(kvout)
    else:
        aliases = {}
    nat = jax.ShapeDtypeStruct((bx, t, DA_WIDTH), BF16)
    stacked = jax.ShapeDtypeStruct((depth, bx, t, DA_HEADS, DA_V_DIM), F32)
    out_specs = [row(DA_WIDTH)] * 3 + [slot, slot, row(P_GATE)]
    out_shape = [nat, nat, nat, stacked, stacked, jax.ShapeDtypeStruct((bx, t, P_GATE), BF16)]
    scratch = [pltpu.VMEM((tm, DA_WIDTH), F32)] * 2
    if fused:
        nc = tm // WKV_CHUNK
        out_specs += [row(RW_WIDTH)] * 6 + [pl.BlockSpec((1, nc, 1, RW_WIDTH), lambda b, i: (b, i, 0, 0)),
                                            pl.BlockSpec((1, 1, P_RW), lambda b, i: (b, 0, 0))]
        out_shape += [jax.ShapeDtypeStruct((bx, t, RW_WIDTH), BF16)] * 6
        out_shape += [jax.ShapeDtypeStruct((bx, t // WKV_CHUNK, 1, RW_WIDTH), F32),
                      jax.ShapeDtypeStruct((bx, 1, P_RW), F32)]
        scratch.append(pltpu.VMEM((SUBLANES + tm, P_RW), F32))
    else:
        out_specs.append(row(P_RW))
        out_shape.append(jax.ShapeDtypeStruct((bx, t, P_RW), F32))
    return pl.pallas_call(
        functools.partial(_in_proj_kernel, tm=tm, fused=fused, aliased=aliased),
        grid=(bx, t // tm),
        in_specs=in_specs,
        out_specs=out_specs,
        out_shape=out_shape,
        scratch_shapes=scratch,
        input_output_aliases=aliases,
        compiler_params=_cparams(2),
        name="in_proj",
    )(*args)


_TK = 256
_CHUNK_SHIFT = CHUNK.bit_length() - 1


def _lambda(lam_ref, lam_init):
    lp = lam_ref[...]
    return (jnp.exp(jnp.sum(lp[0:1] * lp[1:2], axis=1, keepdims=True))
            - jnp.exp(jnp.sum(lp[2:3] * lp[3:4], axis=1, keepdims=True)) + lam_init)


def _q_blockdiag(q_tile):
    tq = q_tile.shape[0]
    qt = q_tile.astype(F32).T.astype(BF16)
    z = jnp.zeros((DA_HEAD_DIM, tq), BF16)
    return jnp.concatenate([jnp.concatenate([qt[:DA_HEAD_DIM], z], axis=1),
                            jnp.concatenate([z, qt[DA_HEAD_DIM:]], axis=1)], axis=0)


def _softmax_step(s, stats, mask, tq):
    new_stats, alphas, ps = [], [], []
    for c in range(2):
        m, l = stats[2 * c:2 * c + 2]
        sc = s[:, c * tq:(c + 1) * tq]
        if mask is not None:
            sc = jnp.where(mask, sc, NEG_BIG)
        m_new = jnp.maximum(m, jnp.max(sc, axis=0, keepdims=True))
        alpha = jnp.exp2(m - m_new)
        p = jnp.exp2(sc - m_new)
        new_stats += [m_new, alpha * l + jnp.sum(p, axis=0, keepdims=True)]
        alphas.append(alpha)
        ps.append(p.astype(BF16))
    return tuple(new_stats), tuple(alphas), tuple(ps)


def _attn_finish(a1, a2, l1, l2, lam, sub_col, lam_init):
    ot = a1 * (1.0 / l1) - lam * (a2 * (1.0 / l2))
    ss = jnp.sum(ot * ot, axis=0, keepdims=True) * (1.0 / DA_V_DIM)
    ot = ot * lax.rsqrt(ss + EPS) * sub_col * (1.0 - lam_init)
    return ot.T.astype(BF16)


def _attn_prompt_kernel(lam_ref, sub_ref, q_ref, k_ref, v_ref, o_ref, vt_s, s_s, p_s, acc_s,
                        *, n_tiles, lam_init):
    tq = _TK
    for jb in range(n_tiles):
        vt_s[jb] = v_ref[0, jb * _TK:(jb + 1) * _TK, :].astype(F32).T.astype(BF16)
    lam = _lambda(lam_ref, lam_init)
    sub_col = sub_ref[...]
    kr = jnp.right_shift(lax.broadcasted_iota(jnp.int32, (_TK, tq), 0), _CHUNK_SHIFT)
    qc = jnp.right_shift(lax.broadcasted_iota(jnp.int32, (_TK, tq), 1), _CHUNK_SHIFT)
    diag_mask = kr <= qc
    init_stats = (jnp.full((1, tq), NEG_BIG, F32), jnp.zeros((1, tq), F32)) * 2

    steps = [(i, j) for i in range(n_tiles) for j in range(i + 1)]
    qbd = {}
    alphas_of = {}
    final_stats = {}

    def scores(n):
        i, j = steps[n]
        if j == 0:
            qbd[i] = _q_blockdiag(q_ref[0, i * tq:(i + 1) * tq, :])
        s_s[n % 2] = _dot(k_ref[0, j * _TK:(j + 1) * _TK, :], qbd[i])

    def values(n):
        i, j = steps[n]
        for c in range(2):
            pv = _dot(vt_s[j], p_s[n % 2, c])
            acc_s[c] = pv if j == 0 else alphas_of[n][c] * acc_s[c] + pv
        if j == i:
            _, l1, _, l2 = final_stats[i]
            o_ref[0, i * tq:(i + 1) * tq, :] = _attn_finish(acc_s[0], acc_s[1], l1, l2, lam, sub_col, lam_init)

    scores(0)
    stats = init_stats
    for n, (i, j) in enumerate(steps):
        if n + 1 < len(steps):
            scores(n + 1)
        if n >= 1:
            values(n - 1)
        if j == 0:
            stats = init_stats
        stats, alphas_of[n], ps = _softmax_step(s_s[n % 2], stats, diag_mask if j == i else None, tq)
        for c in range(2):
            p_s[n % 2, c] = ps[c]
        if j == i:
            final_stats[i] = stats
    values(len(steps) - 1)


def _attn_prompt(da_lambda, subln_col, q, k, v, *, lam_init):
    bx, t, _ = q.shape
    n_tiles = t // _TK
    seq = pl.BlockSpec((1, t, LANES), lambda b, h: (b, 0, h))
    return pl.pallas_call(
        functools.partial(_attn_prompt_kernel, n_tiles=n_tiles, lam_init=lam_init),
        grid=(bx, DA_HEADS),
        in_specs=[_resident((4, DA_HEAD_DIM)), _resident((DA_V_DIM, 1)), seq, seq, seq],
        out_specs=seq,
        out_shape=jax.ShapeDtypeStruct((bx, t, DA_WIDTH), BF16),
        scratch_shapes=[pltpu.VMEM((n_tiles, DA_V_DIM, _TK), BF16),
                        pltpu.VMEM((2, _TK, 2 * _TK), F32), pltpu.VMEM((2, 2, _TK, _TK), BF16),
                        pltpu.VMEM((2, DA_V_DIM, _TK), F32)],
        compiler_params=_cparams(2),
        name="diff_attn",
    )(da_lambda, subln_col, q, k, v)


def _attn_sample_kernel(lam_ref, sub_ref, q_ref, k_ref, v_ref, kp_ref, vp_ref, o_ref, kb_s, vb_s,
                        *, tq, n_new, n_past, n_parts, n_valid_new, lam_init):
    part = pl.program_id(1)
    rows_part = n_past // n_parts
    for r0 in range(0, rows_part, _TK):
        dst = pl.ds(pl.multiple_of(part * rows_part + r0, _TK), _TK)
        kb_s[dst, :] = kp_ref[0, 0, r0:r0 + _TK].reshape(_TK, DA_WIDTH).astype(BF16)
        vb_s[dst, :] = vp_ref[0, 0, r0:r0 + _TK].reshape(_TK, DA_WIDTH).astype(BF16)

    @pl.when(part == n_parts - 1)
    def _():
        _attn_sample_heads(lam_ref, sub_ref, q_ref, k_ref, v_ref, o_ref, kb_s, vb_s, tq=tq, n_new=n_new,
                           n_past=n_past, n_valid_new=n_valid_new, lam_init=lam_init)


def _attn_sample_heads(lam_ref, sub_ref, q_ref, k_ref, v_ref, o_ref, kb_s, vb_s,
                       *, tq, n_new, n_past, n_valid_new, lam_init):
    lam = _lambda(lam_ref, lam_init)
    sub_col = sub_ref[...]
    valid = lax.broadcasted_iota(jnp.int32, (n_new, tq), 0) < n_valid_new
    kb_s[n_past:n_past + n_new, :] = k_ref[0]
    vb_s[n_past:n_past + n_new, :] = v_ref[0]
    bounds = [(r0, _TK, None) for r0 in range(0, n_past, _TK)] + [(n_past, n_new, valid)]
    for h in range(DA_HEADS):
        lanes = slice(h * LANES, (h + 1) * LANES)
        qbd = _q_blockdiag(q_ref[0, :, lanes])
        blocks = [(kb_s[r0:r0 + n, lanes], vb_s[r0:r0 + n, lanes].astype(F32).T.astype(BF16), mask)
                  for r0, n, mask in bounds]
        scores = [_dot(kblk, qbd) for kblk, _, _ in blocks]
        stats = (jnp.full((1, tq), NEG_BIG, F32), jnp.zeros((1, tq), F32)) * 2
        accs = (jnp.zeros((DA_V_DIM, tq), F32),) * 2
        for s, (_, vtblk, mask) in zip(scores, blocks):
            stats, alphas, ps = _softmax_step(s, stats, mask, tq)
            accs = tuple(alphas[c] * accs[c] + _dot(vtblk, ps[c]) for c in range(2))
        _, l1, _, l2 = stats
        o_ref[0, :, lanes] = _attn_finish(accs[0], accs[1], l1, l2, lam, sub_col, lam_init)


def _attn_sample(da_lambda, subln_col, q, k, v, past_k, past_v, *, layer, lam_init, n_valid_new):
    bx, tq, _ = q.shape
    n_new = k.shape[1]
    n_past = past_k.shape[2]
    n_parts = CACHE_PARTS
    new = pl.BlockSpec((1, n_new, DA_WIDTH), lambda b, j: (b, 0, 0))
    old = pl.BlockSpec((1, 1, n_past // n_parts, DA_HEADS, DA_V_DIM), lambda b, j: (layer, b, j, 0, 0))
    return pl.pallas_call(
        functools.partial(_attn_sample_kernel, tq=tq, n_new=n_new, n_past=n_past, n_parts=n_parts,
                          n_valid_new=n_valid_new, lam_init=lam_init),
        grid=(bx, n_parts),
        in_specs=[_resident((4, DA_HEAD_DIM)), _resident((DA_V_DIM, 1)),
                  pl.BlockSpec((1, tq, DA_WIDTH), lambda b, j: (b, 0, 0)), new, new, old, old],
        out_specs=pl.BlockSpec((1, tq, DA_WIDTH), lambda b, j: (b, 0, 0)),
        out_shape=jax.ShapeDtypeStruct((bx, tq, DA_WIDTH), BF16),
        scratch_shapes=[pltpu.VMEM((n_past + n_new, DA_WIDTH), BF16)] * 2,
        compiler_params=_cparams(2),
        name="diff_attn_step",
    )(da_lambda, subln_col, q, k, v, past_k, past_v)


def _wkv_kernel(rt_ref, at_ref, bt_ref, kt_ref, v_ref, g_ref, wc_ref, s0_ref, rk_ref, lnw_ref, lnb_ref,
                o_ref, sout_ref, s_scr, rp_s, m_s, y0_s, gc_s, bon_s, *, n_pairs, n_chunks):
    C = WKV_CHUNK
    R = 2 * C
    t = pl.program_id(2)

    @pl.when(t == 0)
    def _():
        s_scr[...] = s0_ref[0]

    ri = lax.broadcasted_iota(jnp.int32, (R, R), 0)
    ci = lax.broadcasted_iota(jnp.int32, (R, R), 1)
    sh = lambda x, sz: jnp.right_shift(x, sz.bit_length() - 1)
    same_head = sh(ri, C) == sh(ci, C)
    strict = same_head & (ci < ri)
    incl = same_head & (ci <= ri)
    eye = ri == ci

    def bd(sz):
        return sh(ri, sz) == sh(ci, sz)

    lane_lo = lax.broadcasted_iota(jnp.int32, (C, LANES), 1) < RW_HEAD_DIM

    def stack(x):
        z = jnp.zeros_like(x)
        return jnp.concatenate([jnp.where(lane_lo, x, z), jnp.where(lane_lo, z, x)], axis=0)

    b16 = lambda x: x.astype(BF16)
    f32 = lambda x: x.astype(F32)
    ident = jnp.where(eye, 1.0, 0.0)

    units = [(p, c) for p in range(n_pairs) for c in range(n_chunks)]
    idx = range(len(units))
    rs, asb, bsb, ksb, vsb, bdk, kdk, wcs = [], [], [], [], [], [], [], []
    for u, (p, c) in enumerate(units):
        rows, lanes = slice(c * C, (c + 1) * C), slice(p * LANES, (p + 1) * LANES)
        r_, a_, b_, k_, v_ = (stack(ref[0, rows, lanes]) for ref in (rt_ref, at_ref, bt_ref, kt_ref, v_ref))
        wc = wc_ref[0, c, :, lanes]
        rk = rk_ref[:, lanes]
        bon_s[u] = jnp.sum(f32(r_) * f32(k_) * rk, axis=1, keepdims=True) * f32(v_)
        rs.append(r_); asb.append(a_); bsb.append(b_); ksb.append(k_); vsb.append(v_)
        bdk.append(b16(f32(b_) * wc)); kdk.append(b16(f32(k_) * wc)); wcs.append(wc)

    n, n8, aak, arb, ark = [], [], [], [], []
    for u in idx:
        ab = _dot_nt(jnp.concatenate([asb[u], rs[u]], axis=0), jnp.concatenate([bsb[u], ksb[u]], axis=0))
        nc = jnp.where(strict, ab[:R, :R], 0.0)
        n.append(b16(nc)); n8.append(jnp.where(bd(8), nc, 0.0))
        aak.append(b16(jnp.where(strict, ab[:R, R:], 0.0)))
        arb.append(b16(jnp.where(incl, ab[R:, :R], 0.0)))
        ark.append(b16(jnp.where(incl, ab[R:, R:], 0.0)))

    n8b = [b16(x) for x in n8]
    n2 = [_dot(n8b[u], n8b[u]) for u in idx]
    n2b = [b16(x) for x in n2]
    n4 = [_dot(n2b[u], n2b[u]) for u in idx]
    p1 = [_dot(b16(ident + n8[u]), b16(ident + n2[u])) for u in idx]
    tinv = [_dot(b16(p1[u]), b16(ident + n4[u])) for u in idx]
    zero16 = jnp.zeros((R, R), BF16)
    for sz in (8, 16, 32):
        off = bd(2 * sz) & jnp.logical_not(bd(sz))
        tb = [b16(x) for x in tinv]
        x = [_dot(tb[u], jnp.where(off, n[u], zero16)) for u in idx]
        tinv = [tinv[u] + _dot(b16(x[u]), tb[u]) for u in idx]
    tb = [b16(x) for x in tinv]

    u0p = [_dot(aak[u], vsb[u]) for u in idx]
    tx = [_dot(tb[u], jnp.concatenate([asb[u], b16(u0p[u])], axis=1)) for u in idx]
    txb = [b16(x) for x in tx]
    az = [_dot(arb[u], txb[u]) for u in idx]
    for u in idx:
        rp_s[u] = b16(f32(rs[u]) + az[u][:, :LANES])
        y0_s[u] = az[u][:, LANES:] + _dot(ark[u], vsb[u])
    for u in idx:
        txt = b16(tx[u].T)
        mg = _dot(txt, bdk[u])
        m_s[u] = b16(jnp.where(eye, wcs[u], 0.0) + mg[:LANES])
        gc_s[u] = mg[LANES:] + _dot(b16(f32(vsb[u]).T), kdk[u])

    s = [s_scr[p] for p in range(n_pairs)]
    for c in range(n_chunks):
        for p in range(n_pairs):
            u = p * n_chunks + c
            rows, lanes = slice(c * C, (c + 1) * C), slice(p * LANES, (p + 1) * LANES)
            sb = b16(s[p])
            y = _dot_nt(rp_s[u], sb) + y0_s[u]
            s[p] = _dot(sb, m_s[u]) + gc_s[u]
            mu = jnp.sum(y, axis=1, keepdims=True) * (1.0 / RW_HEAD_DIM)
            d = jnp.where(same_head, y - mu, 0.0)
            var = jnp.sum(d * d, axis=1, keepdims=True) * (1.0 / RW_HEAD_DIM)
            ost = (d * lax.rsqrt(var + GN_EPS) * lnw_ref[:, lanes]
                   + jnp.where(same_head, lnb_ref[:, lanes], 0.0) + bon_s[u])
            o_ref[0, rows, lanes] = ((ost[:C] + ost[C:]) * f32(g_ref[0, rows, lanes])).astype(BF16)
    for p in range(n_pairs):
        s_scr[p] = s[p]
        sout_ref[0, p] = s[p]


def _wkv(rt, at, bt, kt, v, g, wc, s0bd, rk, lnw, lnb, *, tb, n_pairs):
    bx, t, _ = rt.shape
    n_chunks = tb // WKV_CHUNK
    n_units = n_pairs * n_chunks
    w = n_pairs * LANES
    seq = pl.BlockSpec((1, tb, w), lambda b, p, i: (b, i, p))
    vec = pl.BlockSpec((1, w), lambda b, p, i: (0, p))
    st = pl.BlockSpec((1, n_pairs, LANES, LANES), lambda b, p, i: (b, p, 0, 0))
    mat = lambda dt: pltpu.VMEM((n_units, LANES, LANES), dt)
    return pl.pallas_call(
        functools.partial(_wkv_kernel, n_pairs=n_pairs, n_chunks=n_chunks),
        grid=(bx, N_PAIRS // n_pairs, t // tb),
        in_specs=[seq] * 6 + [pl.BlockSpec((1, n_chunks, 1, w), lambda b, p, i: (b, i, 0, p)),
                              st, vec, vec, vec],
        out_specs=[seq, st],
        out_shape=[jax.ShapeDtypeStruct((bx, t, RW_WIDTH), BF16),
                   jax.ShapeDtypeStruct((bx, N_PAIRS, LANES, LANES), F32)],
        scratch_shapes=[pltpu.VMEM((n_pairs, LANES, LANES), F32),
                        mat(BF16), mat(BF16), mat(F32), mat(F32), mat(F32)],
        compiler_params=_cparams(3),
        name="wkv",
    )(rt, at, bt, kt, v, g, wc, s0bd, rk, lnw, lnb)


def _merge_kernel(x_ref, oda_ref, orw_ref, gt_ref, wda_ref, wrw_ref, wout_ref, o_ref):
    a = _dot(oda_ref[0], wda_ref[...])
    b = _dot(orw_ref[0], wrw_ref[...])
    gt = gt_ref[0].astype(F32)
    m = _sigmoid(gt[:, :D_MODEL]) * a + _sigmoid(gt[:, D_MODEL:]) * b
    o_ref[0] = x_ref[0] + _dot(m.astype(BF16), wout_ref[...])


def _merge(x, oda, orw, gt, wda, wrw, wout, *, tm):
    bx, t, _ = x.shape
    row = lambda w: pl.BlockSpec((1, tm, w), lambda b, i: (b, i, 0))
    sq = _resident((D_MODEL, D_MODEL))
    return pl.pallas_call(
        _merge_kernel,
        grid=(bx, t // tm),
        in_specs=[row(D_MODEL), row(DA_WIDTH), row(RW_WIDTH), row(P_GATE), sq, sq, sq],
        out_specs=row(D_MODEL),
        out_shape=jax.ShapeDtypeStruct((bx, t, D_MODEL), F32),
        compiler_params=_cparams(2),
        name="merge",
    )(x, oda, orw, gt, wda, wrw, wout)


_FF_CW = 256
_HALO = SUBLANES


def _ffn_kernel(x_ref, halo_ref, c0_ref, g_ref, wup_ref, f_ref, fb_ref, wdn_ref, gf_ref,
                o_ref, conv_ref, ext_s, act_s, *, tm, final):
    first = pl.program_id(1) == 0
    xe = jnp.concatenate([halo_ref[0], x_ref[0]], axis=0)
    hb = _rms(xe, g_ref[...]).astype(BF16)
    starts = list(range(0, D_FF, _FF_CW))

    def up_proj(c0):
        for base in (c0, D_FF + c0):
            cols = slice(base, base + _FF_CW)
            up = _dot(hb, wup_ref[:, cols])
            ext_s[:, cols] = up
            ext_s[_HALO - 2:_HALO, cols] = jnp.where(first, c0_ref[0, :, cols], up[_HALO - 2:_HALO])
            conv_ref[0, :, cols] = up[_HALO + tm - 2:_HALO + tm]

    def conv_act(c0):
        cs = []
        for base in (c0, D_FF + c0):
            cols = slice(base, base + _FF_CW)
            f = f_ref[:, cols]
            cs.append(fb_ref[:, cols]
                      + ext_s[_HALO - 2:_HALO - 2 + tm, cols] * f[0:1]
                      + ext_s[_HALO - 1:_HALO - 1 + tm, cols] * f[1:2]
                      + ext_s[_HALO:_HALO + tm, cols] * f[2:3])
        ca, cb = cs
        act_s[:, c0:c0 + _FF_CW] = (ca * _sigmoid(ca) * cb).astype(BF16)

    up_proj(starts[0])
    for j, c0 in enumerate(starts):
        if j + 1 < len(starts):
            up_proj(starts[j + 1])
        conv_act(c0)
    out = x_ref[0] + _dot(act_s[...], wdn_ref[...])
    if final:
        out = _rms(out, gf_ref[...])
    o_ref[0] = out


def _ffn(x, conv0, g, wup, f, fb, wdn, gfinal, *, tm, final):
    bx, t, _ = x.shape
    row = pl.BlockSpec((1, tm, D_MODEL), lambda b, i: (b, i, 0))
    halo = pl.BlockSpec((1, _HALO, D_MODEL),
                        lambda b, i: (b, jnp.maximum(i * (tm // _HALO) - 1, 0), 0))
    cst = pl.BlockSpec((1, CONV_W - 1, 2 * D_FF), lambda b, i: (b, 0, 0))
    return pl.pallas_call(
        functools.partial(_ffn_kernel, tm=tm, final=final),
        grid=(bx, t // tm),
        in_specs=[row, halo, cst, _resident((1, D_MODEL)), _resident((D_MODEL, 2 * D_FF)),
                  _resident((CONV_W, 2 * D_FF)), _resident((1, 2 * D_FF)), _resident((D_FF, D_MODEL)),
                  _resident((1, D_MODEL))],
        out_specs=[row, cst],
        out_shape=[jax.ShapeDtypeStruct((bx, t, D_MODEL), F32),
                   jax.ShapeDtypeStruct((bx, CONV_W - 1, 2 * D_FF), F32)],
        scratch_shapes=[pltpu.VMEM((_HALO + tm, 2 * D_FF), F32), pltpu.VMEM((tm, D_FF), BF16)],
        compiler_params=_cparams(2),
        name="ffn",
    )(x, x, conv0, g, wup, f, fb, wdn, gfinal)


def _rope_tables(pos):
    half = ROT_DIM // 2
    inv = ROPE_THETA ** (-jnp.arange(0, ROT_DIM, 2, dtype=F32) / ROT_DIM)
    ang = pos.astype(F32)[:, None] * inv[None, :]
    cos, sin = jnp.cos(ang), jnp.sin(ang)
    t = pos.shape[0]
    pad = jnp.zeros((t, DA_HEAD_DIM - ROT_DIM), F32)
    z = jnp.zeros((t, half), F32)
    one_map = lambda a, b, fill: jnp.concatenate([a, b, pad + fill], axis=1)
    rc = one_map(cos, cos, 1.0)
    ra = one_map(z, sin, 0.0)
    rb = one_map(-sin, z, 0.0)
    dup = lambda m: jnp.concatenate([m, m], axis=1)
    return dup(rc), dup(ra), dup(rb)


def _state_to_blockdiag(s):
    b = s.shape[0]
    s = s.reshape(b, N_PAIRS, 2, RW_HEAD_DIM, RW_HEAD_DIM)
    z = jnp.zeros_like(s[:, :, 0])
    top = jnp.concatenate([s[:, :, 0], z], axis=-1)
    bot = jnp.concatenate([z, s[:, :, 1]], axis=-1)
    return jnp.concatenate([top, bot], axis=-2)


def _blockdiag_to_state(sbd):
    b = sbd.shape[0]
    h0 = sbd[:, :, :RW_HEAD_DIM, :RW_HEAD_DIM]
    h1 = sbd[:, :, RW_HEAD_DIM:, RW_HEAD_DIM:]
    return jnp.stack([h0, h1], axis=2).reshape(b, RW_HEADS, RW_HEAD_DIM, RW_HEAD_DIM)


def _layer(x, lidx, depth, tabs, kvbuf, cache, wkv0, shift0, conv0, w, norm_final, *, final):
    bx, t, _ = x.shape
    prompt = cache is None
    lam_init = 0.8 - 0.6 * math.exp(-0.3 * lidx)
    prep = lambda tri: (w["rw_mu"], w["rw_w0"], w["rw_w2p"], w["rw_a0"], w["rw_a2p"], w["rw_g2"],
                        w["rw_k_k"], w["rw_k_a"], w["seg"], tri)

    if prompt:
        tm = ROW_TILE
        q, k, v, kout, vout, gate, rt, at, bt, kt, vr, g, wc, new_shift = _in_proj(
            x, w["norm_mix"], w["w_in"], tabs, kvbuf, layer=lidx, depth=depth, tm=tm, fused=True,
            shift0=shift0, prep=prep(w["tri_p"]))
        o_da = _attn_prompt(w["da_lambda"], w["da_subln"], q, k, v, lam_init=lam_init)
    else:
        tm = t
        flat = lambda a: a.reshape(1, bx * t, a.shape[-1])
        q, k, v, kout, vout, gate, u_rw = _in_proj(flat(x), w["norm_mix"], w["w_in"], tabs, kvbuf,
                                                   layer=lidx, depth=depth, tm=bx * t, fused=False)
        u_rw = u_rw.reshape(bx, t, P_RW)
        new_shift = u_rw[:, -1:]
        pad_rows = lambda a, n: jnp.pad(a.reshape(bx, t, a.shape[-1]), ((0, 0), (0, n - t), (0, 0)))
        o_da = _attn_sample(w["da_lambda"], w["da_subln"], pad_rows(q, LANES), pad_rows(k, LANES),
                            pad_rows(v, LANES), *cache, layer=lidx, lam_init=lam_init, n_valid_new=t)[:, :t]
        rt, at, bt, kt, vr, g, wc = _rw_prep(u_rw, shift0, prep(w["tri_s"]), tm=t)
        rt, at, bt, kt, vr, g = [pad_rows(a, WKV_CHUNK) for a in (rt, at, bt, kt, vr, g)]
    o_rw, sbd = _wkv(rt, at, bt, kt, vr, g, wc, _state_to_blockdiag(wkv0),
                     w["rw_r_k"], w["rw_ln_w"], w["rw_ln_b"],
                     tb=WKV_ROWS if prompt else WKV_CHUNK, n_pairs=WKV_PAIRS if prompt else N_PAIRS)
    o_rw = o_rw[:, :t]

    if prompt:
        x = _merge(x, o_da, o_rw, gate, w["w_o_da"], w["w_o_rw"], w["w_out"], tm=tm)
    else:
        x = _merge(flat(x), flat(o_da), flat(o_rw), gate, w["w_o_da"], w["w_o_rw"], w["w_out"],
                   tm=bx * t).reshape(bx, t, D_MODEL)
    x, new_conv = _ffn(x, conv0, w["norm_ffn"], w["w_up"], w["ffn_conv"], w["ffn_conv_b"], w["w_down"],
                       norm_final, tm=tm, final=final)
    return x, (kout, vout), _blockdiag_to_state(sbd), new_shift, new_conv


def kernel(x_prompt, x_sample, cache_k, cache_v, state_wkv, state_shift, state_ffn_conv, norm_mix, w_in, da_lambda, da_subln, w_o_da, rw_mu, rw_w0, rw_w2, rw_a0, rw_a2, rw_g2, rw_k_k, rw_k_a, rw_r_k, rw_ln_w, rw_ln_b, w_o_rw, w_out, norm_ffn, w_up, ffn_conv, ffn_conv_b, w_down, norm_final):
    bp, tp, _ = x_prompt.shape
    bs, ts, _ = x_sample.shape
    depth = w_in.shape[0]
    past = cache_k.shape[2]
    tabs_p = _rope_tables(jnp.arange(tp, dtype=jnp.int32))
    tabs_s = tuple(jnp.tile(a, (bs, 1)) for a in _rope_tables(past + jnp.arange(ts, dtype=jnp.int32)))
    cache = (cache_k, cache_v)

    lane = jnp.arange(2 * LANES)
    seg = (lane[:, None] // RW_HEAD_DIM == lane[None, :] // RW_HEAD_DIM).astype(BF16)
    tri = lambda n: jnp.tile((jnp.arange(n)[:, None] >= jnp.arange(n)[None, :]).astype(BF16), (1, 3))
    tri_p, tri_s = tri(WKV_CHUNK), tri(ts)
    zl = jnp.zeros((DECAY_LORA, RW_WIDTH), F32)
    row = lambda a: a.reshape(1, -1)
    zero = lambda *s: jnp.zeros(s, x_prompt.dtype)
    wkv_z, shift_z, conv_z = zero(bp, RW_HEADS, RW_HEAD_DIM, RW_HEAD_DIM), zero(bp, 1, P_RW), zero(bp, CONV_W - 1, 2 * D_FF)

    xp, xs = x_prompt, x_sample
    kv_p = kv_s = None
    outs_p, outs_s = [], []
    nf = row(norm_final)
    for l in range(depth):
        w = dict(
            norm_mix=row(norm_mix[l]), w_in=w_in[l].astype(BF16), da_lambda=da_lambda[l],
            da_subln=da_subln[l].reshape(-1, 1), w_o_da=w_o_da[l].astype(BF16),
            rw_mu=row(rw_mu[l]), rw_w0=row(rw_w0[l]),
            rw_w2p=jnp.concatenate([rw_w2[l], zl], axis=0).astype(BF16),
            rw_a0=row(rw_a0[l]), rw_a2p=jnp.concatenate([zl, rw_a2[l]], axis=0).astype(BF16),
            rw_g2=rw_g2[l].astype(BF16), rw_k_k=row(rw_k_k[l]), rw_k_a=row(rw_k_a[l]),
            rw_r_k=row(rw_r_k[l]), rw_ln_w=row(rw_ln_w[l]), rw_ln_b=row(rw_ln_b[l]),
            w_o_rw=w_o_rw[l].astype(BF16), w_out=w_out[l].astype(BF16), norm_ffn=row(norm_ffn[l]),
            w_up=w_up[l].astype(BF16), ffn_conv=ffn_conv[l], ffn_conv_b=row(ffn_conv_b[l]),
            w_down=w_down[l].astype(BF16), seg=seg, tri_p=tri_p, tri_s=tri_s)
        final = l == depth - 1
        xp, kv_p, sp, shp, cp = _layer(xp, l, depth, tabs_p, kv_p, None, wkv_z, shift_z, conv_z, w, nf, final=final)
        xs, kv_s, sq, shq, cq = _layer(xs, l, depth, tabs_s, kv_s, cache, state_wkv[l], state_shift[l],
                                       state_ffn_conv[l], w, nf, final=final)
        outs_p.append((sp, shp, cp))
        outs_s.append((sq, shq, cq))

    wkv_prompt, shift_prompt, conv_prompt = [jnp.stack(t) for t in zip(*outs_p)]
    wkv_sample, shift_sample, conv_sample = [jnp.stack(t) for t in zip(*outs_s)]
    heads = lambda a, b, t: a.reshape(depth, b, t, DA_HEADS, DA_V_DIM)
    return (xp, xs, heads(kv_p[0], bp, tp), heads(kv_p[1], bp, tp), wkv_prompt, shift_prompt, conv_prompt,
            heads(kv_s[0], bs, ts), heads(kv_s[1], bs, ts), wkv_sample, shift_sample, conv_sample)
```

```python
import functools
import math

import jax
import jax.numpy as jnp
from jax import lax
from jax.experimental import pallas as pl
from jax.experimental.pallas import tpu as pltpu

F32 = jnp.float32
BF16 = jnp.bfloat16

D_MODEL = 1024
CHUNK = 64
EPS = 1e-6
DA_HEADS = 8
DA_HEAD_DIM = 64
DA_V_DIM = 128
DA_WIDTH = DA_HEADS * DA_V_DIM
ROPE_THETA = 500000.0
ROT_DIM = DA_HEAD_DIM // 4
RW_HEAD_DIM = 64
RW_HEADS = 16
RW_WIDTH = 1024
DECAY_LORA = 64
AAA_LORA = 64
GATE_LORA = 128
GN_EPS = 64e-5
D_FF = 2816
CONV_W = 3
P_DA = 3 * DA_WIDTH
P_RW = 3 * RW_WIDTH + DECAY_LORA + AAA_LORA + GATE_LORA
P_GATE = 2 * D_MODEL
P_TOTAL = P_DA + P_RW + P_GATE

LANES = 128
SUBLANES = 8
VMEM_LIMIT = 56 * 1024 * 1024

WKV_CHUNK = 64
N_PAIRS = RW_HEADS // 2
NEG_BIG = -1e30
ROW_TILE = 256
WKV_ROWS = 512
WKV_PAIRS = 4
CACHE_PARTS = 2

_ARB = "arbitrary"


def _cparams(n_axes):
    return pltpu.CompilerParams(dimension_semantics=(_ARB,) * n_axes,
                                vmem_limit_bytes=VMEM_LIMIT)


def _resident(shape):
    nd = len(shape)
    return pl.BlockSpec(shape, lambda *_: (0,) * nd, pipeline_mode=pl.Buffered(1))


def _dot(a, b):
    return jnp.dot(a, b, preferred_element_type=F32)


def _dot_nt(a, b):
    return lax.dot_general(a, b, (((1,), (1,)), ((), ())), preferred_element_type=F32)


def _rms(x, g):
    return x * lax.rsqrt(jnp.mean(x * x, axis=-1, keepdims=True) + EPS) * g


def _sigmoid(x):
    return 1.0 / (1.0 + jnp.exp(-x))


def _split3(x):
    hi = x.astype(BF16)
    r1 = x - hi.astype(F32)
    mid = r1.astype(BF16)
    lo = (r1 - mid.astype(F32)).astype(BF16)
    return hi, mid, lo


def _prep_rows(us, w0_ref, w2_ref, a0_ref, a2_ref, g2_ref, kk_ref, ka_ref, seg_ref):
    r = us[:, 0:RW_WIDTH]
    kr = us[:, RW_WIDTH:2 * RW_WIDTH]
    vr = us[:, 2 * RW_WIDTH:3 * RW_WIDTH]
    wa = us[:, 3 * RW_WIDTH:3 * RW_WIDTH + DECAY_LORA + AAA_LORA]
    gd = us[:, 3 * RW_WIDTH + DECAY_LORA + AAA_LORA:]

    z = w0_ref[...] + _dot(jnp.tanh(wa).astype(BF16), w2_ref[...])
    lw = -math.exp(-0.5) * _sigmoid(z)
    a = _sigmoid(a0_ref[...] + _dot(wa.astype(BF16), a2_ref[...]))
    g = _dot(_sigmoid(gd).astype(BF16), g2_ref[...])

    kk = kr * kk_ref[...]
    k2 = (kk * kk).astype(BF16)
    seg = seg_ref[...]
    w = seg.shape[0]
    ssum = jnp.concatenate([_dot(k2[:, c:c + w], seg) for c in range(0, RW_WIDTH, w)],
                           axis=1)
    kk = kk * lax.rsqrt(jnp.maximum(ssum, 1e-24))
    kmod = kr * (1.0 + (a - 1.0) * ka_ref[...])
    return r, vr, g, lw, a, kk, kmod


def _prep_chunk(r, lw, a, kk, kmod, tri):
    cum = _dot(tri, jnp.concatenate(_split3(lw), axis=0))
    e_pos = jnp.exp(cum)
    e_neg = jnp.exp(-cum)
    rt = r * e_pos
    at = -kk * jnp.exp(cum - lw)
    bt = kk * a * e_neg
    kt = kmod * e_neg
    return rt, at, bt, kt, e_pos[-1:, :]


def _rw_prep_kernel(u_ref, sh_ref, mu_ref, w0_ref, w2_ref, a0_ref, a2_ref, g2_ref,
                    kk_ref, ka_ref, seg_ref, tri_ref,
                    rt_ref, at_ref, bt_ref, kt_ref, v_ref, g_ref, wc_ref, *, tm):
    u = u_ref[0]
    rows = lax.broadcasted_iota(jnp.int32, (tm, 1), 0)
    up = jnp.where(rows == 0, sh_ref[0], pltpu.roll(u, 1, 0))
    us = u + (up - u) * mu_ref[...]
    r, vr, g, lw, a, kk, kmod = _prep_rows(us, w0_ref, w2_ref, a0_ref, a2_ref, g2_ref, kk_ref, ka_ref, seg_ref)
    rt, at, bt, kt, wc = _prep_chunk(r, lw, a, kk, kmod, tri_ref[...])
    for ref, val in ((rt_ref, rt), (at_ref, at), (bt_ref, bt), (kt_ref, kt), (v_ref, vr), (g_ref, g)):
        ref[0] = val.astype(BF16)
    wc_ref[0, 0] = wc


def _prep_param_specs(chunk):
    return [_resident((1, P_RW)), _resident((1, RW_WIDTH)),
            _resident((DECAY_LORA + AAA_LORA, RW_WIDTH)), _resident((1, RW_WIDTH)),
            _resident((DECAY_LORA + AAA_LORA, RW_WIDTH)), _resident((GATE_LORA, RW_WIDTH)),
            _resident((1, RW_WIDTH)), _resident((1, RW_WIDTH)),
            _resident((2 * LANES, 2 * LANES)), _resident((chunk, 3 * chunk))]


def _rw_prep(u, shift0, prep, *, tm):
    bx, t, _ = u.shape
    assert t == tm
    row = lambda w: pl.BlockSpec((1, tm, w), lambda b: (b, 0, 0))
    outs = [jax.ShapeDtypeStruct((bx, t, RW_WIDTH), BF16)] * 6
    outs.append(jax.ShapeDtypeStruct((bx, 1, 1, RW_WIDTH), F32))
    return pl.pallas_call(
        functools.partial(_rw_prep_kernel, tm=tm),
        grid=(bx,),
        in_specs=[row(P_RW), pl.BlockSpec((1, 1, P_RW), lambda b: (b, 0, 0))] + _prep_param_specs(tm),
        out_specs=[row(RW_WIDTH)] * 6 + [pl.BlockSpec((1, 1, 1, RW_WIDTH), lambda b: (b, 0, 0, 0))],
        out_shape=outs,
        compiler_params=_cparams(1),
        name="rw_prep",
    )(u, shift0, *prep)


_IN_CW = 256
_LOG2E = math.log2(math.e)
_N_PREP = 10


def _in_proj_kernel(*refs, tm, fused, aliased):
    x_ref, g_ref, w_ref, rc_ref, ra_ref, rb_ref = refs[:6]
    n_in = 6
    if fused:
        sh_ref = refs[6]
        mu_ref, w0_ref, w2_ref, a0_ref, a2_ref, g2_ref, kk_ref, ka_ref, seg_ref, tri_ref = refs[7:7 + _N_PREP]
        n_in = 7 + _N_PREP
    n_in += 2 if aliased else 0
    q_ref, kn_ref, vn_ref, ko_ref, vo_ref, gt_ref = refs[n_in:n_in + 6]
    if fused:
        rt_ref, at_ref, bt_ref, kt_ref, vr_ref, gr_ref, wc_ref, last_ref = refs[n_in + 6:n_in + 14]
        k_scr, v_scr, u_s = refs[n_in + 14:]
    else:
        u_ref = refs[n_in + 6]
        k_scr, v_scr = refs[n_in + 7:]
    hb = _rms(x_ref[0], g_ref[...]).astype(BF16)
    rc = rc_ref[...]
    ra = ra_ref[...]
    rb = rb_ref[...]

    def rope(t):
        return t * rc + pltpu.roll(t, ROT_DIM // 2, 1) * ra + pltpu.roll(t, LANES - ROT_DIM // 2, 1) * rb

    def chunk(c0):
        if P_DA <= c0 < P_DA + P_RW:
            o = c0 - P_DA
            if fused:
                u_s[SUBLANES:, o:o + _IN_CW] = _dot(hb, w_ref[:, c0:c0 + _IN_CW])
            else:
                u_ref[0, :, o:o + _IN_CW] = _dot(hb, w_ref[:, c0:c0 + _IN_CW])
            return
        acc = _dot(hb, w_ref[:, c0:c0 + _IN_CW])
        if c0 < DA_WIDTH:
            for s in range(0, _IN_CW, LANES):
                q_ref[0, :, c0 + s:c0 + s + LANES] = (
                    rope(acc[:, s:s + LANES]) * (DA_HEAD_DIM ** -0.5 * _LOG2E)).astype(BF16)
        elif c0 < 2 * DA_WIDTH:
            o = c0 - DA_WIDTH
            for s in range(0, _IN_CW, LANES):
                kr = rope(acc[:, s:s + LANES])
                k_scr[:, o + s:o + s + LANES] = kr
                kn_ref[0, :, o + s:o + s + LANES] = kr.astype(BF16)
        elif c0 < P_DA:
            o = c0 - 2 * DA_WIDTH
            v_scr[:, o:o + _IN_CW] = acc
            vn_ref[0, :, o:o + _IN_CW] = acc.astype(BF16)
        else:
            o = c0 - P_DA - P_RW
            gt_ref[0, :, o:o + _IN_CW] = acc.astype(BF16)

    def chunks(lo, hi):
        for c0 in range(lo, hi, _IN_CW):
            chunk(c0)

    if not fused:
        chunks(0, P_TOTAL)
    else:
        @pl.when(pl.program_id(1) == 0)
        def _():
            u_s[SUBLANES - 1:SUBLANES, :] = sh_ref[0]

        chunks(0, P_DA + P_RW)
        u = u_s[SUBLANES:SUBLANES + tm, :]
        up = u_s[SUBLANES - 1:SUBLANES - 1 + tm, :]
        us = u + (up - u) * mu_ref[...]
        last = u[tm - 1:tm, :]
        last_ref[0] = last
        u_s[SUBLANES - 1:SUBLANES, :] = last
        r, vr, g, lw, a, kk, kmod = _prep_rows(us, w0_ref, w2_ref, a0_ref, a2_ref, g2_ref, kk_ref, ka_ref, seg_ref)
        vr_ref[0] = vr.astype(BF16)
        gr_ref[0] = g.astype(BF16)
        chunks(P_DA + P_RW, P_DA + P_RW + P_GATE // 2)
        tri = tri_ref[...]
        for c in range(tm // WKV_CHUNK):
            sl = slice(c * WKV_CHUNK, (c + 1) * WKV_CHUNK)
            rt, at, bt, kt, wc = _prep_chunk(r[sl], lw[sl], a[sl], kk[sl], kmod[sl], tri)
            for ref, val in ((rt_ref, rt), (at_ref, at), (bt_ref, bt), (kt_ref, kt)):
                ref[0, sl, :] = val.astype(BF16)
            wc_ref[0, c] = wc
        chunks(P_DA + P_RW + P_GATE // 2, P_TOTAL)
    ko_ref[0, 0] = k_scr[...].reshape(tm, DA_HEADS, DA_V_DIM)
    vo_ref[0, 0] = v_scr[...].reshape(tm, DA_HEADS, DA_V_DIM)


def _in_proj(x, g, w_bf, tabs, kvout, *, layer, depth, tm, fused, shift0=None, prep=None):
    bx, t, _ = x.shape
    row = lambda w: pl.BlockSpec((1, tm, w), lambda b, i: (b, i, 0))
    tab = pl.BlockSpec((tm, LANES), lambda b, i: (i, 0))
    slot = pl.BlockSpec((1, 1, tm, DA_HEADS, DA_V_DIM), lambda b, i: (layer, b, i, 0, 0))
    aliased = kvout is not None
    in_specs = [row(D_MODEL), _resident((1, D_MODEL)), _resident((D_MODEL, P_TOTAL)), tab, tab, tab]
    args = [x, g, w_bf, *tabs]
    if fused:
        in_specs += [pl.BlockSpec((1, 1, P_RW), lambda b, i: (b, 0, 0))] + _prep_param_specs(WKV_CHUNK)
        args += [shift0, *prep]
    if aliased:
        aliases = {len(args): 3, len(args) + 1: 4}
        in_specs += [pl.BlockSpec(memory_space=pl.ANY)] * 2
        args += list(kvout)
    else:
        aliases = {}
    nat = jax.ShapeDtypeStruct((bx, t, DA_WIDTH), BF16)
    stacked = jax.ShapeDtypeStruct((depth, bx, t, DA_HEADS, DA_V_DIM), F32)
    out_specs = [row(DA_WIDTH)] * 3 + [slot, slot, row(P_GATE)]
    out_shape = [nat, nat, nat, stacked, stacked, jax.ShapeDtypeStruct((bx, t, P_GATE), BF16)]
    scratch = [pltpu.VMEM((tm, DA_WIDTH), F32)] * 2
    if fused:
        nc = tm // WKV_CHUNK
        out_specs += [row(RW_WIDTH)] * 6 + [pl.BlockSpec((1, nc, 1, RW_WIDTH), lambda b, i: (b, i, 0, 0)),
                                            pl.BlockSpec((1, 1, P_RW), lambda b, i: (b, 0, 0))]
        out_shape += [jax.ShapeDtypeStruct((bx, t, RW_WIDTH), BF16)] * 6
        out_shape += [jax.ShapeDtypeStruct((bx, t // WKV_CHUNK, 1, RW_WIDTH), F32),
                      jax.ShapeDtypeStruct((bx, 1, P_RW), F32)]
        scratch.append(pltpu.VMEM((SUBLANES + tm, P_RW), F32))
    else:
        out_specs.append(row(P_RW))
        out_shape.append(jax.ShapeDtypeStruct((bx, t, P_RW), F32))
    return pl.pallas_call(
        functools.partial(_in_proj_kernel, tm=tm, fused=fused, aliased=aliased),
        grid=(bx, t // tm),
        in_specs=in_specs,
        out_specs=out_specs,
        out_shape=out_shape,
        scratch_shapes=scratch,
        input_output_aliases=aliases,
        compiler_params=_cparams(2),
        name="in_proj",
    )(*args)


_TK = 256
_CHUNK_SHIFT = CHUNK.bit_length() - 1


def _lambda(lam_ref, lam_init):
    lp = lam_ref[...]
    return (jnp.exp(jnp.sum(lp[0:1] * lp[1:2], axis=1, keepdims=True))
            - jnp.exp(jnp.sum(lp[2:3] * lp[3:4], axis=1, keepdims=True)) + lam_init)


def _q_blockdiag(q_tile):
    tq = q_tile.shape[0]
    qt = q_tile.astype(F32).T.astype(BF16)
    z = jnp.zeros((DA_HEAD_DIM, tq), BF16)
    return jnp.concatenate([jnp.concatenate([qt[:DA_HEAD_DIM], z], axis=1),
                            jnp.concatenate([z, qt[DA_HEAD_DIM:]], axis=1)], axis=0)


def _softmax_step(s, stats, mask, tq):
    new_stats, alphas, ps = [], [], []
    for c in range(2):
        m, l = stats[2 * c:2 * c + 2]
        sc = s[:, c * tq:(c + 1) * tq]
        if mask is not None:
            sc = jnp.where(mask, sc, NEG_BIG)
        m_new = jnp.maximum(m, jnp.max(sc, axis=0, keepdims=True))
        alpha = jnp.exp2(m - m_new)
        p = jnp.exp2(sc - m_new)
        new_stats += [m_new, alpha * l + jnp.sum(p, axis=0, keepdims=True)]
        alphas.append(alpha)
        ps.append(p.astype(BF16))
    return tuple(new_stats), tuple(alphas), tuple(ps)


def _attn_finish(a1, a2, l1, l2, lam, sub_col, lam_init):
    ot = a1 * (1.0 / l1) - lam * (a2 * (1.0 / l2))
    ss = jnp.sum(ot * ot, axis=0, keepdims=True) * (1.0 / DA_V_DIM)
    ot = ot * lax.rsqrt(ss + EPS) * sub_col * (1.0 - lam_init)
    return ot.T.astype(BF16)


def _attn_prompt_kernel(lam_ref, sub_ref, q_ref, k_ref, v_ref, o_ref, vt_s, s_s, *, n_tiles, lam_init):
    tq = _TK
    for jb in range(n_tiles):
        vt_s[jb] = v_ref[0, jb * _TK:(jb + 1) * _TK, :].astype(F32).T.astype(BF16)
    lam = _lambda(lam_ref, lam_init)
    sub_col = sub_ref[...]
    kr = jnp.right_shift(lax.broadcasted_iota(jnp.int32, (_TK, 2 * tq), 0), _CHUNK_SHIFT)
    qc = jnp.right_shift(lax.broadcasted_iota(jnp.int32, (_TK, 2 * tq), 1) & (tq - 1), _CHUNK_SHIFT)
    diag_mask = kr <= qc

    def new_tile(i):
        return dict(qbd=_q_blockdiag(q_ref[0, i * tq:(i + 1) * tq, :]), mx=None)

    def scores(i, j, tile):
        s = _dot(k_ref[0, j * _TK:(j + 1) * _TK, :], tile["qbd"])
        if j == i:
            s = jnp.where(diag_mask, s, NEG_BIG)
        s_s[i % 2, j] = s
        part = jnp.max(s.reshape(_TK // SUBLANES, SUBLANES, 2 * tq), axis=0)
        tile["mx"] = part if tile["mx"] is None else jnp.maximum(tile["mx"], part)

    cur = new_tile(0)
    scores(0, 0, cur)
    for i in range(n_tiles):
        m = jnp.max(cur["mx"], axis=0, keepdims=True)
        nxt = new_tile(i + 1) if i + 1 < n_tiles else None
        acc, lsum = [None, None], None
        for j in range(i + 1):
            if nxt is not None:
                scores(i + 1, j, nxt)
            p = jnp.exp2(s_s[i % 2, j] - m)
            part = jnp.sum(p.reshape(_TK // SUBLANES, SUBLANES, 2 * tq), axis=0)
            lsum = part if lsum is None else lsum + part
            pb = p.astype(BF16)
            for c in range(2):
                pv = _dot(vt_s[j], pb[:, c * tq:(c + 1) * tq])
                acc[c] = pv if acc[c] is None else acc[c] + pv
        if nxt is not None:
            scores(i + 1, i + 1, nxt)
        l = jnp.sum(lsum, axis=0, keepdims=True)
        o_ref[0, i * tq:(i + 1) * tq, :] = _attn_finish(acc[0], acc[1], l[:, :tq], l[:, tq:], lam, sub_col,
                                                         lam_init)
        cur = nxt


def _attn_prompt(da_lambda, subln_col, q, k, v, *, lam_init):
    bx, t, _ = q.shape
    n_tiles = t // _TK
    seq = pl.BlockSpec((1, t, LANES), lambda b, h: (b, 0, h))
    return pl.pallas_call(
        functools.partial(_attn_prompt_kernel, n_tiles=n_tiles, lam_init=lam_init),
        grid=(bx, DA_HEADS),
        in_specs=[_resident((4, DA_HEAD_DIM)), _resident((DA_V_DIM, 1)), seq, seq, seq],
        out_specs=seq,
        out_shape=jax.ShapeDtypeStruct((bx, t, DA_WIDTH), BF16),
        scratch_shapes=[pltpu.VMEM((n_tiles, DA_V_DIM, _TK), BF16),
                        pltpu.VMEM((2, n_tiles, _TK, 2 * _TK), F32)],
        compiler_params=_cparams(2),
        name="diff_attn",
    )(da_lambda, subln_col, q, k, v)


def _attn_sample_kernel(lam_ref, sub_ref, q_ref, k_ref, v_ref, kp_ref, vp_ref, o_ref, kb_s, vb_s,
                        *, tq, n_new, n_past, n_parts, n_valid_new, lam_init):
    part = pl.program_id(1)
    rows_part = n_past // n_parts
    for r0 in range(0, rows_part, _TK):
        dst = pl.ds(pl.multiple_of(part * rows_part + r0, _TK), _TK)
        kb_s[dst, :] = kp_ref[0, 0, r0:r0 + _TK].reshape(_TK, DA_WIDTH).astype(BF16)
        vb_s[dst, :] = vp_ref[0, 0, r0:r0 + _TK].reshape(_TK, DA_WIDTH).astype(BF16)

    @pl.when(part == n_parts - 1)
    def _():
        _attn_sample_heads(lam_ref, sub_ref, q_ref, k_ref, v_ref, o_ref, kb_s, vb_s, tq=tq, n_new=n_new,
                           n_past=n_past, n_valid_new=n_valid_new, lam_init=lam_init)


def _attn_sample_heads(lam_ref, sub_ref, q_ref, k_ref, v_ref, o_ref, kb_s, vb_s,
                       *, tq, n_new, n_past, n_valid_new, lam_init):
    lam = _lambda(lam_ref, lam_init)
    sub_col = sub_ref[...]
    valid = lax.broadcasted_iota(jnp.int32, (n_new, tq), 0) < n_valid_new
    kb_s[n_past:n_past + n_new, :] = k_ref[0]
    vb_s[n_past:n_past + n_new, :] = v_ref[0]
    bounds = [(r0, _TK, None) for r0 in range(0, n_past, _TK)] + [(n_past, n_new, valid)]
    for h in range(DA_HEADS):
        lanes = slice(h * LANES, (h + 1) * LANES)
        qbd = _q_blockdiag(q_ref[0, :, lanes])
        blocks = [(kb_s[r0:r0 + n, lanes], vb_s[r0:r0 + n, lanes].astype(F32).T.astype(BF16), mask)
                  for r0, n, mask in bounds]
        scores = [_dot(kblk, qbd) for kblk, _, _ in blocks]
        stats = (jnp.full((1, tq), NEG_BIG, F32), jnp.zeros((1, tq), F32)) * 2
        accs = (jnp.zeros((DA_V_DIM, tq), F32),) * 2
        for s, (_, vtblk, mask) in zip(scores, blocks):
            stats, alphas, ps = _softmax_step(s, stats, mask, tq)
            accs = tuple(alphas[c] * accs[c] + _dot(vtblk, ps[c]) for c in range(2))
        _, l1, _, l2 = stats
        o_ref[0, :, lanes] = _attn_finish(accs[0], accs[1], l1, l2, lam, sub_col, lam_init)


def _attn_sample(da_lambda, subln_col, q, k, v, past_k, past_v, *, layer, lam_init, n_valid_new):
    bx, tq, _ = q.shape
    n_new = k.shape[1]
    n_past = past_k.shape[2]
    n_parts = CACHE_PARTS
    new = pl.BlockSpec((1, n_new, DA_WIDTH), lambda b, j: (b, 0, 0))
    old = pl.BlockSpec((1, 1, n_past // n_parts, DA_HEADS, DA_V_DIM), lambda b, j: (layer, b, j, 0, 0))
    return pl.pallas_call(
        functools.partial(_attn_sample_kernel, tq=tq, n_new=n_new, n_past=n_past, n_parts=n_parts,
                          n_valid_new=n_valid_new, lam_init=lam_init),
        grid=(bx, n_parts),
        in_specs=[_resident((4, DA_HEAD_DIM)), _resident((DA_V_DIM, 1)),
                  pl.BlockSpec((1, tq, DA_WIDTH), lambda b, j: (b, 0, 0)), new, new, old, old],
        out_specs=pl.BlockSpec((1, tq, DA_WIDTH), lambda b, j: (b, 0, 0)),
        out_shape=jax.ShapeDtypeStruct((bx, tq, DA_WIDTH), BF16),
        scratch_shapes=[pltpu.VMEM((n_past + n_new, DA_WIDTH), BF16)] * 2,
        compiler_params=_cparams(2),
        name="diff_attn_step",
    )(da_lambda, subln_col, q, k, v, past_k, past_v)


def _wkv_kernel(rt_ref, at_ref, bt_ref, kt_ref, v_ref, g_ref, wc_ref, s0_ref, rk_ref, lnw_ref, lnb_ref,
                o_ref, sout_ref, s_scr, rp_s, m_s, y0_s, gc_s, bon_s, *, n_pairs, n_chunks):
    C = WKV_CHUNK
    R = 2 * C
    t = pl.program_id(2)

    @pl.when(t == 0)
    def _():
        s_scr[...] = s0_ref[0]

    ri = lax.broadcasted_iota(jnp.int32, (R, R), 0)
    ci = lax.broadcasted_iota(jnp.int32, (R, R), 1)
    sh = lambda x, sz: jnp.right_shift(x, sz.bit_length() - 1)
    same_head = sh(ri, C) == sh(ci, C)
    strict = same_head & (ci < ri)
    incl = same_head & (ci <= ri)
    eye = ri == ci

    def bd(sz):
        return sh(ri, sz) == sh(ci, sz)

    lane_lo = lax.broadcasted_iota(jnp.int32, (C, LANES), 1) < RW_HEAD_DIM

    def stack(x):
        z = jnp.zeros_like(x)
        return jnp.concatenate([jnp.where(lane_lo, x, z), jnp.where(lane_lo, z, x)], axis=0)

    b16 = lambda x: x.astype(BF16)
    f32 = lambda x: x.astype(F32)
    ident = jnp.where(eye, 1.0, 0.0)

    units = [(p, c) for p in range(n_pairs) for c in range(n_chunks)]
    idx = range(len(units))
    rs, asb, bsb, ksb, vsb, bdk, kdk, wcs = [], [], [], [], [], [], [], []
    for u, (p, c) in enumerate(units):
        rows, lanes = slice(c * C, (c + 1) * C), slice(p * LANES, (p + 1) * LANES)
        r_, a_, b_, k_, v_ = (stack(ref[0, rows, lanes]) for ref in (rt_ref, at_ref, bt_ref, kt_ref, v_ref))
        wc = wc_ref[0, c, :, lanes]
        rk = rk_ref[:, lanes]
        bon_s[u] = jnp.sum(f32(r_) * f32(k_) * rk, axis=1, keepdims=True) * f32(v_)
        rs.append(r_); asb.append(a_); bsb.append(b_); ksb.append(k_); vsb.append(v_)
        bdk.append(b16(f32(b_) * wc)); kdk.append(b16(f32(k_) * wc)); wcs.append(wc)

    n, n8, aak, arb, ark = [], [], [], [], []
    for u in idx:
        ab = _dot_nt(jnp.concatenate([asb[u], rs[u]], axis=0), jnp.concatenate([bsb[u], ksb[u]], axis=0))
        nc = jnp.where(strict, ab[:R, :R], 0.0)
        n.append(b16(nc)); n8.append(jnp.where(bd(8), nc, 0.0))
        aak.append(b16(jnp.where(strict, ab[:R, R:], 0.0)))
        arb.append(b16(jnp.where(incl, ab[R:, :R], 0.0)))
        ark.append(b16(jnp.where(incl, ab[R:, R:], 0.0)))

    n8b = [b16(x) for x in n8]
    n2 = [_dot(n8b[u], n8b[u]) for u in idx]
    n2b = [b16(x) for x in n2]
    n4 = [_dot(n2b[u], n2b[u]) for u in idx]
    p1 = [_dot(b16(ident + n8[u]), b16(ident + n2[u])) for u in idx]
    tinv = [_dot(b16(p1[u]), b16(ident + n4[u])) for u in idx]
    zero16 = jnp.zeros((R, R), BF16)
    for sz in (8, 16, 32):
        off = bd(2 * sz) & jnp.logical_not(bd(sz))
        tb = [b16(x) for x in tinv]
        x = [_dot(tb[u], jnp.where(off, n[u], zero16)) for u in idx]
        tinv = [tinv[u] + _dot(b16(x[u]), tb[u]) for u in idx]
    tb = [b16(x) for x in tinv]

    u0p = [_dot(aak[u], vsb[u]) for u in idx]
    tx = [_dot(tb[u], jnp.concatenate([asb[u], b16(u0p[u])], axis=1)) for u in idx]
    txb = [b16(x) for x in tx]
    az = [_dot(arb[u], txb[u]) for u in idx]
    for u in idx:
        rp_s[u] = b16(f32(rs[u]) + az[u][:, :LANES])
        y0_s[u] = az[u][:, LANES:] + _dot(ark[u], vsb[u])
    for u in idx:
        txt = b16(tx[u].T)
        mg = _dot(txt, bdk[u])
        m_s[u] = b16(jnp.where(eye, wcs[u], 0.0) + mg[:LANES])
        gc_s[u] = mg[LANES:] + _dot(b16(f32(vsb[u]).T), kdk[u])

    s = [s_scr[p] for p in range(n_pairs)]
    for c in range(n_chunks):
        for p in range(n_pairs):
            u = p * n_chunks + c
            rows, lanes = slice(c * C, (c + 1) * C), slice(p * LANES, (p + 1) * LANES)
            sb = b16(s[p])
            y = _dot_nt(rp_s[u], sb) + y0_s[u]
            s[p] = _dot(sb, m_s[u]) + gc_s[u]
            mu = jnp.sum(y, axis=1, keepdims=True) * (1.0 / RW_HEAD_DIM)
            d = jnp.where(same_head, y - mu, 0.0)
            var = jnp.sum(d * d, axis=1, keepdims=True) * (1.0 / RW_HEAD_DIM)
            ost = (d * lax.rsqrt(var + GN_EPS) * lnw_ref[:, lanes]
                   + jnp.where(same_head, lnb_ref[:, lanes], 0.0) + bon_s[u])
            o_ref[0, rows, lanes] = ((ost[:C] + ost[C:]) * f32(g_ref[0, rows, lanes])).astype(BF16)
    for p in range(n_pairs):
        s_scr[p] = s[p]
        sout_ref[0, p] = s[p]


def _wkv(rt, at, bt, kt, v, g, wc, s0bd, rk, lnw, lnb, *, tb, n_pairs):
    bx, t, _ = rt.shape
    n_chunks = tb // WKV_CHUNK
    n_units = n_pairs * n_chunks
    w = n_pairs * LANES
    seq = pl.BlockSpec((1, tb, w), lambda b, p, i: (b, i, p))
    vec = pl.BlockSpec((1, w), lambda b, p, i: (0, p))
    st = pl.BlockSpec((1, n_pairs, LANES, LANES), lambda b, p, i: (b, p, 0, 0))
    mat = lambda dt: pltpu.VMEM((n_units, LANES, LANES), dt)
    return pl.pallas_call(
        functools.partial(_wkv_kernel, n_pairs=n_pairs, n_chunks=n_chunks),
        grid=(bx, N_PAIRS // n_pairs, t // tb),
        in_specs=[seq] * 6 + [pl.BlockSpec((1, n_chunks, 1, w), lambda b, p, i: (b, i, 0, p)),
                              st, vec, vec, vec],
        out_specs=[seq, st],
        out_shape=[jax.ShapeDtypeStruct((bx, t, RW_WIDTH), BF16),
                   jax.ShapeDtypeStruct((bx, N_PAIRS, LANES, LANES), F32)],
        scratch_shapes=[pltpu.VMEM((n_pairs, LANES, LANES), F32),
                        mat(BF16), mat(BF16), mat(F32), mat(F32), mat(F32)],
        compiler_params=_cparams(3),
        name="wkv",
    )(rt, at, bt, kt, v, g, wc, s0bd, rk, lnw, lnb)


def _merge_kernel(x_ref, oda_ref, orw_ref, gt_ref, wda_ref, wrw_ref, wout_ref, o_ref):
    a = _dot(oda_ref[0], wda_ref[...])
    b = _dot(orw_ref[0], wrw_ref[...])
    gt = gt_ref[0].astype(F32)
    m = _sigmoid(gt[:, :D_MODEL]) * a + _sigmoid(gt[:, D_MODEL:]) * b
    o_ref[0] = x_ref[0] + _dot(m.astype(BF16), wout_ref[...])


def _merge(x, oda, orw, gt, wda, wrw, wout, *, tm):
    bx, t, _ = x.shape
    row = lambda w: pl.BlockSpec((1, tm, w), lambda b, i: (b, i, 0))
    sq = _resident((D_MODEL, D_MODEL))
    return pl.pallas_call(
        _merge_kernel,
        grid=(bx, t // tm),
        in_specs=[row(D_MODEL), row(DA_WIDTH), row(RW_WIDTH), row(P_GATE), sq, sq, sq],
        out_specs=row(D_MODEL),
        out_shape=jax.ShapeDtypeStruct((bx, t, D_MODEL), F32),
        compiler_params=_cparams(2),
        name="merge",
    )(x, oda, orw, gt, wda, wrw, wout)


_FF_CW = 256
_HALO = SUBLANES


def _ffn_kernel(*refs, tm, final, merge):
    if merge:
        x_ref, oda_ref, orw_ref, gt_ref, wda_ref, wrw_ref, wout_ref = refs[:7]
        refs = refs[7:]
        a = _dot(oda_ref[0], wda_ref[...])
        b = _dot(orw_ref[0], wrw_ref[...])
        gt = gt_ref[0].astype(F32)
        m = _sigmoid(gt[:, :D_MODEL]) * a + _sigmoid(gt[:, D_MODEL:]) * b
        x = x_ref[0] + _dot(m.astype(BF16), wout_ref[...])
    else:
        x = refs[0][0]
        refs = refs[1:]
    c0_ref, g_ref, wup_ref, f_ref, fb_ref, wdn_ref, gf_ref, o_ref, conv_ref, ext_s, act_s = refs

    @pl.when(pl.program_id(1) == 0)
    def _():
        ext_s[_HALO - 2:_HALO, :] = c0_ref[0]

    hb = _rms(x, g_ref[...]).astype(BF16)
    starts = list(range(0, D_FF, _FF_CW))

    def up_proj(c0):
        for base in (c0, D_FF + c0):
            cols = slice(base, base + _FF_CW)
            up = _dot(hb, wup_ref[:, cols])
            ext_s[_HALO:, cols] = up
            conv_ref[0, :, cols] = up[tm - 2:tm]

    def conv_act(c0):
        cs = []
        for base in (c0, D_FF + c0):
            cols = slice(base, base + _FF_CW)
            f = f_ref[:, cols]
            cs.append(fb_ref[:, cols]
                      + ext_s[_HALO - 2:_HALO - 2 + tm, cols] * f[0:1]
                      + ext_s[_HALO - 1:_HALO - 1 + tm, cols] * f[1:2]
                      + ext_s[_HALO:_HALO + tm, cols] * f[2:3])
            ext_s[_HALO - 2:_HALO, cols] = ext_s[_HALO + tm - 2:_HALO + tm, cols]
        ca, cb = cs
        act_s[:, c0:c0 + _FF_CW] = (ca * _sigmoid(ca) * cb).astype(BF16)

    up_proj(starts[0])
    for j, c0 in enumerate(starts):
        if j + 1 < len(starts):
            up_proj(starts[j + 1])
        conv_act(c0)
    out = x + _dot(act_s[...], wdn_ref[...])
    if final:
        out = _rms(out, gf_ref[...])
    o_ref[0] = out


def _ffn(x, conv0, g, wup, f, fb, wdn, gfinal, *, tm, final, merge=None):
    bx, t, _ = x.shape
    row = lambda w: pl.BlockSpec((1, tm, w), lambda b, i: (b, i, 0))
    cst = pl.BlockSpec((1, CONV_W - 1, 2 * D_FF), lambda b, i: (b, 0, 0))
    in_specs, args = [row(D_MODEL)], [x]
    if merge is not None:
        sq = _resident((D_MODEL, D_MODEL))
        in_specs += [row(DA_WIDTH), row(RW_WIDTH), row(P_GATE), sq, sq, sq]
        args += list(merge)
    in_specs += [cst, _resident((1, D_MODEL)), _resident((D_MODEL, 2 * D_FF)),
                 _resident((CONV_W, 2 * D_FF)), _resident((1, 2 * D_FF)), _resident((D_FF, D_MODEL)),
                 _resident((1, D_MODEL))]
    args += [conv0, g, wup, f, fb, wdn, gfinal]
    return pl.pallas_call(
        functools.partial(_ffn_kernel, tm=tm, final=final, merge=merge is not None),
        grid=(bx, t // tm),
        in_specs=in_specs,
        out_specs=[row(D_MODEL), cst],
        out_shape=[jax.ShapeDtypeStruct((bx, t, D_MODEL), F32),
                   jax.ShapeDtypeStruct((bx, CONV_W - 1, 2 * D_FF), F32)],
        scratch_shapes=[pltpu.VMEM((_HALO + tm, 2 * D_FF), F32), pltpu.VMEM((tm, D_FF), BF16)],
        compiler_params=_cparams(2),
        name="ffn",
    )(*args)


def _rope_tables(pos):
    half = ROT_DIM // 2
    inv = ROPE_THETA ** (-jnp.arange(0, ROT_DIM, 2, dtype=F32) / ROT_DIM)
    ang = pos.astype(F32)[:, None] * inv[None, :]
    cos, sin = jnp.cos(ang), jnp.sin(ang)
    t = pos.shape[0]
    pad = jnp.zeros((t, DA_HEAD_DIM - ROT_DIM), F32)
    z = jnp.zeros((t, half), F32)
    one_map = lambda a, b, fill: jnp.concatenate([a, b, pad + fill], axis=1)
    rc = one_map(cos, cos, 1.0)
    ra = one_map(z, sin, 0.0)
    rb = one_map(-sin, z, 0.0)
    dup = lambda m: jnp.concatenate([m, m], axis=1)
    return dup(rc), dup(ra), dup(rb)


def _state_to_blockdiag(s):
    b = s.shape[0]
    s = s.reshape(b, N_PAIRS, 2, RW_HEAD_DIM, RW_HEAD_DIM)
    z = jnp.zeros_like(s[:, :, 0])
    top = jnp.concatenate([s[:, :, 0], z], axis=-1)
    bot = jnp.concatenate([z, s[:, :, 1]], axis=-1)
    return jnp.concatenate([top, bot], axis=-2)


def _blockdiag_to_state(sbd):
    b = sbd.shape[0]
    h0 = sbd[:, :, :RW_HEAD_DIM, :RW_HEAD_DIM]
    h1 = sbd[:, :, RW_HEAD_DIM:, RW_HEAD_DIM:]
    return jnp.stack([h0, h1], axis=2).reshape(b, RW_HEADS, RW_HEAD_DIM, RW_HEAD_DIM)


def _layer(x, lidx, depth, tabs, kvbuf, cache, wkv0, shift0, conv0, w, norm_final, *, final):
    bx, t, _ = x.shape
    prompt = cache is None
    lam_init = 0.8 - 0.6 * math.exp(-0.3 * lidx)
    prep = lambda tri: (w["rw_mu"], w["rw_w0"], w["rw_w2p"], w["rw_a0"], w["rw_a2p"], w["rw_g2"],
                        w["rw_k_k"], w["rw_k_a"], w["seg"], tri)

    if prompt:
        tm = ROW_TILE
        q, k, v, kout, vout, gate, rt, at, bt, kt, vr, g, wc, new_shift = _in_proj(
            x, w["norm_mix"], w["w_in"], tabs, kvbuf, layer=lidx, depth=depth, tm=tm, fused=True,
            shift0=shift0, prep=prep(w["tri_p"]))
        o_da = _attn_prompt(w["da_lambda"], w["da_subln"], q, k, v, lam_init=lam_init)
    else:
        tm = t
        flat = lambda a: a.reshape(1, bx * t, a.shape[-1])
        q, k, v, kout, vout, gate, u_rw = _in_proj(flat(x), w["norm_mix"], w["w_in"], tabs, kvbuf,
                                                   layer=lidx, depth=depth, tm=bx * t, fused=False)
        u_rw = u_rw.reshape(bx, t, P_RW)
        new_shift = u_rw[:, -1:]
        pad_rows = lambda a, n: jnp.pad(a.reshape(bx, t, a.shape[-1]), ((0, 0), (0, n - t), (0, 0)))
        o_da = _attn_sample(w["da_lambda"], w["da_subln"], pad_rows(q, LANES), pad_rows(k, LANES),
                            pad_rows(v, LANES), *cache, layer=lidx, lam_init=lam_init, n_valid_new=t)[:, :t]
        rt, at, bt, kt, vr, g, wc = _rw_prep(u_rw, shift0, prep(w["tri_s"]), tm=t)
        rt, at, bt, kt, vr, g = [pad_rows(a, WKV_CHUNK) for a in (rt, at, bt, kt, vr, g)]
    o_rw, sbd = _wkv(rt, at, bt, kt, vr, g, wc, _state_to_blockdiag(wkv0),
                     w["rw_r_k"], w["rw_ln_w"], w["rw_ln_b"],
                     tb=WKV_ROWS if prompt else WKV_CHUNK, n_pairs=WKV_PAIRS if prompt else N_PAIRS)
    o_rw = o_rw[:, :t]

    mix = (o_da, o_rw, gate, w["w_o_da"], w["w_o_rw"], w["w_out"])
    if not prompt:
        x = _merge(flat(x), flat(o_da), flat(o_rw), *mix[2:], tm=bx * t).reshape(bx, t, D_MODEL)
        mix = None
    x, new_conv = _ffn(x, conv0, w["norm_ffn"], w["w_up"], w["ffn_conv"], w["ffn_conv_b"], w["w_down"],
                       norm_final, tm=tm, final=final, merge=mix)
    return x, (kout, vout), _blockdiag_to_state(sbd), new_shift, new_conv


def kernel(x_prompt, x_sample, cache_k, cache_v, state_wkv, state_shift, state_ffn_conv, norm_mix, w_in, da_lambda, da_subln, w_o_da, rw_mu, rw_w0, rw_w2, rw_a0, rw_a2, rw_g2, rw_k_k, rw_k_a, rw_r_k, rw_ln_w, rw_ln_b, w_o_rw, w_out, norm_ffn, w_up, ffn_conv, ffn_conv_b, w_down, norm_final):
    bp, tp, _ = x_prompt.shape
    bs, ts, _ = x_sample.shape
    depth = w_in.shape[0]
    past = cache_k.shape[2]
    tabs_p = _rope_tables(jnp.arange(tp, dtype=jnp.int32))
    tabs_s = tuple(jnp.tile(a, (bs, 1)) for a in _rope_tables(past + jnp.arange(ts, dtype=jnp.int32)))
    cache = (cache_k, cache_v)

    lane = jnp.arange(2 * LANES)
    seg = (lane[:, None] // RW_HEAD_DIM == lane[None, :] // RW_HEAD_DIM).astype(BF16)
    tri = lambda n: jnp.tile((jnp.arange(n)[:, None] >= jnp.arange(n)[None, :]).astype(BF16), (1, 3))
    tri_p, tri_s = tri(WKV_CHUNK), tri(ts)
    zl = jnp.zeros((DECAY_LORA, RW_WIDTH), F32)
    row = lambda a: a.reshape(1, -1)
    zero = lambda *s: jnp.zeros(s, x_prompt.dtype)
    wkv_z, shift_z, conv_z = zero(bp, RW_HEADS, RW_HEAD_DIM, RW_HEAD_DIM), zero(bp, 1, P_RW), zero(bp, CONV_W - 1, 2 * D_FF)

    xp, xs = x_prompt, x_sample
    kv_p = kv_s = None
    outs_p, outs_s = [], []
    nf = row(norm_final)
    for l in range(depth):
        w = dict(
            norm_mix=row(norm_mix[l]), w_in=w_in[l].astype(BF16), da_lambda=da_lambda[l],
            da_subln=da_subln[l].reshape(-1, 1), w_o_da=w_o_da[l].astype(BF16),
            rw_mu=row(rw_mu[l]), rw_w0=row(rw_w0[l]),
            rw_w2p=jnp.concatenate([rw_w2[l], zl], axis=0).astype(BF16),
            rw_a0=row(rw_a0[l]), rw_a2p=jnp.concatenate([zl, rw_a2[l]], axis=0).astype(BF16),
            rw_g2=rw_g2[l].astype(BF16), rw_k_k=row(rw_k_k[l]), rw_k_a=row(rw_k_a[l]),
            rw_r_k=row(rw_r_k[l]), rw_ln_w=row(rw_ln_w[l]), rw_ln_b=row(rw_ln_b[l]),
            w_o_rw=w_o_rw[l].astype(BF16), w_out=w_out[l].astype(BF16), norm_ffn=row(norm_ffn[l]),
            w_up=w_up[l].astype(BF16), ffn_conv=ffn_conv[l], ffn_conv_b=row(ffn_conv_b[l]),
            w_down=w_down[l].astype(BF16), seg=seg, tri_p=tri_p, tri_s=tri_s)
        final = l == depth - 1
        xp, kv_p, sp, shp, cp = _layer(xp, l, depth, tabs_p, kv_p, None, wkv_z, shift_z, conv_z, w, nf, final=final)
        xs, kv_s, sq, shq, cq = _layer(xs, l, depth, tabs_s, kv_s, cache, state_wkv[l], state_shift[l],
                                       state_ffn_conv[l], w, nf, final=final)
        outs_p.append((sp, shp, cp))
        outs_s.append((sq, shq, cq))

    wkv_prompt, shift_prompt, conv_prompt = [jnp.stack(t) for t in zip(*outs_p)]
    wkv_sample, shift_sample, conv_sample = [jnp.stack(t) for t in zip(*outs_s)]
    heads = lambda a, b, t: a.reshape(depth, b, t, DA_HEADS, DA_V_DIM)
    return (xp, xs, heads(kv_p[0], bp, tp), heads(kv_p[1], bp, tp), wkv_prompt, shift_prompt, conv_prompt,
            heads(kv_s[0], bs, ts), heads(kv_s[1], bs, ts), wkv_sample, shift_sample, conv_sample)
```

```python
import functools
import math

import jax
import jax.numpy as jnp
from jax import lax
from jax.experimental import pallas as pl
from jax.experimental.pallas import tpu as pltpu

F32 = jnp.float32
BF16 = jnp.bfloat16

D_MODEL = 1024
CHUNK = 64
EPS = 1e-6
DA_HEADS = 8
DA_HEAD_DIM = 64
DA_V_DIM = 128
DA_WIDTH = DA_HEADS * DA_V_DIM
ROPE_THETA = 500000.0
ROT_DIM = DA_HEAD_DIM // 4
RW_HEAD_DIM = 64
RW_HEADS = 16
RW_WIDTH = 1024
DECAY_LORA = 64
AAA_LORA = 64
GATE_LORA = 128
GN_EPS = 64e-5
D_FF = 2816
CONV_W = 3
P_DA = 3 * DA_WIDTH
P_RW = 3 * RW_WIDTH + DECAY_LORA + AAA_LORA + GATE_LORA
P_GATE = 2 * D_MODEL
P_TOTAL = P_DA + P_RW + P_GATE

LANES = 128
SUBLANES = 8
VMEM_LIMIT = 56 * 1024 * 1024

WKV_CHUNK = 64
N_PAIRS = RW_HEADS // 2
NEG_BIG = -1e30
ROW_TILE = 256
WKV_ROWS = 512
WKV_PAIRS = 4
CACHE_PARTS = 2

_ARB = "arbitrary"


def _cparams(n_axes):
    return pltpu.CompilerParams(dimension_semantics=(_ARB,) * n_axes,
                                vmem_limit_bytes=VMEM_LIMIT)


def _resident(shape):
    nd = len(shape)
    return pl.BlockSpec(shape, lambda *_: (0,) * nd, pipeline_mode=pl.Buffered(1))


def _layer_weight(shape, layer):
    nd = len(shape)
    return pl.BlockSpec((1,) + tuple(shape), lambda *_: (layer,) + (0,) * nd, pipeline_mode=pl.Buffered(1))


def _dot(a, b):
    return jnp.dot(a, b, preferred_element_type=F32)


def _dot_nt(a, b):
    return lax.dot_general(a, b, (((1,), (1,)), ((), ())), preferred_element_type=F32)


def _rms(x, g):
    return x * lax.rsqrt(jnp.mean(x * x, axis=-1, keepdims=True) + EPS) * g


def _sigmoid(x):
    return 1.0 / (1.0 + jnp.exp(-x))


def _split3(x):
    hi = x.astype(BF16)
    r1 = x - hi.astype(F32)
    mid = r1.astype(BF16)
    lo = (r1 - mid.astype(F32)).astype(BF16)
    return hi, mid, lo


def _prep_rows(us, w0_ref, w2_ref, a0_ref, a2_ref, g2_ref, kk_ref, ka_ref, seg_ref):
    r = us[:, 0:RW_WIDTH]
    kr = us[:, RW_WIDTH:2 * RW_WIDTH]
    vr = us[:, 2 * RW_WIDTH:3 * RW_WIDTH]
    wa = us[:, 3 * RW_WIDTH:3 * RW_WIDTH + DECAY_LORA + AAA_LORA]
    gd = us[:, 3 * RW_WIDTH + DECAY_LORA + AAA_LORA:]

    z = w0_ref[...] + _dot(jnp.tanh(wa).astype(BF16), w2_ref[...])
    lw = -math.exp(-0.5) * _sigmoid(z)
    a = _sigmoid(a0_ref[...] + _dot(wa.astype(BF16), a2_ref[...]))
    g = _dot(_sigmoid(gd).astype(BF16), g2_ref[...])

    kk = kr * kk_ref[...]
    k2 = (kk * kk).astype(BF16)
    seg = seg_ref[...]
    w = seg.shape[0]
    ssum = jnp.concatenate([_dot(k2[:, c:c + w], seg) for c in range(0, RW_WIDTH, w)],
                           axis=1)
    kk = kk * lax.rsqrt(jnp.maximum(ssum, 1e-24))
    kmod = kr * (1.0 + (a - 1.0) * ka_ref[...])
    return r, vr, g, lw, a, kk, kmod


def _prep_chunk(r, lw, a, kk, kmod, tri):
    cum = _dot(tri, jnp.concatenate(_split3(lw), axis=0))
    e_pos = jnp.exp(cum)
    e_neg = jnp.exp(-cum)
    rt = r * e_pos
    at = -kk * jnp.exp(cum - lw)
    bt = kk * a * e_neg
    kt = kmod * e_neg
    return rt, at, bt, kt, e_pos[-1:, :]


def _rw_prep_kernel(u_ref, sh_ref, mu_ref, w0_ref, w2_ref, a0_ref, a2_ref, g2_ref,
                    kk_ref, ka_ref, seg_ref, tri_ref,
                    rt_ref, at_ref, bt_ref, kt_ref, v_ref, g_ref, wc_ref, *, tm):
    u = u_ref[0]
    rows = lax.broadcasted_iota(jnp.int32, (tm, 1), 0)
    up = jnp.where(rows == 0, sh_ref[0], pltpu.roll(u, 1, 0))
    us = u + (up - u) * mu_ref[...]
    r, vr, g, lw, a, kk, kmod = _prep_rows(us, w0_ref, w2_ref, a0_ref, a2_ref, g2_ref, kk_ref, ka_ref, seg_ref)
    rt, at, bt, kt, wc = _prep_chunk(r, lw, a, kk, kmod, tri_ref[...])
    for ref, val in ((rt_ref, rt), (at_ref, at), (bt_ref, bt), (kt_ref, kt), (v_ref, vr), (g_ref, g)):
        ref[0] = val.astype(BF16)
    wc_ref[0, 0] = wc


def _prep_param_specs(chunk):
    return [_resident((1, P_RW)), _resident((1, RW_WIDTH)),
            _resident((DECAY_LORA + AAA_LORA, RW_WIDTH)), _resident((1, RW_WIDTH)),
            _resident((DECAY_LORA + AAA_LORA, RW_WIDTH)), _resident((GATE_LORA, RW_WIDTH)),
            _resident((1, RW_WIDTH)), _resident((1, RW_WIDTH)),
            _resident((2 * LANES, 2 * LANES)), _resident((chunk, 3 * chunk))]


def _rw_prep(u, shift0, prep, *, tm):
    bx, t, _ = u.shape
    assert t == tm
    row = lambda w: pl.BlockSpec((1, tm, w), lambda b: (b, 0, 0))
    outs = [jax.ShapeDtypeStruct((bx, t, RW_WIDTH), BF16)] * 6
    outs.append(jax.ShapeDtypeStruct((bx, 1, 1, RW_WIDTH), F32))
    return pl.pallas_call(
        functools.partial(_rw_prep_kernel, tm=tm),
        grid=(bx,),
        in_specs=[row(P_RW), pl.BlockSpec((1, 1, P_RW), lambda b: (b, 0, 0))] + _prep_param_specs(tm),
        out_specs=[row(RW_WIDTH)] * 6 + [pl.BlockSpec((1, 1, 1, RW_WIDTH), lambda b: (b, 0, 0, 0))],
        out_shape=outs,
        compiler_params=_cparams(1),
        name="rw_prep",
    )(u, shift0, *prep)


_IN_CW = 256
_LOG2E = math.log2(math.e)
_N_PREP = 10


def _in_proj_kernel(*refs, tm, fused, aliased):
    x_ref, g_ref, w_ref, rc_ref, ra_ref, rb_ref = refs[:6]
    w_ref = w_ref.at[0]
    n_in = 6
    if fused:
        sh_ref = refs[6]
        mu_ref, w0_ref, w2_ref, a0_ref, a2_ref, g2_ref, kk_ref, ka_ref, seg_ref, tri_ref = refs[7:7 + _N_PREP]
        n_in = 7 + _N_PREP
    n_in += 2 if aliased else 0
    q_ref, kn_ref, vn_ref, ko_ref, vo_ref, gt_ref = refs[n_in:n_in + 6]
    if fused:
        rt_ref, at_ref, bt_ref, kt_ref, vr_ref, gr_ref, wc_ref, last_ref = refs[n_in + 6:n_in + 14]
        k_scr, v_scr, u_s = refs[n_in + 14:]
    else:
        u_ref = refs[n_in + 6]
        k_scr, v_scr = refs[n_in + 7:]
    hb = _rms(x_ref[0], g_ref[...]).astype(BF16)
    rc = rc_ref[...]
    ra = ra_ref[...]
    rb = rb_ref[...]

    def rope(t):
        return t * rc + pltpu.roll(t, ROT_DIM // 2, 1) * ra + pltpu.roll(t, LANES - ROT_DIM // 2, 1) * rb

    def chunk(c0):
        if P_DA <= c0 < P_DA + P_RW:
            o = c0 - P_DA
            if fused:
                u_s[SUBLANES:, o:o + _IN_CW] = _dot(hb, w_ref[:, c0:c0 + _IN_CW])
            else:
                u_ref[0, :, o:o + _IN_CW] = _dot(hb, w_ref[:, c0:c0 + _IN_CW])
            return
        acc = _dot(hb, w_ref[:, c0:c0 + _IN_CW])
        if c0 < DA_WIDTH:
            for s in range(0, _IN_CW, LANES):
                q_ref[0, :, c0 + s:c0 + s + LANES] = (
                    rope(acc[:, s:s + LANES]) * (DA_HEAD_DIM ** -0.5 * _LOG2E)).astype(BF16)
        elif c0 < 2 * DA_WIDTH:
            o = c0 - DA_WIDTH
            for s in range(0, _IN_CW, LANES):
                kr = rope(acc[:, s:s + LANES])
                k_scr[:, o + s:o + s + LANES] = kr
                kn_ref[0, :, o + s:o + s + LANES] = kr.astype(BF16)
        elif c0 < P_DA:
            o = c0 - 2 * DA_WIDTH
            v_scr[:, o:o + _IN_CW] = acc
            vn_ref[0, :, o:o + _IN_CW] = acc.astype(BF16)
        else:
            o = c0 - P_DA - P_RW
            gt_ref[0, :, o:o + _IN_CW] = acc.astype(BF16)

    def chunks(lo, hi):
        for c0 in range(lo, hi, _IN_CW):
            chunk(c0)

    if not fused:
        chunks(0, P_TOTAL)
    else:
        @pl.when(pl.program_id(1) == 0)
        def _():
            u_s[SUBLANES - 1:SUBLANES, :] = sh_ref[0]

        chunks(0, P_DA + P_RW)
        u = u_s[SUBLANES:SUBLANES + tm, :]
        up = u_s[SUBLANES - 1:SUBLANES - 1 + tm, :]
        us = u + (up - u) * mu_ref[...]
        last = u[tm - 1:tm, :]
        last_ref[0] = last
        u_s[SUBLANES - 1:SUBLANES, :] = last
        r, vr, g, lw, a, kk, kmod = _prep_rows(us, w0_ref, w2_ref, a0_ref, a2_ref, g2_ref, kk_ref, ka_ref, seg_ref)
        vr_ref[0] = vr.astype(BF16)
        gr_ref[0] = g.astype(BF16)
        chunks(P_DA + P_RW, P_DA + P_RW + P_GATE // 2)
        tri = tri_ref[...]
        for c in range(tm // WKV_CHUNK):
            sl = slice(c * WKV_CHUNK, (c + 1) * WKV_CHUNK)
            rt, at, bt, kt, wc = _prep_chunk(r[sl], lw[sl], a[sl], kk[sl], kmod[sl], tri)
            for ref, val in ((rt_ref, rt), (at_ref, at), (bt_ref, bt), (kt_ref, kt)):
                ref[0, sl, :] = val.astype(BF16)
            wc_ref[0, c] = wc
        chunks(P_DA + P_RW + P_GATE // 2, P_TOTAL)
    ko_ref[0, 0] = k_scr[...].reshape(tm, DA_HEADS, DA_V_DIM)
    vo_ref[0, 0] = v_scr[...].reshape(tm, DA_HEADS, DA_V_DIM)


def _in_proj(x, g, w_bf, tabs, kvout, *, layer, depth, tm, fused, shift0=None, prep=None):
    bx, t, _ = x.shape
    row = lambda w: pl.BlockSpec((1, tm, w), lambda b, i: (b, i, 0))
    tab = pl.BlockSpec((tm, LANES), lambda b, i: (i, 0))
    slot = pl.BlockSpec((1, 1, tm, DA_HEADS, DA_V_DIM), lambda b, i: (layer, b, i, 0, 0))
    aliased = kvout is not None
    in_specs = [row(D_MODEL), _resident((1, D_MODEL)), _layer_weight((D_MODEL, P_TOTAL), layer), tab, tab, tab]
    args = [x, g, w_bf, *tabs]
    if fused:
        in_specs += [pl.BlockSpec((1, 1, P_RW), lambda b, i: (b, 0, 0))] + _prep_param_specs(WKV_CHUNK)
        args += [shift0, *prep]
    if aliased:
        aliases = {len(args): 3, len(args) + 1: 4}
        in_specs += [pl.BlockSpec(memory_space=pl.ANY)] * 2
        args += list(kvout)
    else:
        aliases = {}
    nat = jax.ShapeDtypeStruct((bx, t, DA_WIDTH), BF16)
    stacked = jax.ShapeDtypeStruct((depth, bx, t, DA_HEADS, DA_V_DIM), F32)
    out_specs = [row(DA_WIDTH)] * 3 + [slot, slot, row(P_GATE)]
    out_shape = [nat, nat, nat, stacked, stacked, jax.ShapeDtypeStruct((bx, t, P_GATE), BF16)]
    scratch = [pltpu.VMEM((tm, DA_WIDTH), F32)] * 2
    if fused:
        nc = tm // WKV_CHUNK
        out_specs += [row(RW_WIDTH)] * 6 + [pl.BlockSpec((1, nc, 1, RW_WIDTH), lambda b, i: (b, i, 0, 0)),
                                            pl.BlockSpec((1, 1, P_RW), lambda b, i: (b, 0, 0))]
        out_shape += [jax.ShapeDtypeStruct((bx, t, RW_WIDTH), BF16)] * 6
        out_shape += [jax.ShapeDtypeStruct((bx, t // WKV_CHUNK, 1, RW_WIDTH), F32),
                      jax.ShapeDtypeStruct((bx, 1, P_RW), F32)]
        scratch.append(pltpu.VMEM((SUBLANES + tm, P_RW), F32))
    else:
        out_specs.append(row(P_RW))
        out_shape.append(jax.ShapeDtypeStruct((bx, t, P_RW), F32))
    return pl.pallas_call(
        functools.partial(_in_proj_kernel, tm=tm, fused=fused, aliased=aliased),
        grid=(bx, t // tm),
        in_specs=in_specs,
        out_specs=out_specs,
        out_shape=out_shape,
        scratch_shapes=scratch,
        input_output_aliases=aliases,
        compiler_params=_cparams(2),
        name="in_proj",
    )(*args)


_TK = 256
_CHUNK_SHIFT = CHUNK.bit_length() - 1


def _lambda(lam_ref, lam_init):
    lp = lam_ref[...]
    return (jnp.exp(jnp.sum(lp[0:1] * lp[1:2], axis=1, keepdims=True))
            - jnp.exp(jnp.sum(lp[2:3] * lp[3:4], axis=1, keepdims=True)) + lam_init)


def _q_blockdiag(q_tile):
    tq = q_tile.shape[0]
    qt = q_tile.astype(F32).T.astype(BF16)
    z = jnp.zeros((DA_HEAD_DIM, tq), BF16)
    return jnp.concatenate([jnp.concatenate([qt[:DA_HEAD_DIM], z], axis=1),
                            jnp.concatenate([z, qt[DA_HEAD_DIM:]], axis=1)], axis=0)


def _softmax_step(s, stats, mask, tq):
    new_stats, alphas, ps = [], [], []
    for c in range(2):
        m, l = stats[2 * c:2 * c + 2]
        sc = s[:, c * tq:(c + 1) * tq]
        if mask is not None:
            sc = jnp.where(mask, sc, NEG_BIG)
        m_new = jnp.maximum(m, jnp.max(sc, axis=0, keepdims=True))
        alpha = jnp.exp2(m - m_new)
        p = jnp.exp2(sc - m_new)
        new_stats += [m_new, alpha * l + jnp.sum(p, axis=0, keepdims=True)]
        alphas.append(alpha)
        ps.append(p.astype(BF16))
    return tuple(new_stats), tuple(alphas), tuple(ps)


def _attn_finish(a1, a2, l1, l2, lam, sub_col, lam_init):
    ot = a1 * (1.0 / l1) - lam * (a2 * (1.0 / l2))
    ss = jnp.sum(ot * ot, axis=0, keepdims=True) * (1.0 / DA_V_DIM)
    ot = ot * lax.rsqrt(ss + EPS) * sub_col * (1.0 - lam_init)
    return ot.T.astype(BF16)


def _attn_prompt_kernel(lam_ref, sub_ref, q_ref, k_ref, v_ref, o_ref, vt_s, s_s, *, n_tiles, lam_init):
    tq = _TK
    for jb in range(n_tiles):
        vt_s[jb] = v_ref[0, jb * _TK:(jb + 1) * _TK, :].astype(F32).T.astype(BF16)
    lam = _lambda(lam_ref, lam_init)
    sub_col = sub_ref[...]
    kr = jnp.right_shift(lax.broadcasted_iota(jnp.int32, (_TK, 2 * tq), 0), _CHUNK_SHIFT)
    qc = jnp.right_shift(lax.broadcasted_iota(jnp.int32, (_TK, 2 * tq), 1) & (tq - 1), _CHUNK_SHIFT)
    diag_mask = kr <= qc

    def new_tile(i):
        return dict(qbd=_q_blockdiag(q_ref[0, i * tq:(i + 1) * tq, :]), mx=None)

    def scores(i, j, tile):
        s = _dot(k_ref[0, j * _TK:(j + 1) * _TK, :], tile["qbd"])
        if j == i:
            s = jnp.where(diag_mask, s, NEG_BIG)
        s_s[i % 2, j] = s
        part = jnp.max(s.reshape(_TK // SUBLANES, SUBLANES, 2 * tq), axis=0)
        tile["mx"] = part if tile["mx"] is None else jnp.maximum(tile["mx"], part)

    cur = new_tile(0)
    scores(0, 0, cur)
    for i in range(n_tiles):
        m = jnp.max(cur["mx"], axis=0, keepdims=True)
        nxt = new_tile(i + 1) if i + 1 < n_tiles else None
        acc, lsum, pending = [None, None], None, None

        def values(jp, pb):
            for c in range(2):
                pv = _dot(vt_s[jp], pb[:, c * tq:(c + 1) * tq])
                acc[c] = pv if acc[c] is None else acc[c] + pv

        for j in range(i + 1):
            if nxt is not None:
                scores(i + 1, j, nxt)
            p = jnp.exp2(s_s[i % 2, j] - m)
            part = jnp.sum(p.reshape(_TK // SUBLANES, SUBLANES, 2 * tq), axis=0)
            lsum = part if lsum is None else lsum + part
            if pending is not None:
                values(*pending)
            pending = (j, p.astype(BF16))
        if nxt is not None:
            scores(i + 1, i + 1, nxt)
        values(*pending)
        l = jnp.sum(lsum, axis=0, keepdims=True)
        o_ref[0, i * tq:(i + 1) * tq, :] = _attn_finish(acc[0], acc[1], l[:, :tq], l[:, tq:], lam, sub_col,
                                                         lam_init)
        cur = nxt


def _attn_prompt(da_lambda, subln_col, q, k, v, *, lam_init):
    bx, t, _ = q.shape
    n_tiles = t // _TK
    seq = pl.BlockSpec((1, t, LANES), lambda b, h: (b, 0, h))
    return pl.pallas_call(
        functools.partial(_attn_prompt_kernel, n_tiles=n_tiles, lam_init=lam_init),
        grid=(bx, DA_HEADS),
        in_specs=[_resident((4, DA_HEAD_DIM)), _resident((DA_V_DIM, 1)), seq, seq, seq],
        out_specs=seq,
        out_shape=jax.ShapeDtypeStruct((bx, t, DA_WIDTH), BF16),
        scratch_shapes=[pltpu.VMEM((n_tiles, DA_V_DIM, _TK), BF16),
                        pltpu.VMEM((2, n_tiles, _TK, 2 * _TK), F32)],
        compiler_params=_cparams(2),
        name="diff_attn",
    )(da_lambda, subln_col, q, k, v)


def _attn_sample_kernel(lam_ref, sub_ref, q_ref, k_ref, v_ref, kp_ref, vp_ref, o_ref, kb_s, vb_s,
                        *, tq, n_new, n_past, n_parts, n_valid_new, lam_init):
    part = pl.program_id(1)
    rows_part = n_past // n_parts
    for r0 in range(0, rows_part, _TK):
        dst = pl.ds(pl.multiple_of(part * rows_part + r0, _TK), _TK)
        kb_s[dst, :] = kp_ref[0, 0, r0:r0 + _TK].reshape(_TK, DA_WIDTH).astype(BF16)
        vb_s[dst, :] = vp_ref[0, 0, r0:r0 + _TK].reshape(_TK, DA_WIDTH).astype(BF16)

    @pl.when(part == n_parts - 1)
    def _():
        _attn_sample_heads(lam_ref, sub_ref, q_ref, k_ref, v_ref, o_ref, kb_s, vb_s, tq=tq, n_new=n_new,
                           n_past=n_past, n_valid_new=n_valid_new, lam_init=lam_init)


def _attn_sample_heads(lam_ref, sub_ref, q_ref, k_ref, v_ref, o_ref, kb_s, vb_s,
                       *, tq, n_new, n_past, n_valid_new, lam_init):
    lam = _lambda(lam_ref, lam_init)
    sub_col = sub_ref[...]
    valid = lax.broadcasted_iota(jnp.int32, (n_new, tq), 0) < n_valid_new
    kb_s[n_past:n_past + n_new, :] = k_ref[0]
    vb_s[n_past:n_past + n_new, :] = v_ref[0]
    bounds = [(r0, _TK, None) for r0 in range(0, n_past, _TK)] + [(n_past, n_new, valid)]
    for h in range(DA_HEADS):
        lanes = slice(h * LANES, (h + 1) * LANES)
        qbd = _q_blockdiag(q_ref[0, :, lanes])
        blocks = [(kb_s[r0:r0 + n, lanes], vb_s[r0:r0 + n, lanes].astype(F32).T.astype(BF16), mask)
                  for r0, n, mask in bounds]
        scores = [_dot(kblk, qbd) for kblk, _, _ in blocks]
        stats = (jnp.full((1, tq), NEG_BIG, F32), jnp.zeros((1, tq), F32)) * 2
        accs = (jnp.zeros((DA_V_DIM, tq), F32),) * 2
        for s, (_, vtblk, mask) in zip(scores, blocks):
            stats, alphas, ps = _softmax_step(s, stats, mask, tq)
            accs = tuple(alphas[c] * accs[c] + _dot(vtblk, ps[c]) for c in range(2))
        _, l1, _, l2 = stats
        o_ref[0, :, lanes] = _attn_finish(accs[0], accs[1], l1, l2, lam, sub_col, lam_init)


def _attn_sample(da_lambda, subln_col, q, k, v, past_k, past_v, *, layer, lam_init, n_valid_new):
    bx, tq, _ = q.shape
    n_new = k.shape[1]
    n_past = past_k.shape[2]
    n_parts = CACHE_PARTS
    new = pl.BlockSpec((1, n_new, DA_WIDTH), lambda b, j: (b, 0, 0))
    old = pl.BlockSpec((1, 1, n_past // n_parts, DA_HEADS, DA_V_DIM), lambda b, j: (layer, b, j, 0, 0))
    return pl.pallas_call(
        functools.partial(_attn_sample_kernel, tq=tq, n_new=n_new, n_past=n_past, n_parts=n_parts,
                          n_valid_new=n_valid_new, lam_init=lam_init),
        grid=(bx, n_parts),
        in_specs=[_resident((4, DA_HEAD_DIM)), _resident((DA_V_DIM, 1)),
                  pl.BlockSpec((1, tq, DA_WIDTH), lambda b, j: (b, 0, 0)), new, new, old, old],
        out_specs=pl.BlockSpec((1, tq, DA_WIDTH), lambda b, j: (b, 0, 0)),
        out_shape=jax.ShapeDtypeStruct((bx, tq, DA_WIDTH), BF16),
        scratch_shapes=[pltpu.VMEM((n_past + n_new, DA_WIDTH), BF16)] * 2,
        compiler_params=_cparams(2),
        name="diff_attn_step",
    )(da_lambda, subln_col, q, k, v, past_k, past_v)


def _wkv_kernel(rt_ref, at_ref, bt_ref, kt_ref, v_ref, g_ref, wc_ref, s0_ref, rk_ref, lnw_ref, lnb_ref,
                o_ref, sout_ref, s_scr, rp_s, m_s, y0_s, gc_s, bon_s, *, n_pairs, n_chunks):
    C = WKV_CHUNK
    R = 2 * C
    t = pl.program_id(2)

    @pl.when(t == 0)
    def _():
        s_scr[...] = s0_ref[0]

    ri = lax.broadcasted_iota(jnp.int32, (R, R), 0)
    ci = lax.broadcasted_iota(jnp.int32, (R, R), 1)
    sh = lambda x, sz: jnp.right_shift(x, sz.bit_length() - 1)
    same_head = sh(ri, C) == sh(ci, C)
    strict = same_head & (ci < ri)
    incl = same_head & (ci <= ri)
    eye = ri == ci

    def bd(sz):
        return sh(ri, sz) == sh(ci, sz)

    lane_lo = lax.broadcasted_iota(jnp.int32, (C, LANES), 1) < RW_HEAD_DIM

    def stack(x):
        z = jnp.zeros_like(x)
        return jnp.concatenate([jnp.where(lane_lo, x, z), jnp.where(lane_lo, z, x)], axis=0)

    b16 = lambda x: x.astype(BF16)
    f32 = lambda x: x.astype(F32)
    ident = jnp.where(eye, 1.0, 0.0)

    units = [(p, c) for p in range(n_pairs) for c in range(n_chunks)]
    idx = range(len(units))
    rs, asb, bsb, ksb, vsb, bdk, kdk, wcs = [], [], [], [], [], [], [], []
    for u, (p, c) in enumerate(units):
        rows, lanes = slice(c * C, (c + 1) * C), slice(p * LANES, (p + 1) * LANES)
        r_, a_, b_, k_, v_ = (stack(ref[0, rows, lanes]) for ref in (rt_ref, at_ref, bt_ref, kt_ref, v_ref))
        wc = wc_ref[0, c, :, lanes]
        rk = rk_ref[:, lanes]
        bon_s[u] = jnp.sum(f32(r_) * f32(k_) * rk, axis=1, keepdims=True) * f32(v_)
        rs.append(r_); asb.append(a_); bsb.append(b_); ksb.append(k_); vsb.append(v_)
        bdk.append(b16(f32(b_) * wc)); kdk.append(b16(f32(k_) * wc)); wcs.append(wc)

    n, n8, aak, arb, ark = [], [], [], [], []
    for u in idx:
        ab = _dot_nt(jnp.concatenate([asb[u], rs[u]], axis=0), jnp.concatenate([bsb[u], ksb[u]], axis=0))
        nc = jnp.where(strict, ab[:R, :R], 0.0)
        n.append(b16(nc)); n8.append(jnp.where(bd(8), nc, 0.0))
        aak.append(b16(jnp.where(strict, ab[:R, R:], 0.0)))
        arb.append(b16(jnp.where(incl, ab[R:, :R], 0.0)))
        ark.append(b16(jnp.where(incl, ab[R:, R:], 0.0)))

    n8b = [b16(x) for x in n8]
    n2 = [_dot(n8b[u], n8b[u]) for u in idx]
    n2b = [b16(x) for x in n2]
    n4 = [_dot(n2b[u], n2b[u]) for u in idx]
    p1 = [_dot(b16(ident + n8[u]), b16(ident + n2[u])) for u in idx]
    tinv = [_dot(b16(p1[u]), b16(ident + n4[u])) for u in idx]
    zero16 = jnp.zeros((R, R), BF16)
    for sz in (8, 16, 32):
        off = bd(2 * sz) & jnp.logical_not(bd(sz))
        tb = [b16(x) for x in tinv]
        x = [_dot(tb[u], jnp.where(off, n[u], zero16)) for u in idx]
        tinv = [tinv[u] + _dot(b16(x[u]), tb[u]) for u in idx]
    tb = [b16(x) for x in tinv]

    u0p = [_dot(aak[u], vsb[u]) for u in idx]
    tx = [_dot(tb[u], jnp.concatenate([asb[u], b16(u0p[u])], axis=1)) for u in idx]
    txb = [b16(x) for x in tx]
    az = [_dot(arb[u], txb[u]) for u in idx]
    for u in idx:
        rp_s[u] = b16(f32(rs[u]) + az[u][:, :LANES])
        y0_s[u] = az[u][:, LANES:] + _dot(ark[u], vsb[u])
    for u in idx:
        txt = b16(tx[u].T)
        mg = _dot(txt, bdk[u])
        m_s[u] = b16(jnp.where(eye, wcs[u], 0.0) + mg[:LANES])
        gc_s[u] = mg[LANES:] + _dot(b16(f32(vsb[u]).T), kdk[u])

    s = [s_scr[p] for p in range(n_pairs)]
    for c in range(n_chunks):
        for p in range(n_pairs):
            u = p * n_chunks + c
            rows, lanes = slice(c * C, (c + 1) * C), slice(p * LANES, (p + 1) * LANES)
            sb = b16(s[p])
            y = _dot_nt(rp_s[u], sb) + y0_s[u]
            s[p] = _dot(sb, m_s[u]) + gc_s[u]
            mu = jnp.sum(y, axis=1, keepdims=True) * (1.0 / RW_HEAD_DIM)
            d = jnp.where(same_head, y - mu, 0.0)
            var = jnp.sum(d * d, axis=1, keepdims=True) * (1.0 / RW_HEAD_DIM)
            ost = (d * lax.rsqrt(var + GN_EPS) * lnw_ref[:, lanes]
                   + jnp.where(same_head, lnb_ref[:, lanes], 0.0) + bon_s[u])
            o_ref[0, rows, lanes] = ((ost[:C] + ost[C:]) * f32(g_ref[0, rows, lanes])).astype(BF16)
    for p in range(n_pairs):
        s_scr[p] = s[p]
        sout_ref[0, p] = s[p]


def _wkv(rt, at, bt, kt, v, g, wc, s0bd, rk, lnw, lnb, *, tb, n_pairs):
    bx, t, _ = rt.shape
    n_chunks = tb // WKV_CHUNK
    n_units = n_pairs * n_chunks
    w = n_pairs * LANES
    seq = pl.BlockSpec((1, tb, w), lambda b, p, i: (b, i, p))
    vec = pl.BlockSpec((1, w), lambda b, p, i: (0, p))
    st = pl.BlockSpec((1, n_pairs, LANES, LANES), lambda b, p, i: (b, p, 0, 0))
    mat = lambda dt: pltpu.VMEM((n_units, LANES, LANES), dt)
    return pl.pallas_call(
        functools.partial(_wkv_kernel, n_pairs=n_pairs, n_chunks=n_chunks),
        grid=(bx, N_PAIRS // n_pairs, t // tb),
        in_specs=[seq] * 6 + [pl.BlockSpec((1, n_chunks, 1, w), lambda b, p, i: (b, i, 0, p)),
                              st, vec, vec, vec],
        out_specs=[seq, st],
        out_shape=[jax.ShapeDtypeStruct((bx, t, RW_WIDTH), BF16),
                   jax.ShapeDtypeStruct((bx, N_PAIRS, LANES, LANES), F32)],
        scratch_shapes=[pltpu.VMEM((n_pairs, LANES, LANES), F32),
                        mat(BF16), mat(BF16), mat(F32), mat(F32), mat(F32)],
        compiler_params=_cparams(3),
        name="wkv",
    )(rt, at, bt, kt, v, g, wc, s0bd, rk, lnw, lnb)


def _merge_kernel(x_ref, oda_ref, orw_ref, gt_ref, wda_ref, wrw_ref, wout_ref, o_ref):
    a = _dot(oda_ref[0], wda_ref[0])
    b = _dot(orw_ref[0], wrw_ref[0])
    gt = gt_ref[0].astype(F32)
    m = _sigmoid(gt[:, :D_MODEL]) * a + _sigmoid(gt[:, D_MODEL:]) * b
    o_ref[0] = x_ref[0] + _dot(m.astype(BF16), wout_ref[0])


def _merge(x, oda, orw, gt, wda, wrw, wout, *, tm, layer):
    bx, t, _ = x.shape
    row = lambda w: pl.BlockSpec((1, tm, w), lambda b, i: (b, i, 0))
    sq = _layer_weight((D_MODEL, D_MODEL), layer)
    return pl.pallas_call(
        _merge_kernel,
        grid=(bx, t // tm),
        in_specs=[row(D_MODEL), row(DA_WIDTH), row(RW_WIDTH), row(P_GATE), sq, sq, sq],
        out_specs=row(D_MODEL),
        out_shape=jax.ShapeDtypeStruct((bx, t, D_MODEL), F32),
        compiler_params=_cparams(2),
        name="merge",
    )(x, oda, orw, gt, wda, wrw, wout)


_FF_CW = 256
_HALO = SUBLANES


def _ffn_kernel(*refs, tm, final, merge):
    if merge:
        x_ref, oda_ref, orw_ref, gt_ref, wda_ref, wrw_ref, wout_ref = refs[:7]
        refs = refs[7:]
        a = _dot(oda_ref[0], wda_ref[0])
        b = _dot(orw_ref[0], wrw_ref[0])
        gt = gt_ref[0].astype(F32)
        m = _sigmoid(gt[:, :D_MODEL]) * a + _sigmoid(gt[:, D_MODEL:]) * b
        x = x_ref[0] + _dot(m.astype(BF16), wout_ref[0])
    else:
        x = refs[0][0]
        refs = refs[1:]
    c0_ref, g_ref, wup_ref, f_ref, fb_ref, wdn_ref, gf_ref, o_ref, conv_ref, ext_s, act_s = refs
    wup_ref, wdn_ref = wup_ref.at[0], wdn_ref.at[0]

    @pl.when(pl.program_id(1) == 0)
    def _():
        ext_s[_HALO - 2:_HALO, :] = c0_ref[0]

    hb = _rms(x, g_ref[...]).astype(BF16)
    starts = list(range(0, D_FF, _FF_CW))

    def up_proj(c0):
        for base in (c0, D_FF + c0):
            cols = slice(base, base + _FF_CW)
            up = _dot(hb, wup_ref[:, cols])
            ext_s[_HALO:, cols] = up
            conv_ref[0, :, cols] = up[tm - 2:tm]

    def conv_act(c0):
        cs = []
        for base in (c0, D_FF + c0):
            cols = slice(base, base + _FF_CW)
            f = f_ref[:, cols]
            cs.append(fb_ref[:, cols]
                      + ext_s[_HALO - 2:_HALO - 2 + tm, cols] * f[0:1]
                      + ext_s[_HALO - 1:_HALO - 1 + tm, cols] * f[1:2]
                      + ext_s[_HALO:_HALO + tm, cols] * f[2:3])
            ext_s[_HALO - 2:_HALO, cols] = ext_s[_HALO + tm - 2:_HALO + tm, cols]
        ca, cb = cs
        act_s[:, c0:c0 + _FF_CW] = (ca * _sigmoid(ca) * cb).astype(BF16)

    up_proj(starts[0])
    for j, c0 in enumerate(starts):
        if j + 1 < len(starts):
            up_proj(starts[j + 1])
        conv_act(c0)
    out = x + _dot(act_s[...], wdn_ref[...])
    if final:
        out = _rms(out, gf_ref[...])
    o_ref[0] = out


def _ffn(x, conv0, g, wup, f, fb, wdn, gfinal, *, tm, final, layer, merge=None):
    bx, t, _ = x.shape
    row = lambda w: pl.BlockSpec((1, tm, w), lambda b, i: (b, i, 0))
    cst = pl.BlockSpec((1, CONV_W - 1, 2 * D_FF), lambda b, i: (b, 0, 0))
    in_specs, args = [row(D_MODEL)], [x]
    if merge is not None:
        sq = _layer_weight((D_MODEL, D_MODEL), layer)
        in_specs += [row(DA_WIDTH), row(RW_WIDTH), row(P_GATE), sq, sq, sq]
        args += list(merge)
    in_specs += [cst, _resident((1, D_MODEL)), _layer_weight((D_MODEL, 2 * D_FF), layer),
                 _resident((CONV_W, 2 * D_FF)), _resident((1, 2 * D_FF)), _layer_weight((D_FF, D_MODEL), layer),
                 _resident((1, D_MODEL))]
    args += [conv0, g, wup, f, fb, wdn, gfinal]
    return pl.pallas_call(
        functools.partial(_ffn_kernel, tm=tm, final=final, merge=merge is not None),
        grid=(bx, t // tm),
        in_specs=in_specs,
        out_specs=[row(D_MODEL), cst],
        out_shape=[jax.ShapeDtypeStruct((bx, t, D_MODEL), F32),
                   jax.ShapeDtypeStruct((bx, CONV_W - 1, 2 * D_FF), F32)],
        scratch_shapes=[pltpu.VMEM((_HALO + tm, 2 * D_FF), F32), pltpu.VMEM((tm, D_FF), BF16)],
        compiler_params=_cparams(2),
        name="ffn",
    )(*args)


def _rope_tables(pos):
    half = ROT_DIM // 2
    inv = ROPE_THETA ** (-jnp.arange(0, ROT_DIM, 2, dtype=F32) / ROT_DIM)
    ang = pos.astype(F32)[:, None] * inv[None, :]
    cos, sin = jnp.cos(ang), jnp.sin(ang)
    t = pos.shape[0]
    pad = jnp.zeros((t, DA_HEAD_DIM - ROT_DIM), F32)
    z = jnp.zeros((t, half), F32)
    one_map = lambda a, b, fill: jnp.concatenate([a, b, pad + fill], axis=1)
    rc = one_map(cos, cos, 1.0)
    ra = one_map(z, sin, 0.0)
    rb = one_map(-sin, z, 0.0)
    dup = lambda m: jnp.concatenate([m, m], axis=1)
    return dup(rc), dup(ra), dup(rb)


def _state_to_blockdiag(s):
    b = s.shape[0]
    s = s.reshape(b, N_PAIRS, 2, RW_HEAD_DIM, RW_HEAD_DIM)
    z = jnp.zeros_like(s[:, :, 0])
    top = jnp.concatenate([s[:, :, 0], z], axis=-1)
    bot = jnp.concatenate([z, s[:, :, 1]], axis=-1)
    return jnp.concatenate([top, bot], axis=-2)


def _blockdiag_to_state(sbd):
    b = sbd.shape[0]
    h0 = sbd[:, :, :RW_HEAD_DIM, :RW_HEAD_DIM]
    h1 = sbd[:, :, RW_HEAD_DIM:, RW_HEAD_DIM:]
    return jnp.stack([h0, h1], axis=2).reshape(b, RW_HEADS, RW_HEAD_DIM, RW_HEAD_DIM)


def _layer(x, lidx, depth, tabs, kvbuf, cache, wkv0, shift0, conv0, w, norm_final, *, final):
    bx, t, _ = x.shape
    prompt = cache is None
    lam_init = 0.8 - 0.6 * math.exp(-0.3 * lidx)
    prep = lambda tri: (w["rw_mu"], w["rw_w0"], w["rw_w2p"], w["rw_a0"], w["rw_a2p"], w["rw_g2"],
                        w["rw_k_k"], w["rw_k_a"], w["seg"], tri)

    if prompt:
        tm = ROW_TILE
        q, k, v, kout, vout, gate, rt, at, bt, kt, vr, g, wc, new_shift = _in_proj(
            x, w["norm_mix"], w["w_in"], tabs, kvbuf, layer=lidx, depth=depth, tm=tm, fused=True,
            shift0=shift0, prep=prep(w["tri_p"]))
        o_da = _attn_prompt(w["da_lambda"], w["da_subln"], q, k, v, lam_init=lam_init)
    else:
        tm = t
        flat = lambda a: a.reshape(1, bx * t, a.shape[-1])
        q, k, v, kout, vout, gate, u_rw = _in_proj(flat(x), w["norm_mix"], w["w_in"], tabs, kvbuf,
                                                   layer=lidx, depth=depth, tm=bx * t, fused=False)
        u_rw = u_rw.reshape(bx, t, P_RW)
        new_shift = u_rw[:, -1:]
        pad_rows = lambda a, n: jnp.pad(a.reshape(bx, t, a.shape[-1]), ((0, 0), (0, n - t), (0, 0)))
        o_da = _attn_sample(w["da_lambda"], w["da_subln"], pad_rows(q, LANES), pad_rows(k, LANES),
                            pad_rows(v, LANES), *cache, layer=lidx, lam_init=lam_init, n_valid_new=t)[:, :t]
        rt, at, bt, kt, vr, g, wc = _rw_prep(u_rw, shift0, prep(w["tri_s"]), tm=t)
        rt, at, bt, kt, vr, g = [pad_rows(a, WKV_CHUNK) for a in (rt, at, bt, kt, vr, g)]
    o_rw, sbd = _wkv(rt, at, bt, kt, vr, g, wc, _state_to_blockdiag(wkv0),
                     w["rw_r_k"], w["rw_ln_w"], w["rw_ln_b"],
                     tb=WKV_ROWS if prompt else WKV_CHUNK, n_pairs=WKV_PAIRS if prompt else N_PAIRS)
    o_rw = o_rw[:, :t]

    mix = (o_da, o_rw, gate, w["w_o_da"], w["w_o_rw"], w["w_out"])
    if not prompt:
        x = _merge(flat(x), flat(o_da), flat(o_rw), *mix[2:], tm=bx * t, layer=lidx).reshape(bx, t, D_MODEL)
        mix = None
    x, new_conv = _ffn(x, conv0, w["norm_ffn"], w["w_up"], w["ffn_conv"], w["ffn_conv_b"], w["w_down"],
                       norm_final, tm=tm, final=final, layer=lidx, merge=mix)
    return x, (kout, vout), _blockdiag_to_state(sbd), new_shift, new_conv


def kernel(x_prompt, x_sample, cache_k, cache_v, state_wkv, state_shift, state_ffn_conv, norm_mix, w_in, da_lambda, da_subln, w_o_da, rw_mu, rw_w0, rw_w2, rw_a0, rw_a2, rw_g2, rw_k_k, rw_k_a, rw_r_k, rw_ln_w, rw_ln_b, w_o_rw, w_out, norm_ffn, w_up, ffn_conv, ffn_conv_b, w_down, norm_final):
    bp, tp, _ = x_prompt.shape
    bs, ts, _ = x_sample.shape
    depth = w_in.shape[0]
    past = cache_k.shape[2]
    tabs_p = _rope_tables(jnp.arange(tp, dtype=jnp.int32))
    tabs_s = tuple(jnp.tile(a, (bs, 1)) for a in _rope_tables(past + jnp.arange(ts, dtype=jnp.int32)))
    cache = (cache_k, cache_v)

    lane = jnp.arange(2 * LANES)
    seg = (lane[:, None] // RW_HEAD_DIM == lane[None, :] // RW_HEAD_DIM).astype(BF16)
    tri = lambda n: jnp.tile((jnp.arange(n)[:, None] >= jnp.arange(n)[None, :]).astype(BF16), (1, 3))
    tri_p, tri_s = tri(WKV_CHUNK), tri(ts)
    zl = jnp.zeros((DECAY_LORA, RW_WIDTH), F32)
    row = lambda a: a.reshape(1, -1)
    zero = lambda *s: jnp.zeros(s, x_prompt.dtype)
    wkv_z, shift_z, conv_z = zero(bp, RW_HEADS, RW_HEAD_DIM, RW_HEAD_DIM), zero(bp, 1, P_RW), zero(bp, CONV_W - 1, 2 * D_FF)

    xp, xs = x_prompt, x_sample
    kv_p = kv_s = None
    outs_p, outs_s = [], []
    nf = row(norm_final)
    big = {name: a.astype(BF16) for name, a in dict(w_in=w_in, w_o_da=w_o_da, w_o_rw=w_o_rw, w_out=w_out,
                                                     w_up=w_up, w_down=w_down).items()}
    for l in range(depth):
        w = dict(
            big, norm_mix=row(norm_mix[l]), da_lambda=da_lambda[l],
            da_subln=da_subln[l].reshape(-1, 1),
            rw_mu=row(rw_mu[l]), rw_w0=row(rw_w0[l]),
            rw_w2p=jnp.concatenate([rw_w2[l], zl], axis=0).astype(BF16),
            rw_a0=row(rw_a0[l]), rw_a2p=jnp.concatenate([zl, rw_a2[l]], axis=0).astype(BF16),
            rw_g2=rw_g2[l].astype(BF16), rw_k_k=row(rw_k_k[l]), rw_k_a=row(rw_k_a[l]),
            rw_r_k=row(rw_r_k[l]), rw_ln_w=row(rw_ln_w[l]), rw_ln_b=row(rw_ln_b[l]),
            norm_ffn=row(norm_ffn[l]), ffn_conv=ffn_conv[l], ffn_conv_b=row(ffn_conv_b[l]),
            seg=seg, tri_p=tri_p, tri_s=tri_s)
        final = l == depth - 1
        xp, kv_p, sp, shp, cp = _layer(xp, l, depth, tabs_p, kv_p, None, wkv_z, shift_z, conv_z, w, nf, final=final)
        xs, kv_s, sq, shq, cq = _layer(xs, l, depth, tabs_s, kv_s, cache, state_wkv[l], state_shift[l],
                                       state_ffn_conv[l], w, nf, final=final)
        outs_p.append((sp, shp, cp))
        outs_s.append((sq, shq, cq))

    wkv_prompt, shift_prompt, conv_prompt = [jnp.stack(t) for t in zip(*outs_p)]
    wkv_sample, shift_sample, conv_sample = [jnp.stack(t) for t in zip(*outs_s)]
    heads = lambda a, b, t: a.reshape(depth, b, t, DA_HEADS, DA_V_DIM)
    return (xp, xs, heads(kv_p[0], bp, tp), heads(kv_p[1], bp, tp), wkv_prompt, shift_prompt, conv_prompt,
            heads(kv_s[0], bs, ts), heads(kv_s[1], bs, ts), wkv_sample, shift_sample, conv_sample)
```

```python
import functools
import math

import jax
import jax.numpy as jnp
from jax import lax
from jax.experimental import pallas as pl
from jax.experimental.pallas import tpu as pltpu

F32 = jnp.float32
BF16 = jnp.bfloat16

D_MODEL = 1024
CHUNK = 64
EPS = 1e-6
DA_HEADS = 8
DA_HEAD_DIM = 64
DA_V_DIM = 128
DA_WIDTH = DA_HEADS * DA_V_DIM
ROPE_THETA = 500000.0
ROT_DIM = DA_HEAD_DIM // 4
RW_HEAD_DIM = 64
RW_HEADS = 16
RW_WIDTH = 1024
DECAY_LORA = 64
AAA_LORA = 64
GATE_LORA = 128
GN_EPS = 64e-5
D_FF = 2816
CONV_W = 3
P_DA = 3 * DA_WIDTH
P_RW = 3 * RW_WIDTH + DECAY_LORA + AAA_LORA + GATE_LORA
P_GATE = 2 * D_MODEL
P_TOTAL = P_DA + P_RW + P_GATE

LANES = 128
SUBLANES = 8
VMEM_LIMIT = 56 * 1024 * 1024

WKV_CHUNK = 64
N_PAIRS = RW_HEADS // 2
NEG_BIG = -1e30
ROW_TILE = 256
WKV_ROWS = 512
WKV_PAIRS = 4
CACHE_PARTS = 2

_ARB = "arbitrary"


def _cparams(n_axes):
    return pltpu.CompilerParams(dimension_semantics=(_ARB,) * n_axes,
                                vmem_limit_bytes=VMEM_LIMIT)


def _resident(shape):
    nd = len(shape)
    return pl.BlockSpec(shape, lambda *_: (0,) * nd, pipeline_mode=pl.Buffered(1))


def _layer_weight(shape, layer):
    nd = len(shape)
    return pl.BlockSpec((1,) + tuple(shape), lambda *_: (layer,) + (0,) * nd, pipeline_mode=pl.Buffered(1))


def _dot(a, b):
    return jnp.dot(a, b, preferred_element_type=F32)


def _dot_nt(a, b):
    return lax.dot_general(a, b, (((1,), (1,)), ((), ())), preferred_element_type=F32)


def _rms(x, g):
    return x * lax.rsqrt(jnp.mean(x * x, axis=-1, keepdims=True) + EPS) * g


def _sigmoid(x):
    return 1.0 / (1.0 + jnp.exp(-x))


def _split3(x):
    hi = x.astype(BF16)
    r1 = x - hi.astype(F32)
    mid = r1.astype(BF16)
    lo = (r1 - mid.astype(F32)).astype(BF16)
    return hi, mid, lo


def _prep_rows(us, w0_ref, w2_ref, a0_ref, a2_ref, g2_ref, kk_ref, ka_ref, seg_ref):
    r = us[:, 0:RW_WIDTH]
    kr = us[:, RW_WIDTH:2 * RW_WIDTH]
    vr = us[:, 2 * RW_WIDTH:3 * RW_WIDTH]
    wa = us[:, 3 * RW_WIDTH:3 * RW_WIDTH + DECAY_LORA + AAA_LORA]
    gd = us[:, 3 * RW_WIDTH + DECAY_LORA + AAA_LORA:]

    z = w0_ref[...] + _dot(jnp.tanh(wa).astype(BF16), w2_ref[...])
    lw = -math.exp(-0.5) * _sigmoid(z)
    a = _sigmoid(a0_ref[...] + _dot(wa.astype(BF16), a2_ref[...]))
    g = _dot(_sigmoid(gd).astype(BF16), g2_ref[...])

    kk = kr * kk_ref[...]
    k2 = (kk * kk).astype(BF16)
    seg = seg_ref[...]
    w = seg.shape[0]
    ssum = jnp.concatenate([_dot(k2[:, c:c + w], seg) for c in range(0, RW_WIDTH, w)],
                           axis=1)
    kk = kk * lax.rsqrt(jnp.maximum(ssum, 1e-24))
    kmod = kr * (1.0 + (a - 1.0) * ka_ref[...])
    return r, vr, g, lw, a, kk, kmod


def _prep_chunk(r, lw, a, kk, kmod, tri):
    cum = _dot(tri, jnp.concatenate(_split3(lw), axis=0))
    e_pos = jnp.exp(cum)
    e_neg = jnp.exp(-cum)
    rt = r * e_pos
    at = -kk * jnp.exp(cum - lw)
    bt = kk * a * e_neg
    kt = kmod * e_neg
    return rt, at, bt, kt, e_pos[-1:, :]


def _rw_prep_kernel(u_ref, sh_ref, mu_ref, w0_ref, w2_ref, a0_ref, a2_ref, g2_ref,
                    kk_ref, ka_ref, seg_ref, tri_ref,
                    rt_ref, at_ref, bt_ref, kt_ref, v_ref, g_ref, wc_ref, *, tm):
    u = u_ref[0]
    rows = lax.broadcasted_iota(jnp.int32, (tm, 1), 0)
    up = jnp.where(rows == 0, sh_ref[0], pltpu.roll(u, 1, 0))
    us = u + (up - u) * mu_ref[...]
    r, vr, g, lw, a, kk, kmod = _prep_rows(us, w0_ref, w2_ref, a0_ref, a2_ref, g2_ref, kk_ref, ka_ref, seg_ref)
    rt, at, bt, kt, wc = _prep_chunk(r, lw, a, kk, kmod, tri_ref[...])
    for ref, val in ((rt_ref, rt), (at_ref, at), (bt_ref, bt), (kt_ref, kt), (v_ref, vr), (g_ref, g)):
        ref[0] = val.astype(BF16)
    wc_ref[0, 0] = wc


def _prep_param_specs(chunk):
    return [_resident((1, P_RW)), _resident((1, RW_WIDTH)),
            _resident((DECAY_LORA + AAA_LORA, RW_WIDTH)), _resident((1, RW_WIDTH)),
            _resident((DECAY_LORA + AAA_LORA, RW_WIDTH)), _resident((GATE_LORA, RW_WIDTH)),
            _resident((1, RW_WIDTH)), _resident((1, RW_WIDTH)),
            _resident((2 * LANES, 2 * LANES)), _resident((chunk, 3 * chunk))]


def _rw_prep(u, shift0, prep, *, tm):
    bx, t, _ = u.shape
    assert t == tm
    row = lambda w: pl.BlockSpec((1, tm, w), lambda b: (b, 0, 0))
    outs = [jax.ShapeDtypeStruct((bx, t, RW_WIDTH), BF16)] * 6
    outs.append(jax.ShapeDtypeStruct((bx, 1, 1, RW_WIDTH), F32))
    return pl.pallas_call(
        functools.partial(_rw_prep_kernel, tm=tm),
        grid=(bx,),
        in_specs=[row(P_RW), pl.BlockSpec((1, 1, P_RW), lambda b: (b, 0, 0))] + _prep_param_specs(tm),
        out_specs=[row(RW_WIDTH)] * 6 + [pl.BlockSpec((1, 1, 1, RW_WIDTH), lambda b: (b, 0, 0, 0))],
        out_shape=outs,
        compiler_params=_cparams(1),
        name="rw_prep",
    )(u, shift0, *prep)


_IN_CW = 256
_LOG2E = math.log2(math.e)
_N_PREP = 10


def _in_proj_kernel(*refs, tm, fused, aliased):
    x_ref, g_ref, w_ref, rc_ref, ra_ref, rb_ref = refs[:6]
    w_ref = w_ref.at[0]
    n_in = 6
    if fused:
        sh_ref = refs[6]
        mu_ref, w0_ref, w2_ref, a0_ref, a2_ref, g2_ref, kk_ref, ka_ref, seg_ref, tri_ref = refs[7:7 + _N_PREP]
        n_in = 7 + _N_PREP
    n_in += 2 if aliased else 0
    q_ref, kn_ref, vn_ref, ko_ref, vo_ref, gt_ref = refs[n_in:n_in + 6]
    if fused:
        rt_ref, at_ref, bt_ref, kt_ref, vr_ref, gr_ref, wc_ref, last_ref = refs[n_in + 6:n_in + 14]
        k_scr, v_scr, u_s = refs[n_in + 14:]
    else:
        u_ref = refs[n_in + 6]
        k_scr, v_scr = refs[n_in + 7:]
    hb = _rms(x_ref[0], g_ref[...]).astype(BF16)
    rc = rc_ref[...]
    ra = ra_ref[...]
    rb = rb_ref[...]

    def rope(t):
        return t * rc + pltpu.roll(t, ROT_DIM // 2, 1) * ra + pltpu.roll(t, LANES - ROT_DIM // 2, 1) * rb

    def chunk(c0):
        if P_DA <= c0 < P_DA + P_RW:
            o = c0 - P_DA
            if fused:
                u_s[SUBLANES:, o:o + _IN_CW] = _dot(hb, w_ref[:, c0:c0 + _IN_CW])
            else:
                u_ref[0, :, o:o + _IN_CW] = _dot(hb, w_ref[:, c0:c0 + _IN_CW])
            return
        acc = _dot(hb, w_ref[:, c0:c0 + _IN_CW])
        if c0 < DA_WIDTH:
            for s in range(0, _IN_CW, LANES):
                q_ref[0, :, c0 + s:c0 + s + LANES] = (
                    rope(acc[:, s:s + LANES]) * (DA_HEAD_DIM ** -0.5 * _LOG2E)).astype(BF16)
        elif c0 < 2 * DA_WIDTH:
            o = c0 - DA_WIDTH
            for s in range(0, _IN_CW, LANES):
                kr = rope(acc[:, s:s + LANES])
                k_scr[:, o + s:o + s + LANES] = kr
                kn_ref[0, :, o + s:o + s + LANES] = kr.astype(BF16)
        elif c0 < P_DA:
            o = c0 - 2 * DA_WIDTH
            v_scr[:, o:o + _IN_CW] = acc
            vn_ref[0, :, o:o + _IN_CW] = acc.astype(BF16)
        else:
            o = c0 - P_DA - P_RW
            gt_ref[0, :, o:o + _IN_CW] = acc.astype(BF16)

    def chunks(lo, hi):
        for c0 in range(lo, hi, _IN_CW):
            chunk(c0)

    if not fused:
        chunks(0, P_TOTAL)
    else:
        @pl.when(pl.program_id(1) == 0)
        def _():
            u_s[SUBLANES - 1:SUBLANES, :] = sh_ref[0]

        chunks(0, P_DA + P_RW)
        u = u_s[SUBLANES:SUBLANES + tm, :]
        up = u_s[SUBLANES - 1:SUBLANES - 1 + tm, :]
        us = u + (up - u) * mu_ref[...]
        last = u[tm - 1:tm, :]
        last_ref[0] = last
        u_s[SUBLANES - 1:SUBLANES, :] = last
        r, vr, g, lw, a, kk, kmod = _prep_rows(us, w0_ref, w2_ref, a0_ref, a2_ref, g2_ref, kk_ref, ka_ref, seg_ref)
        vr_ref[0] = vr.astype(BF16)
        gr_ref[0] = g.astype(BF16)
        chunks(P_DA + P_RW, P_DA + P_RW + P_GATE // 2)
        tri = tri_ref[...]
        for c in range(tm // WKV_CHUNK):
            sl = slice(c * WKV_CHUNK, (c + 1) * WKV_CHUNK)
            rt, at, bt, kt, wc = _prep_chunk(r[sl], lw[sl], a[sl], kk[sl], kmod[sl], tri)
            for ref, val in ((rt_ref, rt), (at_ref, at), (bt_ref, bt), (kt_ref, kt)):
                ref[0, sl, :] = val.astype(BF16)
            wc_ref[0, c] = wc
        chunks(P_DA + P_RW + P_GATE // 2, P_TOTAL)
    ko_ref[0, 0] = k_scr[...].reshape(tm, DA_HEADS, DA_V_DIM)
    vo_ref[0, 0] = v_scr[...].reshape(tm, DA_HEADS, DA_V_DIM)


def _in_proj(x, g, w_bf, tabs, kvout, *, layer, depth, tm, fused, shift0=None, prep=None):
    bx, t, _ = x.shape
    row = lambda w: pl.BlockSpec((1, tm, w), lambda b, i: (b, i, 0))
    tab = pl.BlockSpec((tm, LANES), lambda b, i: (i, 0))
    slot = pl.BlockSpec((1, 1, tm, DA_HEADS, DA_V_DIM), lambda b, i: (layer, b, i, 0, 0))
    aliased = kvout is not None
    in_specs = [row(D_MODEL), _resident((1, D_MODEL)), _layer_weight((D_MODEL, P_TOTAL), layer), tab, tab, tab]
    args = [x, g, w_bf, *tabs]
    if fused:
        in_specs += [pl.BlockSpec((1, 1, P_RW), lambda b, i: (b, 0, 0))] + _prep_param_specs(WKV_CHUNK)
        args += [shift0, *prep]
    if aliased:
        aliases = {len(args): 3, len(args) + 1: 4}
        in_specs += [pl.BlockSpec(memory_space=pl.ANY)] * 2
        args += list(kvout)
    else:
        aliases = {}
    nat = jax.ShapeDtypeStruct((bx, t, DA_WIDTH), BF16)
    stacked = jax.ShapeDtypeStruct((depth, bx, t, DA_HEADS, DA_V_DIM), F32)
    out_specs = [row(DA_WIDTH)] * 3 + [slot, slot, row(P_GATE)]
    out_shape = [nat, nat, nat, stacked, stacked, jax.ShapeDtypeStruct((bx, t, P_GATE), BF16)]
    scratch = [pltpu.VMEM((tm, DA_WIDTH), F32)] * 2
    if fused:
        nc = tm // WKV_CHUNK
        out_specs += [row(RW_WIDTH)] * 6 + [pl.BlockSpec((1, nc, 1, RW_WIDTH), lambda b, i: (b, i, 0, 0)),
                                            pl.BlockSpec((1, 1, P_RW), lambda b, i: (b, 0, 0))]
        out_shape += [jax.ShapeDtypeStruct((bx, t, RW_WIDTH), BF16)] * 6
        out_shape += [jax.ShapeDtypeStruct((bx, t // WKV_CHUNK, 1, RW_WIDTH), F32),
                      jax.ShapeDtypeStruct((bx, 1, P_RW), F32)]
        scratch.append(pltpu.VMEM((SUBLANES + tm, P_RW), F32))
    else:
        out_specs.append(row(P_RW))
        out_shape.append(jax.ShapeDtypeStruct((bx, t, P_RW), F32))
    return pl.pallas_call(
        functools.partial(_in_proj_kernel, tm=tm, fused=fused, aliased=aliased),
        grid=(bx, t // tm),
        in_specs=in_specs,
        out_specs=out_specs,
        out_shape=out_shape,
        scratch_shapes=scratch,
        input_output_aliases=aliases,
        compiler_params=_cparams(2),
        name="in_proj",
    )(*args)


_TK = 256
_CHUNK_SHIFT = CHUNK.bit_length() - 1


def _lambda(lam_ref, lam_init):
    lp = lam_ref[...]
    return (jnp.exp(jnp.sum(lp[0:1] * lp[1:2], axis=1, keepdims=True))
            - jnp.exp(jnp.sum(lp[2:3] * lp[3:4], axis=1, keepdims=True)) + lam_init)


def _q_blockdiag(q_tile):
    tq = q_tile.shape[0]
    qt = q_tile.astype(F32).T.astype(BF16)
    z = jnp.zeros((DA_HEAD_DIM, tq), BF16)
    return jnp.concatenate([jnp.concatenate([qt[:DA_HEAD_DIM], z], axis=1),
                            jnp.concatenate([z, qt[DA_HEAD_DIM:]], axis=1)], axis=0)


def _softmax_step(s, stats, mask, tq):
    new_stats, alphas, ps = [], [], []
    for c in range(2):
        m, l = stats[2 * c:2 * c + 2]
        sc = s[:, c * tq:(c + 1) * tq]
        if mask is not None:
            sc = jnp.where(mask, sc, NEG_BIG)
        m_new = jnp.maximum(m, jnp.max(sc, axis=0, keepdims=True))
        alpha = jnp.exp2(m - m_new)
        p = jnp.exp2(sc - m_new)
        new_stats += [m_new, alpha * l + jnp.sum(p, axis=0, keepdims=True)]
        alphas.append(alpha)
        ps.append(p.astype(BF16))
    return tuple(new_stats), tuple(alphas), tuple(ps)


def _attn_finish(a1, a2, l1, l2, lam, sub_col, lam_init):
    ot = a1 * (1.0 / l1) - lam * (a2 * (1.0 / l2))
    ss = jnp.sum(ot * ot, axis=0, keepdims=True) * (1.0 / DA_V_DIM)
    ot = ot * lax.rsqrt(ss + EPS) * sub_col * (1.0 - lam_init)
    return ot.T.astype(BF16)


def _attn_prompt_kernel(lam_ref, sub_ref, q_ref, k_ref, v_ref, o_ref, vt_s, s_s, *, n_tiles, lam_init):
    tq = _TK
    for jb in range(n_tiles):
        vt_s[jb] = v_ref[0, jb * _TK:(jb + 1) * _TK, :].astype(F32).T.astype(BF16)
    lam = _lambda(lam_ref, lam_init)
    sub_col = sub_ref[...]
    kr = jnp.right_shift(lax.broadcasted_iota(jnp.int32, (_TK, 2 * tq), 0), _CHUNK_SHIFT)
    qc = jnp.right_shift(lax.broadcasted_iota(jnp.int32, (_TK, 2 * tq), 1) & (tq - 1), _CHUNK_SHIFT)
    diag_mask = kr <= qc

    def new_tile(i):
        return dict(qbd=_q_blockdiag(q_ref[0, i * tq:(i + 1) * tq, :]), mx=None)

    def scores(i, j, tile):
        s = _dot(k_ref[0, j * _TK:(j + 1) * _TK, :], tile["qbd"])
        if j == i:
            s = jnp.where(diag_mask, s, NEG_BIG)
        s_s[i % 2, j] = s
        part = jnp.max(s.reshape(_TK // SUBLANES, SUBLANES, 2 * tq), axis=0)
        tile["mx"] = part if tile["mx"] is None else jnp.maximum(tile["mx"], part)

    cur = new_tile(0)
    scores(0, 0, cur)
    for i in range(n_tiles):
        m = jnp.max(cur["mx"], axis=0, keepdims=True)
        nxt = new_tile(i + 1) if i + 1 < n_tiles else None
        acc, lsum, pending = [None, None], None, None

        def values(jp, pb):
            for c in range(2):
                pv = _dot(vt_s[jp], pb[:, c * tq:(c + 1) * tq])
                acc[c] = pv if acc[c] is None else acc[c] + pv

        for j in range(i + 1):
            if nxt is not None:
                scores(i + 1, j, nxt)
            p = jnp.exp2(s_s[i % 2, j] - m)
            part = jnp.sum(p.reshape(_TK // SUBLANES, SUBLANES, 2 * tq), axis=0)
            lsum = part if lsum is None else lsum + part
            if pending is not None:
                values(*pending)
            pending = (j, p.astype(BF16))
        if nxt is not None:
            scores(i + 1, i + 1, nxt)
        values(*pending)
        l = jnp.sum(lsum, axis=0, keepdims=True)
        o_ref[0, i * tq:(i + 1) * tq, :] = _attn_finish(acc[0], acc[1], l[:, :tq], l[:, tq:], lam, sub_col,
                                                         lam_init)
        cur = nxt


def _attn_prompt(da_lambda, subln_col, q, k, v, *, lam_init):
    bx, t, _ = q.shape
    n_tiles = t // _TK
    seq = pl.BlockSpec((1, t, LANES), lambda b, h: (b, 0, h))
    return pl.pallas_call(
        functools.partial(_attn_prompt_kernel, n_tiles=n_tiles, lam_init=lam_init),
        grid=(bx, DA_HEADS),
        in_specs=[_resident((4, DA_HEAD_DIM)), _resident((DA_V_DIM, 1)), seq, seq, seq],
        out_specs=seq,
        out_shape=jax.ShapeDtypeStruct((bx, t, DA_WIDTH), BF16),
        scratch_shapes=[pltpu.VMEM((n_tiles, DA_V_DIM, _TK), BF16),
                        pltpu.VMEM((2, n_tiles, _TK, 2 * _TK), F32)],
        compiler_params=_cparams(2),
        name="diff_attn",
    )(da_lambda, subln_col, q, k, v)


def _attn_sample_kernel(lam_ref, sub_ref, q_ref, k_ref, v_ref, kp_ref, vp_ref, o_ref, kb_s, vb_s,
                        *, tq, n_new, n_past, n_parts, n_valid_new, lam_init):
    part = pl.program_id(1)
    rows_part = n_past // n_parts
    for r0 in range(0, rows_part, _TK):
        dst = pl.ds(pl.multiple_of(part * rows_part + r0, _TK), _TK)
        kb_s[dst, :] = kp_ref[0, 0, r0:r0 + _TK].reshape(_TK, DA_WIDTH).astype(BF16)
        vb_s[dst, :] = vp_ref[0, 0, r0:r0 + _TK].reshape(_TK, DA_WIDTH).astype(BF16)

    @pl.when(part == n_parts - 1)
    def _():
        _attn_sample_heads(lam_ref, sub_ref, q_ref, k_ref, v_ref, o_ref, kb_s, vb_s, tq=tq, n_new=n_new,
                           n_past=n_past, n_valid_new=n_valid_new, lam_init=lam_init)


def _attn_sample_heads(lam_ref, sub_ref, q_ref, k_ref, v_ref, o_ref, kb_s, vb_s,
                       *, tq, n_new, n_past, n_valid_new, lam_init):
    assert 2 * tq == LANES
    lam = _lambda(lam_ref, lam_init)
    sub_col = sub_ref[...]
    valid = lax.broadcasted_iota(jnp.int32, (n_new, LANES), 0) < n_valid_new
    kb_s[n_past:n_past + n_new, :] = k_ref[0]
    vb_s[n_past:n_past + n_new, :] = v_ref[0]
    bounds = [(r0, _TK, None) for r0 in range(0, n_past, _TK)] + [(n_past, n_new, valid)]
    own_map = (lax.broadcasted_iota(jnp.int32, (2 * tq, LANES), 0) < tq) == (
        lax.broadcasted_iota(jnp.int32, (2 * tq, LANES), 1) < DA_HEAD_DIM)
    for h in range(DA_HEADS):
        lanes = slice(h * LANES, (h + 1) * LANES)
        q = q_ref[0, :, lanes]
        qq = jnp.concatenate([q, q], axis=0)
        qbd = jnp.where(own_map, qq, jnp.zeros_like(qq)).astype(F32).T.astype(BF16)
        scores = []
        for r0, n, mask in bounds:
            s = _dot(kb_s[r0:r0 + n, lanes], qbd)
            scores.append(s if mask is None else jnp.where(mask, s, NEG_BIG))
        m = functools.reduce(jnp.maximum, [jnp.max(s, axis=0, keepdims=True) for s in scores])
        l, acc = None, None
        for s, (r0, n, _) in zip(scores, bounds):
            p = jnp.exp2(s - m)
            pl_ = jnp.sum(p, axis=0, keepdims=True)
            pv = _dot(vb_s[r0:r0 + n, lanes].astype(F32).T.astype(BF16), p.astype(BF16))
            l, acc = (pl_, pv) if l is None else (l + pl_, acc + pv)
        r = acc * (1.0 / l)
        ot = r - lam * pltpu.roll(r, tq, 1)
        ss = jnp.sum(ot * ot, axis=0, keepdims=True) * (1.0 / DA_V_DIM)
        ot = ot * lax.rsqrt(ss + EPS) * sub_col * (1.0 - lam_init)
        o_ref[0, :, lanes] = ot.T[:tq].astype(BF16)


def _attn_sample(da_lambda, subln_col, q, k, v, past_k, past_v, *, layer, lam_init, n_valid_new):
    bx, tq, _ = q.shape
    n_new = k.shape[1]
    n_past = past_k.shape[2]
    n_parts = CACHE_PARTS
    new = pl.BlockSpec((1, n_new, DA_WIDTH), lambda b, j: (b, 0, 0))
    old = pl.BlockSpec((1, 1, n_past // n_parts, DA_HEADS, DA_V_DIM), lambda b, j: (layer, b, j, 0, 0))
    return pl.pallas_call(
        functools.partial(_attn_sample_kernel, tq=tq, n_new=n_new, n_past=n_past, n_parts=n_parts,
                          n_valid_new=n_valid_new, lam_init=lam_init),
        grid=(bx, n_parts),
        in_specs=[_resident((4, DA_HEAD_DIM)), _resident((DA_V_DIM, 1)),
                  pl.BlockSpec((1, tq, DA_WIDTH), lambda b, j: (b, 0, 0)), new, new, old, old],
        out_specs=pl.BlockSpec((1, tq, DA_WIDTH), lambda b, j: (b, 0, 0)),
        out_shape=jax.ShapeDtypeStruct((bx, tq, DA_WIDTH), BF16),
        scratch_shapes=[pltpu.VMEM((n_past + n_new, DA_WIDTH), BF16)] * 2,
        compiler_params=_cparams(2),
        name="diff_attn_step",
    )(da_lambda, subln_col, q, k, v, past_k, past_v)


def _wkv_kernel(rt_ref, at_ref, bt_ref, kt_ref, v_ref, g_ref, wc_ref, s0_ref, rk_ref, lnw_ref, lnb_ref,
                o_ref, sout_ref, s_scr, rp_s, m_s, y0_s, gc_s, bon_s, *, n_pairs, n_chunks):
    C = WKV_CHUNK
    R = 2 * C
    t = pl.program_id(2)

    @pl.when(t == 0)
    def _():
        s_scr[...] = s0_ref[0]

    ri = lax.broadcasted_iota(jnp.int32, (R, R), 0)
    ci = lax.broadcasted_iota(jnp.int32, (R, R), 1)
    sh = lambda x, sz: jnp.right_shift(x, sz.bit_length() - 1)
    same_head = sh(ri, C) == sh(ci, C)
    strict = same_head & (ci < ri)
    incl = same_head & (ci <= ri)
    eye = ri == ci

    def bd(sz):
        return sh(ri, sz) == sh(ci, sz)

    lane_lo = lax.broadcasted_iota(jnp.int32, (C, LANES), 1) < RW_HEAD_DIM

    def stack(x):
        z = jnp.zeros_like(x)
        return jnp.concatenate([jnp.where(lane_lo, x, z), jnp.where(lane_lo, z, x)], axis=0)

    b16 = lambda x: x.astype(BF16)
    f32 = lambda x: x.astype(F32)
    ident = jnp.where(eye, 1.0, 0.0)

    units = [(p, c) for p in range(n_pairs) for c in range(n_chunks)]
    idx = range(len(units))
    rs, asb, bsb, ksb, vsb, bdk, kdk, wcs = [], [], [], [], [], [], [], []
    for u, (p, c) in enumerate(units):
        rows, lanes = slice(c * C, (c + 1) * C), slice(p * LANES, (p + 1) * LANES)
        r_, a_, b_, k_, v_ = (stack(ref[0, rows, lanes]) for ref in (rt_ref, at_ref, bt_ref, kt_ref, v_ref))
        wc = wc_ref[0, c, :, lanes]
        rk = rk_ref[:, lanes]
        bon_s[u] = jnp.sum(f32(r_) * f32(k_) * rk, axis=1, keepdims=True) * f32(v_)
        rs.append(r_); asb.append(a_); bsb.append(b_); ksb.append(k_); vsb.append(v_)
        bdk.append(b16(f32(b_) * wc)); kdk.append(b16(f32(k_) * wc)); wcs.append(wc)

    n, n8, aak, arb, ark = [], [], [], [], []
    for u in idx:
        ab = _dot_nt(jnp.concatenate([asb[u], rs[u]], axis=0), jnp.concatenate([bsb[u], ksb[u]], axis=0))
        nc = jnp.where(strict, ab[:R, :R], 0.0)
        n.append(b16(nc)); n8.append(jnp.where(bd(8), nc, 0.0))
        aak.append(b16(jnp.where(strict, ab[:R, R:], 0.0)))
        arb.append(b16(jnp.where(incl, ab[R:, :R], 0.0)))
        ark.append(b16(jnp.where(incl, ab[R:, R:], 0.0)))

    n8b = [b16(x) for x in n8]
    n2 = [_dot(n8b[u], n8b[u]) for u in idx]
    n2b = [b16(x) for x in n2]
    n4 = [_dot(n2b[u], n2b[u]) for u in idx]
    p1 = [_dot(b16(ident + n8[u]), b16(ident + n2[u])) for u in idx]
    tinv = [_dot(b16(p1[u]), b16(ident + n4[u])) for u in idx]
    zero16 = jnp.zeros((R, R), BF16)
    for sz in (8, 16, 32):
        off = bd(2 * sz) & jnp.logical_not(bd(sz))
        tb = [b16(x) for x in tinv]
        x = [_dot(tb[u], jnp.where(off, n[u], zero16)) for u in idx]
        tinv = [tinv[u] + _dot(b16(x[u]), tb[u]) for u in idx]
    tb = [b16(x) for x in tinv]

    u0p = [_dot(aak[u], vsb[u]) for u in idx]
    tx = [_dot(tb[u], jnp.concatenate([asb[u], b16(u0p[u])], axis=1)) for u in idx]
    txb = [b16(x) for x in tx]
    az = [_dot(arb[u], txb[u]) for u in idx]
    for u in idx:
        rp_s[u] = b16(f32(rs[u]) + az[u][:, :LANES])
        y0_s[u] = az[u][:, LANES:] + _dot(ark[u], vsb[u])
    for u in idx:
        txt = b16(tx[u].T)
        mg = _dot(txt, bdk[u])
        m_s[u] = b16(jnp.where(eye, wcs[u], 0.0) + mg[:LANES])
        gc_s[u] = mg[LANES:] + _dot(b16(f32(vsb[u]).T), kdk[u])

    s = [s_scr[p] for p in range(n_pairs)]
    for c in range(n_chunks):
        for p in range(n_pairs):
            u = p * n_chunks + c
            rows, lanes = slice(c * C, (c + 1) * C), slice(p * LANES, (p + 1) * LANES)
            sb = b16(s[p])
            y = _dot_nt(rp_s[u], sb) + y0_s[u]
            s[p] = _dot(sb, m_s[u]) + gc_s[u]
            mu = jnp.sum(y, axis=1, keepdims=True) * (1.0 / RW_HEAD_DIM)
            d = jnp.where(same_head, y - mu, 0.0)
            var = jnp.sum(d * d, axis=1, keepdims=True) * (1.0 / RW_HEAD_DIM)
            ost = (d * lax.rsqrt(var + GN_EPS) * lnw_ref[:, lanes]
                   + jnp.where(same_head, lnb_ref[:, lanes], 0.0) + bon_s[u])
            o_ref[0, rows, lanes] = ((ost[:C] + ost[C:]) * f32(g_ref[0, rows, lanes])).astype(BF16)
    for p in range(n_pairs):
        s_scr[p] = s[p]
        sout_ref[0, p] = s[p]


def _wkv(rt, at, bt, kt, v, g, wc, s0bd, rk, lnw, lnb, *, tb, n_pairs):
    bx, t, _ = rt.shape
    n_chunks = tb // WKV_CHUNK
    n_units = n_pairs * n_chunks
    w = n_pairs * LANES
    seq = pl.BlockSpec((1, tb, w), lambda b, p, i: (b, i, p))
    vec = pl.BlockSpec((1, w), lambda b, p, i: (0, p))
    st = pl.BlockSpec((1, n_pairs, LANES, LANES), lambda b, p, i: (b, p, 0, 0))
    mat = lambda dt: pltpu.VMEM((n_units, LANES, LANES), dt)
    return pl.pallas_call(
        functools.partial(_wkv_kernel, n_pairs=n_pairs, n_chunks=n_chunks),
        grid=(bx, N_PAIRS // n_pairs, t // tb),
        in_specs=[seq] * 6 + [pl.BlockSpec((1, n_chunks, 1, w), lambda b, p, i: (b, i, 0, p)),
                              st, vec, vec, vec],
        out_specs=[seq, st],
        out_shape=[jax.ShapeDtypeStruct((bx, t, RW_WIDTH), BF16),
                   jax.ShapeDtypeStruct((bx, N_PAIRS, LANES, LANES), F32)],
        scratch_shapes=[pltpu.VMEM((n_pairs, LANES, LANES), F32),
                        mat(BF16), mat(BF16), mat(F32), mat(F32), mat(F32)],
        compiler_params=_cparams(3),
        name="wkv",
    )(rt, at, bt, kt, v, g, wc, s0bd, rk, lnw, lnb)


def _merge_kernel(x_ref, oda_ref, orw_ref, gt_ref, wda_ref, wrw_ref, wout_ref, o_ref):
    a = _dot(oda_ref[0], wda_ref[0])
    b = _dot(orw_ref[0], wrw_ref[0])
    gt = gt_ref[0].astype(F32)
    m = _sigmoid(gt[:, :D_MODEL]) * a + _sigmoid(gt[:, D_MODEL:]) * b
    o_ref[0] = x_ref[0] + _dot(m.astype(BF16), wout_ref[0])


def _merge(x, oda, orw, gt, wda, wrw, wout, *, tm, layer):
    bx, t, _ = x.shape
    row = lambda w: pl.BlockSpec((1, tm, w), lambda b, i: (b, i, 0))
    sq = _layer_weight((D_MODEL, D_MODEL), layer)
    return pl.pallas_call(
        _merge_kernel,
        grid=(bx, t // tm),
        in_specs=[row(D_MODEL), row(DA_WIDTH), row(RW_WIDTH), row(P_GATE), sq, sq, sq],
        out_specs=row(D_MODEL),
        out_shape=jax.ShapeDtypeStruct((bx, t, D_MODEL), F32),
        compiler_params=_cparams(2),
        name="merge",
    )(x, oda, orw, gt, wda, wrw, wout)


_FF_CW = 256
_HALO = SUBLANES


def _ffn_kernel(*refs, tm, final, merge):
    if merge:
        x_ref, oda_ref, orw_ref, gt_ref, wda_ref, wrw_ref, wout_ref = refs[:7]
        refs = refs[7:]
        a = _dot(oda_ref[0], wda_ref[0])
        b = _dot(orw_ref[0], wrw_ref[0])
        gt = gt_ref[0].astype(F32)
        m = _sigmoid(gt[:, :D_MODEL]) * a + _sigmoid(gt[:, D_MODEL:]) * b
        x = x_ref[0] + _dot(m.astype(BF16), wout_ref[0])
    else:
        x = refs[0][0]
        refs = refs[1:]
    c0_ref, g_ref, wup_ref, f_ref, fb_ref, wdn_ref, gf_ref, o_ref, conv_ref, ext_s, act_s = refs
    wup_ref, wdn_ref = wup_ref.at[0], wdn_ref.at[0]

    @pl.when(pl.program_id(1) == 0)
    def _():
        ext_s[_HALO - 2:_HALO, :] = c0_ref[0]

    hb = _rms(x, g_ref[...]).astype(BF16)
    starts = list(range(0, D_FF, _FF_CW))

    def up_proj(c0):
        for base in (c0, D_FF + c0):
            cols = slice(base, base + _FF_CW)
            up = _dot(hb, wup_ref[:, cols])
            ext_s[_HALO:, cols] = up
            conv_ref[0, :, cols] = up[tm - 2:tm]

    def conv_act(c0):
        cs = []
        for base in (c0, D_FF + c0):
            cols = slice(base, base + _FF_CW)
            f = f_ref[:, cols]
            cs.append(fb_ref[:, cols]
                      + ext_s[_HALO - 2:_HALO - 2 + tm, cols] * f[0:1]
                      + ext_s[_HALO - 1:_HALO - 1 + tm, cols] * f[1:2]
                      + ext_s[_HALO:_HALO + tm, cols] * f[2:3])
            ext_s[_HALO - 2:_HALO, cols] = ext_s[_HALO + tm - 2:_HALO + tm, cols]
        ca, cb = cs
        act_s[:, c0:c0 + _FF_CW] = (ca * _sigmoid(ca) * cb).astype(BF16)

    up_proj(starts[0])
    for j, c0 in enumerate(starts):
        if j + 1 < len(starts):
            up_proj(starts[j + 1])
        conv_act(c0)
    out = x + _dot(act_s[...], wdn_ref[...])
    if final:
        out = _rms(out, gf_ref[...])
    o_ref[0] = out


def _ffn(x, conv0, g, wup, f, fb, wdn, gfinal, *, tm, final, layer, merge=None):
    bx, t, _ = x.shape
    row = lambda w: pl.BlockSpec((1, tm, w), lambda b, i: (b, i, 0))
    cst = pl.BlockSpec((1, CONV_W - 1, 2 * D_FF), lambda b, i: (b, 0, 0))
    in_specs, args = [row(D_MODEL)], [x]
    if merge is not None:
        sq = _layer_weight((D_MODEL, D_MODEL), layer)
        in_specs += [row(DA_WIDTH), row(RW_WIDTH), row(P_GATE), sq, sq, sq]
        args += list(merge)
    in_specs += [cst, _resident((1, D_MODEL)), _layer_weight((D_MODEL, 2 * D_FF), layer),
                 _resident((CONV_W, 2 * D_FF)), _resident((1, 2 * D_FF)), _layer_weight((D_FF, D_MODEL), layer),
                 _resident((1, D_MODEL))]
    args += [conv0, g, wup, f, fb, wdn, gfinal]
    return pl.pallas_call(
        functools.partial(_ffn_kernel, tm=tm, final=final, merge=merge is not None),
        grid=(bx, t // tm),
        in_specs=in_specs,
        out_specs=[row(D_MODEL), cst],
        out_shape=[jax.ShapeDtypeStruct((bx, t, D_MODEL), F32),
                   jax.ShapeDtypeStruct((bx, CONV_W - 1, 2 * D_FF), F32)],
        scratch_shapes=[pltpu.VMEM((_HALO + tm, 2 * D_FF), F32), pltpu.VMEM((tm, D_FF), BF16)],
        compiler_params=_cparams(2),
        name="ffn",
    )(*args)


def _rope_tables(pos):
    half = ROT_DIM // 2
    inv = ROPE_THETA ** (-jnp.arange(0, ROT_DIM, 2, dtype=F32) / ROT_DIM)
    ang = pos.astype(F32)[:, None] * inv[None, :]
    cos, sin = jnp.cos(ang), jnp.sin(ang)
    t = pos.shape[0]
    pad = jnp.zeros((t, DA_HEAD_DIM - ROT_DIM), F32)
    z = jnp.zeros((t, half), F32)
    one_map = lambda a, b, fill: jnp.concatenate([a, b, pad + fill], axis=1)
    rc = one_map(cos, cos, 1.0)
    ra = one_map(z, sin, 0.0)
    rb = one_map(-sin, z, 0.0)
    dup = lambda m: jnp.concatenate([m, m], axis=1)
    return dup(rc), dup(ra), dup(rb)


def _state_to_blockdiag(s):
    b = s.shape[0]
    s = s.reshape(b, N_PAIRS, 2, RW_HEAD_DIM, RW_HEAD_DIM)
    z = jnp.zeros_like(s[:, :, 0])
    top = jnp.concatenate([s[:, :, 0], z], axis=-1)
    bot = jnp.concatenate([z, s[:, :, 1]], axis=-1)
    return jnp.concatenate([top, bot], axis=-2)


def _blockdiag_to_state(sbd):
    b = sbd.shape[0]
    h0 = sbd[:, :, :RW_HEAD_DIM, :RW_HEAD_DIM]
    h1 = sbd[:, :, RW_HEAD_DIM:, RW_HEAD_DIM:]
    return jnp.stack([h0, h1], axis=2).reshape(b, RW_HEADS, RW_HEAD_DIM, RW_HEAD_DIM)


def _layer(x, lidx, depth, tabs, kvbuf, cache, wkv0, shift0, conv0, w, norm_final, *, final):
    bx, t, _ = x.shape
    prompt = cache is None
    lam_init = 0.8 - 0.6 * math.exp(-0.3 * lidx)
    prep = lambda tri: (w["rw_mu"], w["rw_w0"], w["rw_w2p"], w["rw_a0"], w["rw_a2p"], w["rw_g2"],
                        w["rw_k_k"], w["rw_k_a"], w["seg"], tri)

    if prompt:
        tm = ROW_TILE
        q, k, v, kout, vout, gate, rt, at, bt, kt, vr, g, wc, new_shift = _in_proj(
            x, w["norm_mix"], w["w_in"], tabs, kvbuf, layer=lidx, depth=depth, tm=tm, fused=True,
            shift0=shift0, prep=prep(w["tri_p"]))
        o_da = _attn_prompt(w["da_lambda"], w["da_subln"], q, k, v, lam_init=lam_init)
    else:
        tm = t
        flat = lambda a: a.reshape(1, bx * t, a.shape[-1])
        q, k, v, kout, vout, gate, u_rw = _in_proj(flat(x), w["norm_mix"], w["w_in"], tabs, kvbuf,
                                                   layer=lidx, depth=depth, tm=bx * t, fused=False)
        u_rw = u_rw.reshape(bx, t, P_RW)
        new_shift = u_rw[:, -1:]
        pad_rows = lambda a, n: jnp.pad(a.reshape(bx, t, a.shape[-1]), ((0, 0), (0, n - t), (0, 0)))
        o_da = _attn_sample(w["da_lambda"], w["da_subln"], pad_rows(q, LANES // 2), pad_rows(k, LANES),
                            pad_rows(v, LANES), *cache, layer=lidx, lam_init=lam_init, n_valid_new=t)[:, :t]
        rt, at, bt, kt, vr, g, wc = _rw_prep(u_rw, shift0, prep(w["tri_s"]), tm=t)
        rt, at, bt, kt, vr, g = [pad_rows(a, WKV_CHUNK) for a in (rt, at, bt, kt, vr, g)]
    o_rw, sbd = _wkv(rt, at, bt, kt, vr, g, wc, _state_to_blockdiag(wkv0),
                     w["rw_r_k"], w["rw_ln_w"], w["rw_ln_b"],
                     tb=WKV_ROWS if prompt else WKV_CHUNK, n_pairs=WKV_PAIRS if prompt else N_PAIRS)
    o_rw = o_rw[:, :t]

    mix = (o_da, o_rw, gate, w["w_o_da"], w["w_o_rw"], w["w_out"])
    if not prompt:
        x = _merge(flat(x), flat(o_da), flat(o_rw), *mix[2:], tm=bx * t, layer=lidx).reshape(bx, t, D_MODEL)
        mix = None
    x, new_conv = _ffn(x, conv0, w["norm_ffn"], w["w_up"], w["ffn_conv"], w["ffn_conv_b"], w["w_down"],
                       norm_final, tm=tm, final=final, layer=lidx, merge=mix)
    return x, (kout, vout), _blockdiag_to_state(sbd), new_shift, new_conv


def kernel(x_prompt, x_sample, cache_k, cache_v, state_wkv, state_shift, state_ffn_conv, norm_mix, w_in, da_lambda, da_subln, w_o_da, rw_mu, rw_w0, rw_w2, rw_a0, rw_a2, rw_g2, rw_k_k, rw_k_a, rw_r_k, rw_ln_w, rw_ln_b, w_o_rw, w_out, norm_ffn, w_up, ffn_conv, ffn_conv_b, w_down, norm_final):
    bp, tp, _ = x_prompt.shape
    bs, ts, _ = x_sample.shape
    depth = w_in.shape[0]
    past = cache_k.shape[2]
    tabs_p = _rope_tables(jnp.arange(tp, dtype=jnp.int32))
    tabs_s = tuple(jnp.tile(a, (bs, 1)) for a in _rope_tables(past + jnp.arange(ts, dtype=jnp.int32)))
    cache = (cache_k, cache_v)

    lane = jnp.arange(2 * LANES)
    seg = (lane[:, None] // RW_HEAD_DIM == lane[None, :] // RW_HEAD_DIM).astype(BF16)
    tri = lambda n: jnp.tile((jnp.arange(n)[:, None] >= jnp.arange(n)[None, :]).astype(BF16), (1, 3))
    tri_p, tri_s = tri(WKV_CHUNK), tri(ts)
    zl = jnp.zeros((DECAY_LORA, RW_WIDTH), F32)
    row = lambda a: a.reshape(1, -1)
    zero = lambda *s: jnp.zeros(s, x_prompt.dtype)
    wkv_z, shift_z, conv_z = zero(bp, RW_HEADS, RW_HEAD_DIM, RW_HEAD_DIM), zero(bp, 1, P_RW), zero(bp, CONV_W - 1, 2 * D_FF)

    xp, xs = x_prompt, x_sample
    kv_p = kv_s = None
    outs_p, outs_s = [], []
    nf = row(norm_final)
    big = {name: a.astype(BF16) for name, a in dict(w_in=w_in, w_o_da=w_o_da, w_o_rw=w_o_rw, w_out=w_out,
                                                     w_up=w_up, w_down=w_down).items()}
    for l in range(depth):
        w = dict(
            big, norm_mix=row(norm_mix[l]), da_lambda=da_lambda[l],
            da_subln=da_subln[l].reshape(-1, 1),
            rw_mu=row(rw_mu[l]), rw_w0=row(rw_w0[l]),
            rw_w2p=jnp.concatenate([rw_w2[l], zl], axis=0).astype(BF16),
            rw_a0=row(rw_a0[l]), rw_a2p=jnp.concatenate([zl, rw_a2[l]], axis=0).astype(BF16),
            rw_g2=rw_g2[l].astype(BF16), rw_k_k=row(rw_k_k[l]), rw_k_a=row(rw_k_a[l]),
            rw_r_k=row(rw_r_k[l]), rw_ln_w=row(rw_ln_w[l]), rw_ln_b=row(rw_ln_b[l]),
            norm_ffn=row(norm_ffn[l]), ffn_conv=ffn_conv[l], ffn_conv_b=row(ffn_conv_b[l]),
            seg=seg, tri_p=tri_p, tri_s=tri_s)
        final = l == depth - 1
        xp, kv_p, sp, shp, cp = _layer(xp, l, depth, tabs_p, kv_p, None, wkv_z, shift_z, conv_z, w, nf, final=final)
        xs, kv_s, sq, shq, cq = _layer(xs, l, depth, tabs_s, kv_s, cache, state_wkv[l], state_shift[l],
                                       state_ffn_conv[l], w, nf, final=final)
        outs_p.append((sp, shp, cp))
        outs_s.append((sq, shq, cq))

    wkv_prompt, shift_prompt, conv_prompt = [jnp.stack(t) for t in zip(*outs_p)]
    wkv_sample, shift_sample, conv_sample = [jnp.stack(t) for t in zip(*outs_s)]
    heads = lambda a, b, t: a.reshape(depth, b, t, DA_HEADS, DA_V_DIM)
    return (xp, xs, heads(kv_p[0], bp, tp), heads(kv_p[1], bp, tp), wkv_prompt, shift_prompt, conv_prompt,
            heads(kv_s[0], bs, ts), heads(kv_s[1], bs, ts), wkv_sample, shift_sample, conv_sample)
```

```python
import functools
import math

import jax
import jax.numpy as jnp
from jax import lax
from jax.experimental import pallas as pl
from jax.experimental.pallas import tpu as pltpu

F32 = jnp.float32
BF16 = jnp.bfloat16

D_MODEL = 1024
CHUNK = 64
EPS = 1e-6
DA_HEADS = 8
DA_HEAD_DIM = 64
DA_V_DIM = 128
DA_WIDTH = DA_HEADS * DA_V_DIM
ROPE_THETA = 500000.0
ROT_DIM = DA_HEAD_DIM // 4
RW_HEAD_DIM = 64
RW_HEADS = 16
RW_WIDTH = 1024
DECAY_LORA = 64
AAA_LORA = 64
GATE_LORA = 128
GN_EPS = 64e-5
D_FF = 2816
CONV_W = 3
P_DA = 3 * DA_WIDTH
P_RW = 3 * RW_WIDTH + DECAY_LORA + AAA_LORA + GATE_LORA
P_GATE = 2 * D_MODEL
P_TOTAL = P_DA + P_RW + P_GATE

LANES = 128
SUBLANES = 8
VMEM_LIMIT = 56 * 1024 * 1024

WKV_CHUNK = 64
N_PAIRS = RW_HEADS // 2
NEG_BIG = -1e30
ROW_TILE = 256
WKV_ROWS = 512
WKV_PAIRS = 4
CACHE_PARTS = 2

_ARB = "arbitrary"


def _cparams(n_axes):
    return pltpu.CompilerParams(dimension_semantics=(_ARB,) * n_axes,
                                vmem_limit_bytes=VMEM_LIMIT)


def _resident(shape):
    nd = len(shape)
    return pl.BlockSpec(shape, lambda *_: (0,) * nd, pipeline_mode=pl.Buffered(1))


def _layer_weight(shape, layer):
    nd = len(shape)
    return pl.BlockSpec((1,) + tuple(shape), lambda *_: (layer,) + (0,) * nd, pipeline_mode=pl.Buffered(1))


def _dot(a, b):
    return jnp.dot(a, b, preferred_element_type=F32)


def _dot_nt(a, b):
    return lax.dot_general(a, b, (((1,), (1,)), ((), ())), preferred_element_type=F32)


def _rms(x, g):
    return x * lax.rsqrt(jnp.mean(x * x, axis=-1, keepdims=True) + EPS) * g


def _sigmoid(x):
    return 1.0 / (1.0 + jnp.exp(-x))


def _split3(x):
    hi = x.astype(BF16)
    r1 = x - hi.astype(F32)
    mid = r1.astype(BF16)
    lo = (r1 - mid.astype(F32)).astype(BF16)
    return hi, mid, lo


def _prep_rows(us, w0_ref, w2_ref, a0_ref, a2_ref, g2_ref, kk_ref, ka_ref, seg_ref):
    r = us[:, 0:RW_WIDTH]
    kr = us[:, RW_WIDTH:2 * RW_WIDTH]
    vr = us[:, 2 * RW_WIDTH:3 * RW_WIDTH]
    wa = us[:, 3 * RW_WIDTH:3 * RW_WIDTH + DECAY_LORA + AAA_LORA]
    gd = us[:, 3 * RW_WIDTH + DECAY_LORA + AAA_LORA:]

    z = w0_ref[...] + _dot(jnp.tanh(wa).astype(BF16), w2_ref[...])
    lw = -math.exp(-0.5) * _sigmoid(z)
    a = _sigmoid(a0_ref[...] + _dot(wa.astype(BF16), a2_ref[...]))
    g = _dot(_sigmoid(gd).astype(BF16), g2_ref[...])

    kk = kr * kk_ref[...]
    k2 = (kk * kk).astype(BF16)
    seg = seg_ref[...]
    w = seg.shape[0]
    ssum = jnp.concatenate([_dot(k2[:, c:c + w], seg) for c in range(0, RW_WIDTH, w)],
                           axis=1)
    kk = kk * lax.rsqrt(jnp.maximum(ssum, 1e-24))
    kmod = kr * (1.0 + (a - 1.0) * ka_ref[...])
    return r, vr, g, lw, a, kk, kmod


def _prep_chunk(r, lw, a, kk, kmod, tri):
    cum = _dot(tri, jnp.concatenate(_split3(lw), axis=0))
    e_pos = jnp.exp(cum)
    e_neg = jnp.exp(-cum)
    rt = r * e_pos
    at = -kk * jnp.exp(cum - lw)
    bt = kk * a * e_neg
    kt = kmod * e_neg
    return rt, at, bt, kt, e_pos[-1:, :]


def _rw_prep_kernel(u_ref, sh_ref, mu_ref, w0_ref, w2_ref, a0_ref, a2_ref, g2_ref,
                    kk_ref, ka_ref, seg_ref, tri_ref,
                    rt_ref, at_ref, bt_ref, kt_ref, v_ref, g_ref, wc_ref, *, tm):
    u = u_ref[0]
    rows = lax.broadcasted_iota(jnp.int32, (tm, 1), 0)
    up = jnp.where(rows == 0, sh_ref[0], pltpu.roll(u, 1, 0))
    us = u + (up - u) * mu_ref[...]
    r, vr, g, lw, a, kk, kmod = _prep_rows(us, w0_ref, w2_ref, a0_ref, a2_ref, g2_ref, kk_ref, ka_ref, seg_ref)
    rt, at, bt, kt, wc = _prep_chunk(r, lw, a, kk, kmod, tri_ref[...])
    for ref, val in ((rt_ref, rt), (at_ref, at), (bt_ref, bt), (kt_ref, kt), (v_ref, vr), (g_ref, g)):
        ref[0] = val.astype(BF16)
    wc_ref[0, 0] = wc


def _prep_param_specs(chunk):
    return [_resident((1, P_RW)), _resident((1, RW_WIDTH)),
            _resident((DECAY_LORA + AAA_LORA, RW_WIDTH)), _resident((1, RW_WIDTH)),
            _resident((DECAY_LORA + AAA_LORA, RW_WIDTH)), _resident((GATE_LORA, RW_WIDTH)),
            _resident((1, RW_WIDTH)), _resident((1, RW_WIDTH)),
            _resident((2 * LANES, 2 * LANES)), _resident((chunk, 3 * chunk))]


def _rw_prep(u, shift0, prep, *, tm):
    bx, t, _ = u.shape
    assert t == tm
    row = lambda w: pl.BlockSpec((1, tm, w), lambda b: (b, 0, 0))
    outs = [jax.ShapeDtypeStruct((bx, t, RW_WIDTH), BF16)] * 6
    outs.append(jax.ShapeDtypeStruct((bx, 1, 1, RW_WIDTH), F32))
    return pl.pallas_call(
        functools.partial(_rw_prep_kernel, tm=tm),
        grid=(bx,),
        in_specs=[row(P_RW), pl.BlockSpec((1, 1, P_RW), lambda b: (b, 0, 0))] + _prep_param_specs(tm),
        out_specs=[row(RW_WIDTH)] * 6 + [pl.BlockSpec((1, 1, 1, RW_WIDTH), lambda b: (b, 0, 0, 0))],
        out_shape=outs,
        compiler_params=_cparams(1),
        name="rw_prep",
    )(u, shift0, *prep)


_IN_CW = 256
_LOG2E = math.log2(math.e)
_N_PREP = 10


def _in_proj_kernel(*refs, tm, fused, aliased):
    x_ref, g_ref, w_ref, rc_ref, ra_ref, rb_ref = refs[:6]
    w_ref = w_ref.at[0]
    n_in = 6
    if fused:
        sh_ref = refs[6]
        mu_ref, w0_ref, w2_ref, a0_ref, a2_ref, g2_ref, kk_ref, ka_ref, seg_ref, tri_ref = refs[7:7 + _N_PREP]
        n_in = 7 + _N_PREP
    n_in += 2 if aliased else 0
    q_ref, kn_ref, vn_ref, ko_ref, vo_ref, gt_ref = refs[n_in:n_in + 6]
    if fused:
        rt_ref, at_ref, bt_ref, kt_ref, vr_ref, gr_ref, wc_ref, last_ref = refs[n_in + 6:n_in + 14]
        k_scr, v_scr, u_s = refs[n_in + 14:]
    else:
        u_ref = refs[n_in + 6]
        k_scr, v_scr = refs[n_in + 7:]
    hb = _rms(x_ref[0], g_ref[...]).astype(BF16)
    rc = rc_ref[...]
    ra = ra_ref[...]
    rb = rb_ref[...]

    def rope(t):
        return t * rc + pltpu.roll(t, ROT_DIM // 2, 1) * ra + pltpu.roll(t, LANES - ROT_DIM // 2, 1) * rb

    def chunk(c0):
        if P_DA <= c0 < P_DA + P_RW:
            o = c0 - P_DA
            if fused:
                u_s[SUBLANES:, o:o + _IN_CW] = _dot(hb, w_ref[:, c0:c0 + _IN_CW])
            else:
                u_ref[0, :, o:o + _IN_CW] = _dot(hb, w_ref[:, c0:c0 + _IN_CW])
            return
        acc = _dot(hb, w_ref[:, c0:c0 + _IN_CW])
        if c0 < DA_WIDTH:
            for s in range(0, _IN_CW, LANES):
                q_ref[0, :, c0 + s:c0 + s + LANES] = (
                    rope(acc[:, s:s + LANES]) * (DA_HEAD_DIM ** -0.5 * _LOG2E)).astype(BF16)
        elif c0 < 2 * DA_WIDTH:
            o = c0 - DA_WIDTH
            for s in range(0, _IN_CW, LANES):
                kr = rope(acc[:, s:s + LANES])
                k_scr[:, o + s:o + s + LANES] = kr
                kn_ref[0, :, o + s:o + s + LANES] = kr.astype(BF16)
        elif c0 < P_DA:
            o = c0 - 2 * DA_WIDTH
            v_scr[:, o:o + _IN_CW] = acc
            vn_ref[0, :, o:o + _IN_CW] = acc.astype(BF16)
        else:
            o = c0 - P_DA - P_RW
            gt_ref[0, :, o:o + _IN_CW] = acc.astype(BF16)

    def chunks(lo, hi):
        for c0 in range(lo, hi, _IN_CW):
            chunk(c0)

    if not fused:
        chunks(0, P_TOTAL)
    else:
        @pl.when(pl.program_id(1) == 0)
        def _():
            u_s[SUBLANES - 1:SUBLANES, :] = sh_ref[0]

        chunks(0, P_DA + P_RW)
        u = u_s[SUBLANES:SUBLANES + tm, :]
        up = u_s[SUBLANES - 1:SUBLANES - 1 + tm, :]
        us = u + (up - u) * mu_ref[...]
        last = u[tm - 1:tm, :]
        last_ref[0] = last
        u_s[SUBLANES - 1:SUBLANES, :] = last
        r, vr, g, lw, a, kk, kmod = _prep_rows(us, w0_ref, w2_ref, a0_ref, a2_ref, g2_ref, kk_ref, ka_ref, seg_ref)
        vr_ref[0] = vr.astype(BF16)
        gr_ref[0] = g.astype(BF16)
        chunks(P_DA + P_RW, P_DA + P_RW + P_GATE // 2)
        tri = tri_ref[...]
        for c in range(tm // WKV_CHUNK):
            sl = slice(c * WKV_CHUNK, (c + 1) * WKV_CHUNK)
            rt, at, bt, kt, wc = _prep_chunk(r[sl], lw[sl], a[sl], kk[sl], kmod[sl], tri)
            for ref, val in ((rt_ref, rt), (at_ref, at), (bt_ref, bt), (kt_ref, kt)):
                ref[0, sl, :] = val.astype(BF16)
            wc_ref[0, c] = wc
        chunks(P_DA + P_RW + P_GATE // 2, P_TOTAL)
    ko_ref[0, 0] = k_scr[...].reshape(tm, DA_HEADS, DA_V_DIM)
    vo_ref[0, 0] = v_scr[...].reshape(tm, DA_HEADS, DA_V_DIM)


def _in_proj(x, g, w_bf, tabs, kvout, *, layer, depth, tm, fused, shift0=None, prep=None):
    bx, t, _ = x.shape
    row = lambda w: pl.BlockSpec((1, tm, w), lambda b, i: (b, i, 0))
    tab = pl.BlockSpec((tm, LANES), lambda b, i: (i, 0))
    slot = pl.BlockSpec((1, 1, tm, DA_HEADS, DA_V_DIM), lambda b, i: (layer, b, i, 0, 0))
    aliased = kvout is not None
    in_specs = [row(D_MODEL), _resident((1, D_MODEL)), _layer_weight((D_MODEL, P_TOTAL), layer), tab, tab, tab]
    args = [x, g, w_bf, *tabs]
    if fused:
        in_specs += [pl.BlockSpec((1, 1, P_RW), lambda b, i: (b, 0, 0))] + _prep_param_specs(WKV_CHUNK)
        args += [shift0, *prep]
    if aliased:
        aliases = {len(args): 3, len(args) + 1: 4}
        in_specs += [pl.BlockSpec(memory_space=pl.ANY)] * 2
        args += list(kvout)
    else:
        aliases = {}
    nat = jax.ShapeDtypeStruct((bx, t, DA_WIDTH), BF16)
    stacked = jax.ShapeDtypeStruct((depth, bx, t, DA_HEADS, DA_V_DIM), F32)
    out_specs = [row(DA_WIDTH)] * 3 + [slot, slot, row(P_GATE)]
    out_shape = [nat, nat, nat, stacked, stacked, jax.ShapeDtypeStruct((bx, t, P_GATE), BF16)]
    scratch = [pltpu.VMEM((tm, DA_WIDTH), F32)] * 2
    if fused:
        nc = tm // WKV_CHUNK
        out_specs += [row(RW_WIDTH)] * 6 + [pl.BlockSpec((1, nc, 1, RW_WIDTH), lambda b, i: (b, i, 0, 0)),
                                            pl.BlockSpec((1, 1, P_RW), lambda b, i: (b, 0, 0))]
        out_shape += [jax.ShapeDtypeStruct((bx, t, RW_WIDTH), BF16)] * 6
        out_shape += [jax.ShapeDtypeStruct((bx, t // WKV_CHUNK, 1, RW_WIDTH), F32),
                      jax.ShapeDtypeStruct((bx, 1, P_RW), F32)]
        scratch.append(pltpu.VMEM((SUBLANES + tm, P_RW), F32))
    else:
        out_specs.append(row(P_RW))
        out_shape.append(jax.ShapeDtypeStruct((bx, t, P_RW), F32))
    return pl.pallas_call(
        functools.partial(_in_proj_kernel, tm=tm, fused=fused, aliased=aliased),
        grid=(bx, t // tm),
        in_specs=in_specs,
        out_specs=out_specs,
        out_shape=out_shape,
        scratch_shapes=scratch,
        input_output_aliases=aliases,
        compiler_params=_cparams(2),
        name="in_proj",
    )(*args)


_TK = 256
_CHUNK_SHIFT = CHUNK.bit_length() - 1


def _lambda(lam_ref, lam_init):
    lp = lam_ref[...]
    return (jnp.exp(jnp.sum(lp[0:1] * lp[1:2], axis=1, keepdims=True))
            - jnp.exp(jnp.sum(lp[2:3] * lp[3:4], axis=1, keepdims=True)) + lam_init)


def _q_blockdiag(q_tile):
    tq = q_tile.shape[0]
    qt = q_tile.astype(F32).T.astype(BF16)
    z = jnp.zeros((DA_HEAD_DIM, tq), BF16)
    return jnp.concatenate([jnp.concatenate([qt[:DA_HEAD_DIM], z], axis=1),
                            jnp.concatenate([z, qt[DA_HEAD_DIM:]], axis=1)], axis=0)


def _softmax_step(s, stats, mask, tq):
    new_stats, alphas, ps = [], [], []
    for c in range(2):
        m, l = stats[2 * c:2 * c + 2]
        sc = s[:, c * tq:(c + 1) * tq]
        if mask is not None:
            sc = jnp.where(mask, sc, NEG_BIG)
        m_new = jnp.maximum(m, jnp.max(sc, axis=0, keepdims=True))
        alpha = jnp.exp2(m - m_new)
        p = jnp.exp2(sc - m_new)
        new_stats += [m_new, alpha * l + jnp.sum(p, axis=0, keepdims=True)]
        alphas.append(alpha)
        ps.append(p.astype(BF16))
    return tuple(new_stats), tuple(alphas), tuple(ps)


def _attn_finish(a1, a2, l1, l2, lam, sub_col, lam_init):
    ot = a1 * (1.0 / l1) - lam * (a2 * (1.0 / l2))
    ss = jnp.sum(ot * ot, axis=0, keepdims=True) * (1.0 / DA_V_DIM)
    ot = ot * lax.rsqrt(ss + EPS) * sub_col * (1.0 - lam_init)
    return ot.T.astype(BF16)


def _attn_prompt_kernel(lam_ref, sub_ref, q_ref, k_ref, v_ref, o_ref, vt_s, s_s, *, n_tiles, lam_init):
    tq = _TK
    for jb in range(n_tiles):
        vt_s[jb] = v_ref[0, jb * _TK:(jb + 1) * _TK, :].astype(F32).T.astype(BF16)
    lam = _lambda(lam_ref, lam_init)
    sub_col = sub_ref[...]
    kr = jnp.right_shift(lax.broadcasted_iota(jnp.int32, (_TK, 2 * tq), 0), _CHUNK_SHIFT)
    qc = jnp.right_shift(lax.broadcasted_iota(jnp.int32, (_TK, 2 * tq), 1) & (tq - 1), _CHUNK_SHIFT)
    diag_mask = kr <= qc

    def new_tile(i):
        return dict(qbd=_q_blockdiag(q_ref[0, i * tq:(i + 1) * tq, :]), mx=None)

    def scores(i, j, tile):
        s = _dot(k_ref[0, j * _TK:(j + 1) * _TK, :], tile["qbd"])
        if j == i:
            s = jnp.where(diag_mask, s, NEG_BIG)
        s_s[i % 2, j] = s
        part = jnp.max(s.reshape(_TK // SUBLANES, SUBLANES, 2 * tq), axis=0)
        tile["mx"] = part if tile["mx"] is None else jnp.maximum(tile["mx"], part)

    cur = new_tile(0)
    scores(0, 0, cur)
    for i in range(n_tiles):
        m = jnp.max(cur["mx"], axis=0, keepdims=True)
        nxt = new_tile(i + 1) if i + 1 < n_tiles else None
        acc, lsum, pending = [None, None], None, []

        def values(jp, pb):
            for c in range(2):
                pv = _dot(vt_s[jp], pb[:, c * tq:(c + 1) * tq])
                acc[c] = pv if acc[c] is None else acc[c] + pv

        for j in range(i + 1):
            if nxt is not None:
                scores(i + 1, j, nxt)
            p = jnp.exp2(s_s[i % 2, j] - m)
            part = jnp.sum(p.reshape(_TK // SUBLANES, SUBLANES, 2 * tq), axis=0)
            lsum = part if lsum is None else lsum + part
            if len(pending) == 2:
                values(*pending.pop(0))
            pending.append((j, p.astype(BF16)))
        if nxt is not None:
            scores(i + 1, i + 1, nxt)
        for item in pending:
            values(*item)
        l = jnp.sum(lsum, axis=0, keepdims=True)
        o_ref[0, i * tq:(i + 1) * tq, :] = _attn_finish(acc[0], acc[1], l[:, :tq], l[:, tq:], lam, sub_col,
                                                         lam_init)
        cur = nxt


def _attn_prompt(da_lambda, subln_col, q, k, v, *, lam_init):
    bx, t, _ = q.shape
    n_tiles = t // _TK
    seq = pl.BlockSpec((1, t, LANES), lambda b, h: (b, 0, h))
    return pl.pallas_call(
        functools.partial(_attn_prompt_kernel, n_tiles=n_tiles, lam_init=lam_init),
        grid=(bx, DA_HEADS),
        in_specs=[_resident((4, DA_HEAD_DIM)), _resident((DA_V_DIM, 1)), seq, seq, seq],
        out_specs=seq,
        out_shape=jax.ShapeDtypeStruct((bx, t, DA_WIDTH), BF16),
        scratch_shapes=[pltpu.VMEM((n_tiles, DA_V_DIM, _TK), BF16),
                        pltpu.VMEM((2, n_tiles, _TK, 2 * _TK), F32)],
        compiler_params=_cparams(2),
        name="diff_attn",
    )(da_lambda, subln_col, q, k, v)


def _attn_sample_kernel(lam_ref, sub_ref, q_ref, k_ref, v_ref, kp_ref, vp_ref, o_ref, kb_s, vb_s,
                        *, tq, n_new, n_past, n_parts, n_valid_new, lam_init):
    part = pl.program_id(1)
    rows_part = n_past // n_parts
    for r0 in range(0, rows_part, _TK):
        dst = pl.ds(pl.multiple_of(part * rows_part + r0, _TK), _TK)
        kb_s[dst, :] = kp_ref[0, 0, r0:r0 + _TK].reshape(_TK, DA_WIDTH).astype(BF16)
        vb_s[dst, :] = vp_ref[0, 0, r0:r0 + _TK].reshape(_TK, DA_WIDTH).astype(BF16)

    @pl.when(part == n_parts - 1)
    def _():
        _attn_sample_heads(lam_ref, sub_ref, q_ref, k_ref, v_ref, o_ref, kb_s, vb_s, tq=tq, n_new=n_new,
                           n_past=n_past, n_valid_new=n_valid_new, lam_init=lam_init)


def _attn_sample_heads(lam_ref, sub_ref, q_ref, k_ref, v_ref, o_ref, kb_s, vb_s,
                       *, tq, n_new, n_past, n_valid_new, lam_init):
    assert 2 * tq == LANES
    lam = _lambda(lam_ref, lam_init)
    sub_col = sub_ref[...]
    valid = lax.broadcasted_iota(jnp.int32, (n_new, LANES), 0) < n_valid_new
    kb_s[n_past:n_past + n_new, :] = k_ref[0]
    vb_s[n_past:n_past + n_new, :] = v_ref[0]
    bounds = [(r0, _TK, None) for r0 in range(0, n_past, _TK)] + [(n_past, n_new, valid)]
    own_map = (lax.broadcasted_iota(jnp.int32, (2 * tq, LANES), 0) < tq) == (
        lax.broadcasted_iota(jnp.int32, (2 * tq, LANES), 1) < DA_HEAD_DIM)
    for h in range(DA_HEADS):
        lanes = slice(h * LANES, (h + 1) * LANES)
        q = q_ref[0, :, lanes]
        qq = jnp.concatenate([q, q], axis=0)
        qbd = jnp.where(own_map, qq, jnp.zeros_like(qq)).astype(F32).T.astype(BF16)
        scores = []
        for r0, n, mask in bounds:
            s = _dot(kb_s[r0:r0 + n, lanes], qbd)
            scores.append(s if mask is None else jnp.where(mask, s, NEG_BIG))
        m = functools.reduce(jnp.maximum, [jnp.max(s, axis=0, keepdims=True) for s in scores])
        l, acc = None, None
        for s, (r0, n, _) in zip(scores, bounds):
            p = jnp.exp2(s - m)
            pl_ = jnp.sum(p, axis=0, keepdims=True)
            pv = _dot(vb_s[r0:r0 + n, lanes].astype(F32).T.astype(BF16), p.astype(BF16))
            l, acc = (pl_, pv) if l is None else (l + pl_, acc + pv)
        r = acc * (1.0 / l)
        ot = r - lam * pltpu.roll(r, tq, 1)
        ss = jnp.sum(ot * ot, axis=0, keepdims=True) * (1.0 / DA_V_DIM)
        ot = ot * lax.rsqrt(ss + EPS) * sub_col * (1.0 - lam_init)
        o_ref[0, :, lanes] = ot.T[:tq].astype(BF16)


def _attn_sample(da_lambda, subln_col, q, k, v, past_k, past_v, *, layer, lam_init, n_valid_new):
    bx, tq, _ = q.shape
    n_new = k.shape[1]
    n_past = past_k.shape[2]
    n_parts = CACHE_PARTS
    new = pl.BlockSpec((1, n_new, DA_WIDTH), lambda b, j: (b, 0, 0))
    old = pl.BlockSpec((1, 1, n_past // n_parts, DA_HEADS, DA_V_DIM), lambda b, j: (layer, b, j, 0, 0))
    return pl.pallas_call(
        functools.partial(_attn_sample_kernel, tq=tq, n_new=n_new, n_past=n_past, n_parts=n_parts,
                          n_valid_new=n_valid_new, lam_init=lam_init),
        grid=(bx, n_parts),
        in_specs=[_resident((4, DA_HEAD_DIM)), _resident((DA_V_DIM, 1)),
                  pl.BlockSpec((1, tq, DA_WIDTH), lambda b, j: (b, 0, 0)), new, new, old, old],
        out_specs=pl.BlockSpec((1, tq, DA_WIDTH), lambda b, j: (b, 0, 0)),
        out_shape=jax.ShapeDtypeStruct((bx, tq, DA_WIDTH), BF16),
        scratch_shapes=[pltpu.VMEM((n_past + n_new, DA_WIDTH), BF16)] * 2,
        compiler_params=_cparams(2),
        name="diff_attn_step",
    )(da_lambda, subln_col, q, k, v, past_k, past_v)


def _wkv_kernel(rt_ref, at_ref, bt_ref, kt_ref, v_ref, g_ref, wc_ref, s0_ref, rk_ref, lnw_ref, lnb_ref,
                o_ref, sout_ref, s_scr, rp_s, m_s, y0_s, gc_s, bon_s, *, n_pairs, n_chunks):
    C = WKV_CHUNK
    R = 2 * C
    t = pl.program_id(2)

    @pl.when(t == 0)
    def _():
        s_scr[...] = s0_ref[0]

    ri = lax.broadcasted_iota(jnp.int32, (R, R), 0)
    ci = lax.broadcasted_iota(jnp.int32, (R, R), 1)
    sh = lambda x, sz: jnp.right_shift(x, sz.bit_length() - 1)
    same_head = sh(ri, C) == sh(ci, C)
    strict = same_head & (ci < ri)
    incl = same_head & (ci <= ri)
    eye = ri == ci

    def bd(sz):
        return sh(ri, sz) == sh(ci, sz)

    lane_lo = lax.broadcasted_iota(jnp.int32, (C, LANES), 1) < RW_HEAD_DIM

    def stack(x):
        z = jnp.zeros_like(x)
        return jnp.concatenate([jnp.where(lane_lo, x, z), jnp.where(lane_lo, z, x)], axis=0)

    b16 = lambda x: x.astype(BF16)
    f32 = lambda x: x.astype(F32)
    ident = jnp.where(eye, 1.0, 0.0)

    units = [(p, c) for p in range(n_pairs) for c in range(n_chunks)]
    idx = range(len(units))
    rs, asb, bsb, ksb, vsb, bdk, kdk, wcs = [], [], [], [], [], [], [], []
    for u, (p, c) in enumerate(units):
        rows, lanes = slice(c * C, (c + 1) * C), slice(p * LANES, (p + 1) * LANES)
        r_, a_, b_, k_, v_ = (stack(ref[0, rows, lanes]) for ref in (rt_ref, at_ref, bt_ref, kt_ref, v_ref))
        wc = wc_ref[0, c, :, lanes]
        rk = rk_ref[:, lanes]
        bon_s[u] = jnp.sum(f32(r_) * f32(k_) * rk, axis=1, keepdims=True) * f32(v_)
        rs.append(r_); asb.append(a_); bsb.append(b_); ksb.append(k_); vsb.append(v_)
        bdk.append(b16(f32(b_) * wc)); kdk.append(b16(f32(k_) * wc)); wcs.append(wc)

    n, n8, aak, arb, ark = [], [], [], [], []
    for u in idx:
        ab = _dot_nt(jnp.concatenate([asb[u], rs[u]], axis=0), jnp.concatenate([bsb[u], ksb[u]], axis=0))
        nc = jnp.where(strict, ab[:R, :R], 0.0)
        n.append(b16(nc)); n8.append(jnp.where(bd(8), nc, 0.0))
        aak.append(b16(jnp.where(strict, ab[:R, R:], 0.0)))
        arb.append(b16(jnp.where(incl, ab[R:, :R], 0.0)))
        ark.append(b16(jnp.where(incl, ab[R:, R:], 0.0)))

    n8b = [b16(x) for x in n8]
    n2 = [_dot(n8b[u], n8b[u]) for u in idx]
    n2b = [b16(x) for x in n2]
    n4 = [_dot(n2b[u], n2b[u]) for u in idx]
    p1 = [_dot(b16(ident + n8[u]), b16(ident + n2[u])) for u in idx]
    tinv = [_dot(b16(p1[u]), b16(ident + n4[u])) for u in idx]
    zero16 = jnp.zeros((R, R), BF16)
    for sz in (8, 16, 32):
        off = bd(2 * sz) & jnp.logical_not(bd(sz))
        tb = [b16(x) for x in tinv]
        x = [_dot(tb[u], jnp.where(off, n[u], zero16)) for u in idx]
        tinv = [tinv[u] + _dot(b16(x[u]), tb[u]) for u in idx]
    tb = [b16(x) for x in tinv]

    u0p = [_dot(aak[u], vsb[u]) for u in idx]
    tx = [_dot(tb[u], jnp.concatenate([asb[u], b16(u0p[u])], axis=1)) for u in idx]
    txb = [b16(x) for x in tx]
    az = [_dot(arb[u], txb[u]) for u in idx]
    for u in idx:
        rp_s[u] = b16(f32(rs[u]) + az[u][:, :LANES])
        y0_s[u] = az[u][:, LANES:] + _dot(ark[u], vsb[u])
    for u in idx:
        txt = b16(tx[u].T)
        mg = _dot(txt, bdk[u])
        m_s[u] = b16(jnp.where(eye, wcs[u], 0.0) + mg[:LANES])
        gc_s[u] = mg[LANES:] + _dot(b16(f32(vsb[u]).T), kdk[u])

    s = [s_scr[p] for p in range(n_pairs)]
    for c in range(n_chunks):
        for p in range(n_pairs):
            u = p * n_chunks + c
            rows, lanes = slice(c * C, (c + 1) * C), slice(p * LANES, (p + 1) * LANES)
            sb = b16(s[p])
            y = _dot_nt(rp_s[u], sb) + y0_s[u]
            s[p] = _dot(sb, m_s[u]) + gc_s[u]
            mu = jnp.sum(y, axis=1, keepdims=True) * (1.0 / RW_HEAD_DIM)
            d = jnp.where(same_head, y - mu, 0.0)
            var = jnp.sum(d * d, axis=1, keepdims=True) * (1.0 / RW_HEAD_DIM)
            ost = (d * lax.rsqrt(var + GN_EPS) * lnw_ref[:, lanes]
                   + jnp.where(same_head, lnb_ref[:, lanes], 0.0) + bon_s[u])
            o_ref[0, rows, lanes] = ((ost[:C] + ost[C:]) * f32(g_ref[0, rows, lanes])).astype(BF16)
    for p in range(n_pairs):
        s_scr[p] = s[p]
        sout_ref[0, p] = s[p]


def _wkv(rt, at, bt, kt, v, g, wc, s0bd, rk, lnw, lnb, *, tb, n_pairs):
    bx, t, _ = rt.shape
    n_chunks = tb // WKV_CHUNK
    n_units = n_pairs * n_chunks
    w = n_pairs * LANES
    seq = pl.BlockSpec((1, tb, w), lambda b, p, i: (b, i, p))
    vec = pl.BlockSpec((1, w), lambda b, p, i: (0, p))
    st = pl.BlockSpec((1, n_pairs, LANES, LANES), lambda b, p, i: (b, p, 0, 0))
    mat = lambda dt: pltpu.VMEM((n_units, LANES, LANES), dt)
    return pl.pallas_call(
        functools.partial(_wkv_kernel, n_pairs=n_pairs, n_chunks=n_chunks),
        grid=(bx, N_PAIRS // n_pairs, t // tb),
        in_specs=[seq] * 6 + [pl.BlockSpec((1, n_chunks, 1, w), lambda b, p, i: (b, i, 0, p)),
                              st, vec, vec, vec],
        out_specs=[seq, st],
        out_shape=[jax.ShapeDtypeStruct((bx, t, RW_WIDTH), BF16),
                   jax.ShapeDtypeStruct((bx, N_PAIRS, LANES, LANES), F32)],
        scratch_shapes=[pltpu.VMEM((n_pairs, LANES, LANES), F32),
                        mat(BF16), mat(BF16), mat(F32), mat(F32), mat(F32)],
        compiler_params=_cparams(3),
        name="wkv",
    )(rt, at, bt, kt, v, g, wc, s0bd, rk, lnw, lnb)


def _merge_kernel(x_ref, oda_ref, orw_ref, gt_ref, wda_ref, wrw_ref, wout_ref, o_ref):
    a = _dot(oda_ref[0], wda_ref[0])
    b = _dot(orw_ref[0], wrw_ref[0])
    gt = gt_ref[0].astype(F32)
    m = _sigmoid(gt[:, :D_MODEL]) * a + _sigmoid(gt[:, D_MODEL:]) * b
    o_ref[0] = x_ref[0] + _dot(m.astype(BF16), wout_ref[0])


def _merge(x, oda, orw, gt, wda, wrw, wout, *, tm, layer):
    bx, t, _ = x.shape
    row = lambda w: pl.BlockSpec((1, tm, w), lambda b, i: (b, i, 0))
    sq = _layer_weight((D_MODEL, D_MODEL), layer)
    return pl.pallas_call(
        _merge_kernel,
        grid=(bx, t // tm),
        in_specs=[row(D_MODEL), row(DA_WIDTH), row(RW_WIDTH), row(P_GATE), sq, sq, sq],
        out_specs=row(D_MODEL),
        out_shape=jax.ShapeDtypeStruct((bx, t, D_MODEL), F32),
        compiler_params=_cparams(2),
        name="merge",
    )(x, oda, orw, gt, wda, wrw, wout)


_FF_CW = 256
_HALO = SUBLANES


def _ffn_kernel(*refs, tm, final, merge):
    if merge:
        x_ref, oda_ref, orw_ref, gt_ref, wda_ref, wrw_ref, wout_ref = refs[:7]
        refs = refs[7:]
        a = _dot(oda_ref[0], wda_ref[0])
        b = _dot(orw_ref[0], wrw_ref[0])
        gt = gt_ref[0].astype(F32)
        m = _sigmoid(gt[:, :D_MODEL]) * a + _sigmoid(gt[:, D_MODEL:]) * b
        x = x_ref[0] + _dot(m.astype(BF16), wout_ref[0])
    else:
        x = refs[0][0]
        refs = refs[1:]
    c0_ref, g_ref, wup_ref, f_ref, fb_ref, wdn_ref, gf_ref, o_ref, conv_ref, ext_s, act_s = refs
    wup_ref, wdn_ref = wup_ref.at[0], wdn_ref.at[0]

    @pl.when(pl.program_id(1) == 0)
    def _():
        ext_s[_HALO - 2:_HALO, :] = c0_ref[0]

    hb = _rms(x, g_ref[...]).astype(BF16)
    starts = list(range(0, D_FF, _FF_CW))

    def up_proj(c0):
        for base in (c0, D_FF + c0):
            cols = slice(base, base + _FF_CW)
            up = _dot(hb, wup_ref[:, cols])
            ext_s[_HALO:, cols] = up
            conv_ref[0, :, cols] = up[tm - 2:tm]

    def conv_act(c0):
        cs = []
        for base in (c0, D_FF + c0):
            cols = slice(base, base + _FF_CW)
            f = f_ref[:, cols]
            cs.append(fb_ref[:, cols]
                      + ext_s[_HALO - 2:_HALO - 2 + tm, cols] * f[0:1]
                      + ext_s[_HALO - 1:_HALO - 1 + tm, cols] * f[1:2]
                      + ext_s[_HALO:_HALO + tm, cols] * f[2:3])
            ext_s[_HALO - 2:_HALO, cols] = ext_s[_HALO + tm - 2:_HALO + tm, cols]
        ca, cb = cs
        act_s[:, c0:c0 + _FF_CW] = (ca * _sigmoid(ca) * cb).astype(BF16)

    up_proj(starts[0])
    for j, c0 in enumerate(starts):
        if j + 1 < len(starts):
            up_proj(starts[j + 1])
        conv_act(c0)
    out = x + _dot(act_s[...], wdn_ref[...])
    if final:
        out = _rms(out, gf_ref[...])
    o_ref[0] = out


def _ffn(x, conv0, g, wup, f, fb, wdn, gfinal, *, tm, final, layer, merge=None):
    bx, t, _ = x.shape
    row = lambda w: pl.BlockSpec((1, tm, w), lambda b, i: (b, i, 0))
    cst = pl.BlockSpec((1, CONV_W - 1, 2 * D_FF), lambda b, i: (b, 0, 0))
    in_specs, args = [row(D_MODEL)], [x]
    if merge is not None:
        sq = _layer_weight((D_MODEL, D_MODEL), layer)
        in_specs += [row(DA_WIDTH), row(RW_WIDTH), row(P_GATE), sq, sq, sq]
        args += list(merge)
    in_specs += [cst, _resident((1, D_MODEL)), _layer_weight((D_MODEL, 2 * D_FF), layer),
                 _resident((CONV_W, 2 * D_FF)), _resident((1, 2 * D_FF)), _layer_weight((D_FF, D_MODEL), layer),
                 _resident((1, D_MODEL))]
    args += [conv0, g, wup, f, fb, wdn, gfinal]
    return pl.pallas_call(
        functools.partial(_ffn_kernel, tm=tm, final=final, merge=merge is not None),
        grid=(bx, t // tm),
        in_specs=in_specs,
        out_specs=[row(D_MODEL), cst],
        out_shape=[jax.ShapeDtypeStruct((bx, t, D_MODEL), F32),
                   jax.ShapeDtypeStruct((bx, CONV_W - 1, 2 * D_FF), F32)],
        scratch_shapes=[pltpu.VMEM((_HALO + tm, 2 * D_FF), F32), pltpu.VMEM((tm, D_FF), BF16)],
        compiler_params=_cparams(2),
        name="ffn",
    )(*args)


def _rope_tables(pos):
    half = ROT_DIM // 2
    inv = ROPE_THETA ** (-jnp.arange(0, ROT_DIM, 2, dtype=F32) / ROT_DIM)
    ang = pos.astype(F32)[:, None] * inv[None, :]
    cos, sin = jnp.cos(ang), jnp.sin(ang)
    t = pos.shape[0]
    pad = jnp.zeros((t, DA_HEAD_DIM - ROT_DIM), F32)
    z = jnp.zeros((t, half), F32)
    one_map = lambda a, b, fill: jnp.concatenate([a, b, pad + fill], axis=1)
    rc = one_map(cos, cos, 1.0)
    ra = one_map(z, sin, 0.0)
    rb = one_map(-sin, z, 0.0)
    dup = lambda m: jnp.concatenate([m, m], axis=1)
    return dup(rc), dup(ra), dup(rb)


def _state_to_blockdiag(s):
    b = s.shape[0]
    s = s.reshape(b, N_PAIRS, 2, RW_HEAD_DIM, RW_HEAD_DIM)
    z = jnp.zeros_like(s[:, :, 0])
    top = jnp.concatenate([s[:, :, 0], z], axis=-1)
    bot = jnp.concatenate([z, s[:, :, 1]], axis=-1)
    return jnp.concatenate([top, bot], axis=-2)


def _blockdiag_to_state(sbd):
    b = sbd.shape[0]
    h0 = sbd[:, :, :RW_HEAD_DIM, :RW_HEAD_DIM]
    h1 = sbd[:, :, RW_HEAD_DIM:, RW_HEAD_DIM:]
    return jnp.stack([h0, h1], axis=2).reshape(b, RW_HEADS, RW_HEAD_DIM, RW_HEAD_DIM)


def _layer(x, lidx, depth, tabs, kvbuf, cache, wkv0, shift0, conv0, w, norm_final, *, final):
    bx, t, _ = x.shape
    prompt = cache is None
    lam_init = 0.8 - 0.6 * math.exp(-0.3 * lidx)
    prep = lambda tri: (w["rw_mu"], w["rw_w0"], w["rw_w2p"], w["rw_a0"], w["rw_a2p"], w["rw_g2"],
                        w["rw_k_k"], w["rw_k_a"], w["seg"], tri)

    if prompt:
        tm = ROW_TILE
        q, k, v, kout, vout, gate, rt, at, bt, kt, vr, g, wc, new_shift = _in_proj(
            x, w["norm_mix"], w["w_in"], tabs, kvbuf, layer=lidx, depth=depth, tm=tm, fused=True,
            shift0=shift0, prep=prep(w["tri_p"]))
        o_da = _attn_prompt(w["da_lambda"], w["da_subln"], q, k, v, lam_init=lam_init)
    else:
        tm = t
        flat = lambda a: a.reshape(1, bx * t, a.shape[-1])
        q, k, v, kout, vout, gate, u_rw = _in_proj(flat(x), w["norm_mix"], w["w_in"], tabs, kvbuf,
                                                   layer=lidx, depth=depth, tm=bx * t, fused=False)
        u_rw = u_rw.reshape(bx, t, P_RW)
        new_shift = u_rw[:, -1:]
        pad_rows = lambda a, n: jnp.pad(a.reshape(bx, t, a.shape[-1]), ((0, 0), (0, n - t), (0, 0)))
        o_da = _attn_sample(w["da_lambda"], w["da_subln"], pad_rows(q, LANES // 2), pad_rows(k, LANES),
                            pad_rows(v, LANES), *cache, layer=lidx, lam_init=lam_init, n_valid_new=t)[:, :t]
        rt, at, bt, kt, vr, g, wc = _rw_prep(u_rw, shift0, prep(w["tri_s"]), tm=t)
        rt, at, bt, kt, vr, g = [pad_rows(a, WKV_CHUNK) for a in (rt, at, bt, kt, vr, g)]
    o_rw, sbd = _wkv(rt, at, bt, kt, vr, g, wc, _state_to_blockdiag(wkv0),
                     w["rw_r_k"], w["rw_ln_w"], w["rw_ln_b"],
                     tb=WKV_ROWS if prompt else WKV_CHUNK, n_pairs=WKV_PAIRS if prompt else N_PAIRS)
    o_rw = o_rw[:, :t]

    mix = (o_da, o_rw, gate, w["w_o_da"], w["w_o_rw"], w["w_out"])
    if not prompt:
        x = _merge(flat(x), flat(o_da), flat(o_rw), *mix[2:], tm=bx * t, layer=lidx).reshape(bx, t, D_MODEL)
        mix = None
    x, new_conv = _ffn(x, conv0, w["norm_ffn"], w["w_up"], w["ffn_conv"], w["ffn_conv_b"], w["w_down"],
                       norm_final, tm=tm, final=final, layer=lidx, merge=mix)
    return x, (kout, vout), _blockdiag_to_state(sbd), new_shift, new_conv


def kernel(x_prompt, x_sample, cache_k, cache_v, state_wkv, state_shift, state_ffn_conv, norm_mix, w_in, da_lambda, da_subln, w_o_da, rw_mu, rw_w0, rw_w2, rw_a0, rw_a2, rw_g2, rw_k_k, rw_k_a, rw_r_k, rw_ln_w, rw_ln_b, w_o_rw, w_out, norm_ffn, w_up, ffn_conv, ffn_conv_b, w_down, norm_final):
    bp, tp, _ = x_prompt.shape
    bs, ts, _ = x_sample.shape
    depth = w_in.shape[0]
    past = cache_k.shape[2]
    tabs_p = _rope_tables(jnp.arange(tp, dtype=jnp.int32))
    tabs_s = tuple(jnp.tile(a, (bs, 1)) for a in _rope_tables(past + jnp.arange(ts, dtype=jnp.int32)))
    cache = (cache_k, cache_v)

    lane = jnp.arange(2 * LANES)
    seg = (lane[:, None] // RW_HEAD_DIM == lane[None, :] // RW_HEAD_DIM).astype(BF16)
    tri = lambda n: jnp.tile((jnp.arange(n)[:, None] >= jnp.arange(n)[None, :]).astype(BF16), (1, 3))
    tri_p, tri_s = tri(WKV_CHUNK), tri(ts)
    zl = jnp.zeros((DECAY_LORA, RW_WIDTH), F32)
    row = lambda a: a.reshape(1, -1)
    zero = lambda *s: jnp.zeros(s, x_prompt.dtype)
    wkv_z, shift_z, conv_z = zero(bp, RW_HEADS, RW_HEAD_DIM, RW_HEAD_DIM), zero(bp, 1, P_RW), zero(bp, CONV_W - 1, 2 * D_FF)

    xp, xs = x_prompt, x_sample
    kv_p = kv_s = None
    outs_p, outs_s = [], []
    nf = row(norm_final)
    big = {name: a.astype(BF16) for name, a in dict(w_in=w_in, w_o_da=w_o_da, w_o_rw=w_o_rw, w_out=w_out,
                                                     w_up=w_up, w_down=w_down).items()}
    for l in range(depth):
        w = dict(
            big, norm_mix=row(norm_mix[l]), da_lambda=da_lambda[l],
            da_subln=da_subln[l].reshape(-1, 1),
            rw_mu=row(rw_mu[l]), rw_w0=row(rw_w0[l]),
            rw_w2p=jnp.concatenate([rw_w2[l], zl], axis=0).astype(BF16),
            rw_a0=row(rw_a0[l]), rw_a2p=jnp.concatenate([zl, rw_a2[l]], axis=0).astype(BF16),
            rw_g2=rw_g2[l].astype(BF16), rw_k_k=row(rw_k_k[l]), rw_k_a=row(rw_k_a[l]),
            rw_r_k=row(rw_r_k[l]), rw_ln_w=row(rw_ln_w[l]), rw_ln_b=row(rw_ln_b[l]),
            norm_ffn=row(norm_ffn[l]), ffn_conv=ffn_conv[l], ffn_conv_b=row(ffn_conv_b[l]),
            seg=seg, tri_p=tri_p, tri_s=tri_s)
        final = l == depth - 1
        xp, kv_p, sp, shp, cp = _layer(xp, l, depth, tabs_p, kv_p, None, wkv_z, shift_z, conv_z, w, nf, final=final)
        xs, kv_s, sq, shq, cq = _layer(xs, l, depth, tabs_s, kv_s, cache, state_wkv[l], state_shift[l],
                                       state_ffn_conv[l], w, nf, final=final)
        outs_p.append((sp, shp, cp))
        outs_s.append((sq, shq, cq))

    wkv_prompt, shift_prompt, conv_prompt = [jnp.stack(t) for t in zip(*outs_p)]
    wkv_sample, shift_sample, conv_sample = [jnp.stack(t) for t in zip(*outs_s)]
    heads = lambda a, b, t: a.reshape(depth, b, t, DA_HEADS, DA_V_DIM)
    return (xp, xs, heads(kv_p[0], bp, tp), heads(kv_p[1], bp, tp), wkv_prompt, shift_prompt, conv_prompt,
            heads(kv_s[0], bs, ts), heads(kv_s[1], bs, ts), wkv_sample, shift_sample, conv_sample)
```
